```python
import math
import jax, jax.numpy as jnp
from jax import lax
import numpy as np

D_MODEL = 1024
BATCH = 32
SEQ = 2048
DEPTH = 1

GRID_W = 64
CTX_LEN = 256
CONV_DIM = 512
CONV_WIDTH = 31
RET_HEADS = 4
RET_DK = 128
RET_DV = 256
RET_CHUNK = 128
ROPE_BASE = 10000.0
QK_DIM = RET_HEADS * RET_DK
V_DIM = RET_HEADS * RET_DV
N_EXPERTS = 16
EXPERT_HIDDEN = 1024
EC_FACTOR = 2
LN_EPS = 1e-5
DEEPNORM_ALPHA = (2.0 * DEPTH) ** 0.25
DEEPNORM_BETA = (8.0 * DEPTH) ** -0.25
U_END = 2 * CONV_DIM
Q_END = U_END + QK_DIM
K_END = Q_END + QK_DIM
V_END = K_END + V_DIM
G_END = V_END + V_DIM
GA_END = G_END + D_MODEL
IN_COLS = GA_END + D_MODEL

kernel_name = "hybrid_conv_retention_ec_moe_dit"


def _ln(x):
    xf = x.astype(jnp.float32)
    mu = jnp.mean(xf, axis=-1, keepdims=True)
    var = jnp.mean(jnp.square(xf - mu), axis=-1, keepdims=True)
    return ((xf - mu) * lax.rsqrt(var + LN_EPS)).astype(x.dtype)


def layer_norm(x, g, b):
    return _ln(x) * g + b


def modulate(x, shift, scale):
    return _ln(x) * (1.0 + scale) + shift


def to_heads(t, n_heads):
    b, L, _ = t.shape
    return t.reshape(b, L, n_heads, -1).transpose(0, 2, 1, 3)


def rope_2d(t, rows, cols):
    n_pairs_axis = RET_DK // 4
    freqs = ROPE_BASE ** (-jnp.arange(n_pairs_axis, dtype=jnp.float32) / n_pairs_axis)
    ang = jnp.concatenate([rows[:, None] * freqs, cols[:, None] * freqs], axis=-1)
    cos, sin = jnp.cos(ang).astype(t.dtype), jnp.sin(ang).astype(t.dtype)
    t1, t2 = t[..., 0::2], t[..., 1::2]
    return jnp.stack([t1 * cos - t2 * sin, t1 * sin + t2 * cos], axis=-1).reshape(t.shape)


def context_states(k, v, log_gamma_f, log_gamma_b):
    L = k.shape[2]
    t = jnp.arange(L, dtype=jnp.float32)
    w_f = jnp.exp(log_gamma_f.astype(jnp.float32)[:, None] * (L - 1 - t))
    w_b = jnp.exp(log_gamma_b.astype(jnp.float32)[:, None] * t)
    s_f = jnp.einsum('bhtd,ht,bhtv->bhdv', k, w_f, v).astype(jnp.float32)
    s_b = jnp.einsum('bhtd,ht,bhtv->bhdv', k, w_b, v).astype(jnp.float32)
    return s_f, s_b


def retention_chunkwise(q, k, v, log_gamma, s0, inclusive):
    b, h, L, _ = q.shape
    n = L // RET_CHUNK
    lg = log_gamma.astype(jnp.float32)
    pos = jnp.arange(RET_CHUNK, dtype=jnp.float32)
    diff = pos[:, None] - pos[None, :]
    mask = (diff >= 0) if inclusive else (diff > 0)
    intra_dec = jnp.where(mask[None], jnp.exp(lg[:, None, None] * jnp.maximum(diff, 0.0)[None]), 0.0)
    cross_dec = jnp.exp(lg[:, None] * (pos + 1.0))
    state_dec = jnp.exp(lg[:, None] * (RET_CHUNK - 1.0 - pos))
    chunk_dec = jnp.exp(lg * RET_CHUNK)

    def to_chunks(t):
        return t.reshape(b, h, n, RET_CHUNK, t.shape[-1]).transpose(2, 0, 1, 3, 4)

    def step(s, qkv):
        qc, kc, vc = qkv
        att = jnp.einsum('bhid,bhjd->bhij', qc, kc) * intra_dec
        o = (jnp.einsum('bhij,bhjv->bhiv', att, vc)
             + jnp.einsum('bhid,bhdv->bhiv', qc * cross_dec[..., None], s))
        s = s * chunk_dec[:, None, None] + jnp.einsum('bhjd,bhjv->bhdv', kc * state_dec[..., None], vc)
        return s, o

    _, o = lax.scan(step, s0.astype(jnp.float32), (to_chunks(q), to_chunks(k), to_chunks(v)))
    return o.transpose(1, 2, 0, 3, 4).reshape(b, h, L, -1)


def bidir_retention(q, k, v, lg_f, lg_b, s_f, s_b):
    o_f = retention_chunkwise(q, k, v, lg_f, s_f, True)
    flip = lambda t: jnp.flip(t, axis=2)
    o_b = flip(retention_chunkwise(flip(q), flip(k), flip(v), lg_b, s_b, False))
    return o_f + o_b


def head_group_norm(o, g, dtype):
    b, h, L, dv = o.shape
    of = o.astype(jnp.float32)
    mu = jnp.mean(of, axis=-1, keepdims=True)
    var = jnp.mean(jnp.square(of - mu), axis=-1, keepdims=True)
    on = (of - mu) * lax.rsqrt(var + LN_EPS)
    return on.transpose(0, 2, 1, 3).reshape(b, L, h * dv).astype(dtype) * g


def conformer_conv(u, conv_w, conv_b, ln_g, ln_b):
    a, gt = u[..., :CONV_DIM], u[..., CONV_DIM:]
    y = a * jax.nn.sigmoid(gt)
    y = lax.conv_general_dilated(y, conv_w[:, None, :], window_strides=(1,),
                                 padding=[(CONV_WIDTH // 2, CONV_WIDTH // 2)],
                                 dimension_numbers=('NWC', 'WIO', 'NWC'),
                                 feature_group_count=CONV_DIM) + conv_b
    return jax.nn.silu(layer_norm(y, ln_g, ln_b))


def token_mixer(z, rot, s_f, s_b, conv_w, conv_b, conv_ln_g, conv_ln_b, w_conv_out,
                lg_f, lg_b, ret_gn_g, w_ret_out, w_out):
    u = z[..., :U_END]
    q = to_heads(z[..., U_END:Q_END], RET_HEADS)
    k = to_heads(z[..., Q_END:K_END], RET_HEADS) * (RET_DK ** -0.5)
    v = to_heads(z[..., K_END:V_END], RET_HEADS)
    g_ret = z[..., V_END:G_END]
    g_a, g_b = z[..., G_END:GA_END], z[..., GA_END:IN_COLS]
    y_a = conformer_conv(u, conv_w, conv_b, conv_ln_g, conv_ln_b) @ w_conv_out
    if rot is not None:
        q, k = rope_2d(q, *rot), rope_2d(k, *rot)
    o = bidir_retention(q, k, v, lg_f, lg_b, s_f, s_b)
    y_b = (jax.nn.silu(g_ret) * head_group_norm(o, ret_gn_g, z.dtype)) @ w_ret_out
    y = jax.nn.sigmoid(g_a) * y_a + jax.nn.sigmoid(g_b) * y_b
    return y @ w_out


def expert_choice_ffn(h, w_router, w_gate, w_up, w_down):
    b, L, d = h.shape
    cap = EC_FACTOR * L // N_EXPERTS
    aff = jax.nn.softmax((h @ w_router).astype(jnp.float32), axis=-1)
    gates, idx = lax.top_k(aff.transpose(0, 2, 1), cap)
    xe = jax.vmap(lambda hb, ib: hb[ib])(h, idx)
    he = jax.nn.silu(jnp.einsum('becd,edf->becf', xe, w_gate)) * jnp.einsum('becd,edf->becf', xe, w_up)
    ye = jnp.einsum('becf,efd->becd', he, w_down) * gates[..., None].astype(h.dtype)
    return jax.vmap(lambda yb, ib: jnp.zeros((L, d), yb.dtype).at[ib.reshape(-1)].add(yb.reshape(-1, d)))(ye, idx)


def setup_inputs(seed: int = 0) -> dict:
    key = jax.random.key(seed)
    ks = jax.random.split(key, 25)
    f32 = jnp.float32
    nrm = lambda k, shape, s: jax.random.normal(k, shape, f32) * s
    base_decay = jnp.log(1.0 - 2.0 ** (-5.0 - jnp.arange(RET_HEADS, dtype=f32)))
    return {
        "x": nrm(ks[0], (BATCH, SEQ, D_MODEL), 1.0),
        "c": nrm(ks[1], (BATCH, D_MODEL), 1.0),
        "ctx": nrm(ks[2], (BATCH, CTX_LEN, D_MODEL), 1.0),
        "c_ctx": nrm(ks[3], (D_MODEL,), 1.0),
        "w_ada": nrm(ks[4], (DEPTH, D_MODEL, 6 * D_MODEL), 0.5 * D_MODEL ** -0.5),
        "b_ada": nrm(ks[5], (DEPTH, 6 * D_MODEL), 0.02),
        "w_in": nrm(ks[6], (DEPTH, D_MODEL, IN_COLS), D_MODEL ** -0.5),
        "conv_w": nrm(ks[7], (DEPTH, CONV_WIDTH, CONV_DIM), CONV_WIDTH ** -0.5),
        "conv_b": nrm(ks[8], (DEPTH, CONV_DIM), 0.02),
        "conv_ln_g": 1.0 + nrm(ks[9], (DEPTH, CONV_DIM), 0.02),
        "conv_ln_b": nrm(ks[10], (DEPTH, CONV_DIM), 0.02),
        "w_conv_out": nrm(ks[11], (DEPTH, CONV_DIM, D_MODEL), DEEPNORM_BETA * CONV_DIM ** -0.5),
        "log_decay_f": base_decay[None] * jnp.exp(nrm(ks[12], (DEPTH, RET_HEADS), 0.1)),
        "log_decay_b": base_decay[None] * jnp.exp(nrm(ks[13], (DEPTH, RET_HEADS), 0.1)),
        "ret_gn_g": 1.0 + nrm(ks[14], (DEPTH, V_DIM), 0.02),
        "w_ret_out": nrm(ks[15], (DEPTH, V_DIM, D_MODEL), DEEPNORM_BETA * V_DIM ** -0.5),
        "w_out": nrm(ks[16], (DEPTH, D_MODEL, D_MODEL), DEEPNORM_BETA * D_MODEL ** -0.5),
        "ln1_g": 1.0 + nrm(ks[17], (DEPTH, D_MODEL), 0.02),
        "ln1_b": nrm(ks[18], (DEPTH, D_MODEL), 0.02),
        "w_router": nrm(ks[19], (DEPTH, D_MODEL, N_EXPERTS), D_MODEL ** -0.5),
        "w_gate": nrm(ks[20], (DEPTH, N_EXPERTS, D_MODEL, EXPERT_HIDDEN), D_MODEL ** -0.5),
        "w_up": nrm(ks[21], (DEPTH, N_EXPERTS, D_MODEL, EXPERT_HIDDEN), D_MODEL ** -0.5),
        "w_down": nrm(ks[22], (DEPTH, N_EXPERTS, EXPERT_HIDDEN, D_MODEL), DEEPNORM_BETA * EXPERT_HIDDEN ** -0.5),
        "ln2_g": 1.0 + nrm(ks[23], (DEPTH, D_MODEL), 0.02),
        "ln2_b": nrm(ks[24], (DEPTH, D_MODEL), 0.02),
    }


def reference(x, c, ctx, c_ctx, w_ada, b_ada, w_in, conv_w, conv_b, conv_ln_g, conv_ln_b, w_conv_out,
              log_decay_f, log_decay_b, ret_gn_g, w_ret_out, w_out, ln1_g, ln1_b,
              w_router, w_gate, w_up, w_down, ln2_g, ln2_b):
    L = x.shape[1]
    n_rows = L // GRID_W
    rows = jnp.repeat(jnp.arange(n_rows, dtype=jnp.float32), GRID_W)
    cols = jnp.tile(jnp.arange(GRID_W, dtype=jnp.float32), n_rows)
    for l in range(DEPTH):
        last = l == DEPTH - 1
        mod = jax.nn.silu(c) @ w_ada[l] + b_ada[l]
        mod_c = jax.nn.silu(c_ctx) @ w_ada[l] + b_ada[l]
        sh1, sc1, g1, sh2, sc2, g2 = jnp.split(mod[:, None, :], 6, axis=-1)
        csh1, csc1, cg1, csh2, csc2, cg2 = jnp.split(mod_c, 6, axis=-1)
        mix_w = (conv_w[l], conv_b[l], conv_ln_g[l], conv_ln_b[l], w_conv_out[l],
                 log_decay_f[l], log_decay_b[l], ret_gn_g[l], w_ret_out[l], w_out[l])
        h = modulate(x, sh1, sc1)
        h_c = modulate(ctx, csh1, csc1)
        if last:
            kv_c = h_c @ w_in[l][:, Q_END:V_END]
            k_c, v_c = kv_c[..., :QK_DIM], kv_c[..., QK_DIM:]
        else:
            z_c = h_c @ w_in[l]
            k_c, v_c = z_c[..., Q_END:K_END], z_c[..., K_END:V_END]
        s_f, s_b = context_states(to_heads(k_c, RET_HEADS) * (RET_DK ** -0.5), to_heads(v_c, RET_HEADS),
                                  log_decay_f[l], log_decay_b[l])
        y = token_mixer(h @ w_in[l], (rows, cols), s_f, s_b, *mix_w)
        x = layer_norm(DEEPNORM_ALPHA * x + g1 * y, ln1_g[l], ln1_b[l])
        if not last:
            zero_state = jnp.zeros_like(s_f)
            y_c = token_mixer(z_c, None, zero_state, zero_state, *mix_w)
            ctx = layer_norm(DEEPNORM_ALPHA * ctx + cg1 * y_c, ln1_g[l], ln1_b[l])
        h = modulate(x, sh2, sc2)
        x = layer_norm(DEEPNORM_ALPHA * x + g2 * expert_choice_ffn(h, w_router[l], w_gate[l], w_up[l], w_down[l]),
                       ln2_g[l], ln2_b[l])
        if not last:
            h_c = modulate(ctx, csh2, csc2)
            ctx = layer_norm(DEEPNORM_ALPHA * ctx + cg2 * expert_choice_ffn(h_c, w_router[l], w_gate[l], w_up[l], w_down[l]),
                             ln2_g[l], ln2_b[l])
    return x
```

```python
import functools

import numpy as np
import jax
import jax.numpy as jnp
from jax import lax
from jax.experimental import pallas as pl
from jax.experimental.pallas import tpu as pltpu

F32 = jnp.float32
BF16 = jnp.bfloat16

GRID_W = 64
CONV_DIM = 512
CONV_WIDTH = 31
CONV_HALO = CONV_WIDTH // 2
RET_HEADS = 4
RET_DK = 128
RET_DV = 256
RET_CHUNK = 128
ROPE_BASE = 10000.0
QK_DIM = RET_HEADS * RET_DK
V_DIM = RET_HEADS * RET_DV
N_EXPERTS = 16
EC_FACTOR = 2
LN_EPS = 1e-5

ADA_COLS = 768
PROJ_ROWS = 512
MIX_ROWS = 256
CONV_PAD = 16
CONV_BLOCK = 32
OUT_ROWS = 512
V7X_VMEM_LIMIT = 56 * 1024 * 1024


def _dot(a, b):
    return jnp.dot(a, b, preferred_element_type=F32)


def _dot_nt(a, b):
    return lax.dot_general(a, b, (((1,), (1,)), ((), ())), preferred_element_type=F32)


def _dot_tn(a, b):
    return lax.dot_general(a, b, (((0,), (0,)), ((), ())), preferred_element_type=F32)


def _split(a):
    hi = a.astype(BF16)
    lo = (a - hi.astype(F32)).astype(BF16)
    return hi, lo


def _norm(x):
    mu = jnp.mean(x, axis=-1, keepdims=True)
    xc = x - mu
    var = jnp.mean(xc * xc, axis=-1, keepdims=True)
    return xc * lax.rsqrt(var + LN_EPS)


def _sigmoid(x):
    return 1.0 / (1.0 + jnp.exp(-x))


def _silu(x):
    return x * _sigmoid(x)


def _col_iota(n):
    return lax.broadcasted_iota(jnp.int32, (n, 1), 0).astype(F32)


def _smem():
    return pl.BlockSpec(memory_space=pltpu.SMEM)


def _params(sem, vmem=V7X_VMEM_LIMIT):
    return pltpu.CompilerParams(dimension_semantics=sem, vmem_limit_bytes=vmem)


def _ada_kernel(c_ref, w_ref, b_ref, o_ref):
    a_hi, a_lo = _split(_silu(c_ref[...]))
    w_hi, w_lo = _split(w_ref[...])
    o_ref[...] = _dot(a_hi, w_hi) + (_dot(a_hi, w_lo) + _dot(a_lo, w_hi)) + b_ref[...]


def _ada(cc, w, b):
    m, d = cc.shape
    n = w.shape[1]
    return pl.pallas_call(
        _ada_kernel,
        grid=(n // ADA_COLS,),
        in_specs=[pl.BlockSpec((m, d), lambda j: (0, 0)),
                  pl.BlockSpec((d, ADA_COLS), lambda j: (0, j)),
                  pl.BlockSpec((1, ADA_COLS), lambda j: (0, j))],
        out_specs=pl.BlockSpec((m, ADA_COLS), lambda j: (0, j)),
        out_shape=jax.ShapeDtypeStruct((m, n), F32),
        compiler_params=_params(("parallel",)),
        name="ada_proj",
    )(cc, w, b)


def _ctx_kernel(lgf_ref, lgb_ref, ctx_ref, sh_ref, sc_ref, w_ref, sf_ref, sb_ref):
    x = ctx_ref[0]
    lc = x.shape[0]
    h = (_norm(x) * (1.0 + sc_ref[...]) + sh_ref[...]).astype(BF16)
    kv = _dot(h, w_ref[...])
    t = _col_iota(lc)
    for hh in range(RET_HEADS):
        k = kv[:, hh * RET_DK:(hh + 1) * RET_DK] * (RET_DK ** -0.5)
        v = kv[:, QK_DIM + hh * RET_DV:QK_DIM + (hh + 1) * RET_DV].astype(BF16)
        wf = jnp.exp(lgf_ref[hh] * (lc - 1.0 - t))
        wb = jnp.exp(lgb_ref[hh] * t)
        sf_ref[0, hh] = _dot_tn((k * wf).astype(BF16), v)
        sb_ref[0, hh] = _dot_tn((k * wb).astype(BF16), v)


def _ctx_states(lgf, lgb, ctx, csh, csc, w_kvu):
    b, lc, d = ctx.shape
    kvw = QK_DIM + V_DIM
    st = jax.ShapeDtypeStruct((b, RET_HEADS, RET_DK, RET_DV), F32)
    st_spec = pl.BlockSpec((1, RET_HEADS, RET_DK, RET_DV), lambda i: (i, 0, 0, 0))
    return pl.pallas_call(
        _ctx_kernel,
        grid=(b,),
        in_specs=[_smem(), _smem(),
                  pl.BlockSpec((1, lc, d), lambda i: (i, 0, 0)),
                  pl.BlockSpec((1, d), lambda i: (0, 0)),
                  pl.BlockSpec((1, d), lambda i: (0, 0)),
                  pl.BlockSpec((d, kvw), lambda i: (0, 0))],
        out_specs=(st_spec, st_spec),
        out_shape=(st, st),
        compiler_params=_params(("parallel",)),
        name="ctx_states",
    )(lgf, lgb, ctx, csh, csc, w_kvu)


def _proj_kernel(lgf_ref, xp_ref, x_ref, xn_ref, sh_ref, sc_ref, w_ref, ck_ref, sk_ref, sf0_ref,
                 cw_ref, cb_ref, clg_ref, clb_ref,
                 h_ref, k_ref, v_ref, ya_ref, sf_ref, s_scr, y_scr, yres_scr):
    t = pl.program_id(1)
    nt = pl.num_programs(1)

    @pl.when(t == 0)
    def _():
        s_scr[...] = sf0_ref[0]

    rows = x_ref.shape[1]
    ext = rows + 2 * CONV_PAD
    x_ext = jnp.concatenate([xp_ref[0], x_ref[0], xn_ref[0]], axis=0)
    h_ext = (_norm(x_ext) * (1.0 + sc_ref[0]) + sh_ref[0]).astype(BF16)
    h = h_ext[CONV_PAD:CONV_PAD + rows]
    h_ref[0] = h
    kvw = QK_DIM + V_DIM
    u = _dot(h_ext, w_ref[:, kvw:])
    row = lax.broadcasted_iota(jnp.int32, (ext, 1), 0)
    head_ok = jnp.where(t == 0, 0.0, 1.0)
    tail_ok = jnp.where(t == nt - 1, 0.0, 1.0)
    inside = jnp.where(row < CONV_PAD, head_ok, jnp.where(row >= CONV_PAD + rows, tail_ok, 1.0))
    y_scr[...] = u[:, :CONV_DIM] * _sigmoid(u[:, CONV_DIM:]) * inside

    kk = _dot(h, w_ref[:, :QK_DIM])
    vv = _dot(h, w_ref[:, QK_DIM:kvw]).astype(BF16)
    v_ref[0] = vv
    ck = ck_ref[...]
    sk = sk_ref[...]
    pos = _col_iota(RET_CHUNK)
    for hh in range(RET_HEADS):
        lg = lgf_ref[hh]
        kh = kk[:, hh * RET_DK:(hh + 1) * RET_DK]
        kr = kh * ck + pltpu.roll(kh, RET_DK // 2, 1) * sk
        k_ref[0, :, hh * RET_DK:(hh + 1) * RET_DK] = kr.astype(BF16)
        state_dec = jnp.exp(lg * (RET_CHUNK - 1.0 - pos))
        chunk_dec = jnp.exp(lg * jnp.full((1, RET_DV), float(RET_CHUNK), F32))
        for c in range(rows // RET_CHUNK):
            r0 = c * RET_CHUNK
            s = s_scr[hh]
            sf_ref[0, c, hh] = s.astype(BF16)
            kc = (kr[r0:r0 + RET_CHUNK] * state_dec).astype(BF16)
            vc = vv[r0:r0 + RET_CHUNK, hh * RET_DV:(hh + 1) * RET_DV]
            s_scr[hh] = s * chunk_dec + _dot_tn(kc, vc)

    span = ext - 8
    for r in range(8):
        yres_scr[r] = y_scr[pl.ds(r, span), :]

    def conv_block(i, carry):
        base = pl.multiple_of(i * CONV_BLOCK, CONV_BLOCK)
        acc = jnp.zeros((CONV_BLOCK, CONV_DIM), F32)
        for w in range(CONV_WIDTH):
            a, r = divmod(w + 1, 8)
            acc = acc + yres_scr[r, pl.ds(base + 8 * a, CONV_BLOCK), :] * cw_ref[w:w + 1, :]
        z = _norm(acc + cb_ref[...]) * clg_ref[...] + clb_ref[...]
        ya_ref[0, pl.ds(base, CONV_BLOCK), :] = _silu(z).astype(BF16)
        return carry

    lax.fori_loop(0, rows // CONV_BLOCK, conv_block, 0)


def _proj(lgf, x, sh1, sc1, w_kvu, ck, sk, sf0, cw, cb, clg, clb):
    b, l, d = x.shape
    nt = l // PROJ_ROWS
    cpt = PROJ_ROWS // RET_CHUNK
    hpt = PROJ_ROWS // CONV_PAD
    n_halo = l // CONV_PAD
    ext = PROJ_ROWS + 2 * CONV_PAD
    tile = lambda w: pl.BlockSpec((1, PROJ_ROWS, w), lambda i, t: (i, t, 0))
    vec = pl.BlockSpec((1, 1, d), lambda i, t: (i, 0, 0))
    full = lambda a: pl.BlockSpec(a.shape, lambda i, t: (0,) * a.ndim)
    rope = pl.BlockSpec((PROJ_ROWS, RET_DK), lambda i, t: (t, 0))
    prev = pl.BlockSpec((1, CONV_PAD, d), lambda i, t: (i, jnp.maximum(t * hpt - 1, 0), 0))
    nxt = pl.BlockSpec((1, CONV_PAD, d), lambda i, t: (i, jnp.minimum((t + 1) * hpt, n_halo - 1), 0))
    return pl.pallas_call(
        _proj_kernel,
        grid=(b, nt),
        in_specs=[_smem(), prev, tile(d), nxt, vec, vec, full(w_kvu), rope, rope,
                  pl.BlockSpec((1, RET_HEADS, RET_DK, RET_DV), lambda i, t: (i, 0, 0, 0)),
                  full(cw), full(cb), full(clg), full(clb)],
        out_specs=(tile(d), tile(QK_DIM), tile(V_DIM), tile(CONV_DIM),
                   pl.BlockSpec((1, cpt, RET_HEADS, RET_DK, RET_DV), lambda i, t: (i, t, 0, 0, 0))),
        out_shape=(jax.ShapeDtypeStruct((b, l, d), BF16),
                   jax.ShapeDtypeStruct((b, l, QK_DIM), BF16),
                   jax.ShapeDtypeStruct((b, l, V_DIM), BF16),
                   jax.ShapeDtypeStruct((b, l, CONV_DIM), BF16),
                   jax.ShapeDtypeStruct((b, l // RET_CHUNK, RET_HEADS, RET_DK, RET_DV), BF16)),
        scratch_shapes=[pltpu.VMEM((RET_HEADS, RET_DK, RET_DV), F32),
                        pltpu.VMEM((ext, CONV_DIM), F32),
                        pltpu.VMEM((8, ext - 8, CONV_DIM), F32)],
        compiler_params=_params(("parallel", "arbitrary")),
        name="proj_kv_conv",
    )(lgf, x, x, x, sh1, sc1, w_kvu, ck, sk, sf0, cw, cb, clg, clb)


def _mixer_kernel(alpha, lgf_ref, lgb_ref, x_ref, h_ref, k_ref, v_ref, sf_ref, sb0_ref, ya_ref,
                  wq_ref, wg_ref, wco_ref, wro_ref, wo_ref, gng_ref,
                  g1_ref, sh2_ref, sc2_ref, l1g_ref, l1b_ref, wr_ref, cq_ref, sq_ref,
                  x1_ref, h2_ref, lt_ref, sb_scr):
    t = pl.program_id(1)

    @pl.when(t == 0)
    def _():
        sb_scr[...] = sb0_ref[0]

    h = h_ref[0]
    rows = h.shape[0]
    cq = cq_ref[...]
    sq = sq_ref[...]
    q_all = _dot(h, wq_ref[...])
    pos = _col_iota(RET_CHUNK)
    di = lax.broadcasted_iota(jnp.int32, (RET_CHUNK, RET_CHUNK), 0)
    dj = lax.broadcasted_iota(jnp.int32, (RET_CHUNK, RET_CHUNK), 1)
    dist = (di - dj).astype(F32)

    n_chunks = rows // RET_CHUNK
    o_parts = [[None] * RET_HEADS for _ in range(n_chunks)]
    for hh in range(RET_HEADS):
        lgf = lgf_ref[hh]
        lgb = lgb_ref[hh]
        qh = q_all[:, hh * RET_DK:(hh + 1) * RET_DK]
        qr = qh * cq + pltpu.roll(qh, RET_DK // 2, 1) * sq
        decay = jnp.where(dist >= 0.0, jnp.exp(lgf * jnp.maximum(dist, 0.0)),
                          jnp.exp(lgb * jnp.maximum(-dist, 0.0)))
        cross_f = jnp.exp(lgf * (pos + 1.0))
        cross_b = jnp.exp(lgb * (RET_CHUNK - pos))
        state_b = jnp.exp(lgb * pos)
        chunk_b = jnp.exp(lgb * jnp.full((1, RET_DV), float(RET_CHUNK), F32))
        for c in reversed(range(n_chunks)):
            r0 = c * RET_CHUNK
            qc = qr[r0:r0 + RET_CHUNK]
            kc = k_ref[0, r0:r0 + RET_CHUNK, hh * RET_DK:(hh + 1) * RET_DK]
            vc = v_ref[0, r0:r0 + RET_CHUNK, hh * RET_DV:(hh + 1) * RET_DV]
            sb = sb_scr[hh]
            att = _dot_nt(qc.astype(BF16), kc) * decay
            o = (_dot(att.astype(BF16), vc)
                 + _dot((qc * cross_f).astype(BF16), sf_ref[0, c, hh])
                 + _dot((qc * cross_b).astype(BF16), sb.astype(BF16)))
            kb = (kc.astype(F32) * state_b).astype(BF16)
            sb_scr[hh] = sb * chunk_b + _dot_tn(kb, vc)
            o_parts[c][hh] = _norm(o)
    on = jnp.concatenate([jnp.concatenate(o_parts[c], axis=1) for c in range(n_chunks)], axis=0)

    g_ret = _dot(h, wg_ref[:, :V_DIM])
    yb_in = (_silu(g_ret) * (on * gng_ref[...])).astype(BF16)
    y_b = _dot(yb_in, wro_ref[...])

    y_a = _dot(ya_ref[0], wco_ref[...])

    d = x_ref.shape[-1]
    g_a = _dot(h, wg_ref[:, V_DIM:V_DIM + d])
    g_b = _dot(h, wg_ref[:, V_DIM + d:])
    y = (_sigmoid(g_a) * y_a + _sigmoid(g_b) * y_b).astype(BF16)
    ym = _dot(y, wo_ref[...])

    x1 = _norm(alpha * x_ref[0] + g1_ref[0] * ym) * l1g_ref[...] + l1b_ref[...]
    x1_ref[0] = x1
    h2 = _norm(x1) * (1.0 + sc2_ref[0]) + sh2_ref[0]
    h2_hi, h2_lo = _split(h2)
    h2_ref[0] = h2_hi
    wr_hi, wr_lo = _split(wr_ref[...])
    lt_ref[0] = _dot_nt(wr_hi, h2_hi) + (_dot_nt(wr_hi, h2_lo) + _dot_nt(wr_lo, h2_hi))


def _mixer(alpha, lgf, lgb, x, h, k, v, sf, sb0, ya, wq, wg, wco, wro, wo, gng,
           g1, sh2, sc2, l1g, l1b, wr_t, cq, sq):
    b, l, d = x.shape
    nt = l // MIX_ROWS
    cpt = MIX_ROWS // RET_CHUNK
    rev = lambda w: pl.BlockSpec((1, MIX_ROWS, w), lambda i, t: (i, nt - 1 - t, 0))
    vec = pl.BlockSpec((1, 1, d), lambda i, t: (i, 0, 0))
    full = lambda a: pl.BlockSpec(a.shape, lambda i, t: (0,) * a.ndim)
    rope = pl.BlockSpec((MIX_ROWS, RET_DK), lambda i, t: (nt - 1 - t, 0))
    state = pl.BlockSpec((1, RET_HEADS, RET_DK, RET_DV), lambda i, t: (i, 0, 0, 0))
    return pl.pallas_call(
        functools.partial(_mixer_kernel, alpha),
        grid=(b, nt),
        in_specs=[_smem(), _smem(), rev(d), rev(d), rev(QK_DIM), rev(V_DIM),
                  pl.BlockSpec((1, cpt, RET_HEADS, RET_DK, RET_DV), lambda i, t: (i, nt - 1 - t, 0, 0, 0)),
                  state, rev(CONV_DIM),
                  full(wq), full(wg), full(wco), full(wro), full(wo), full(gng),
                  vec, vec, vec, full(l1g), full(l1b), full(wr_t), rope, rope],
        out_specs=(rev(d), rev(d),
                   pl.BlockSpec((1, N_EXPERTS, MIX_ROWS), lambda i, t: (i, 0, nt - 1 - t))),
        out_shape=(jax.ShapeDtypeStruct((b, l, d), F32),
                   jax.ShapeDtypeStruct((b, l, d), BF16),
                   jax.ShapeDtypeStruct((b, N_EXPERTS, l), F32)),
        scratch_shapes=[pltpu.VMEM((RET_HEADS, RET_DK, RET_DV), F32)],
        compiler_params=_params(("parallel", "arbitrary")),
        name="mixer",
    )(lgf, lgb, x, h, k, v, sf, sb0, ya, wq, wg, wco, wro, wo, gng,
      g1, sh2, sc2, l1g, l1b, wr_t, cq, sq)


def _route_kernel(cap, lt_ref, rank_ref, aff_ref):
    logits = lt_ref[0]
    n_e, l = logits.shape
    m = jnp.max(logits, axis=0, keepdims=True)
    p = jnp.exp(logits - m)
    aff = p / jnp.sum(p, axis=0, keepdims=True)
    aff_ref[0] = aff
    capf = float(cap)

    def count(ones):
        return jnp.sum(ones, axis=1, keepdims=True)

    def value_step(i, lo):
        cand = lo | jnp.left_shift(jnp.int32(1), 30 - i)
        n_ge = count(jnp.where(aff >= pltpu.bitcast(cand, F32), 1.0, 0.0))
        return jnp.where(n_ge >= capf, cand, lo)

    thr = pltpu.bitcast(lax.fori_loop(0, 31, value_step, jnp.zeros((n_e, 1), jnp.int32)), F32)
    gt = jnp.where(aff > thr, 1.0, 0.0)
    eq = jnp.where(aff == thr, 1.0, 0.0)
    need = capf - count(gt)
    idx = lax.broadcasted_iota(jnp.int32, (n_e, l), 1)
    idx_bits = int(l - 1).bit_length()

    def index_step(i, lo):
        cand = lo | jnp.left_shift(jnp.int32(1), idx_bits - 1 - i)
        below = count(jnp.where(idx < cand, eq, 0.0))
        return jnp.where(below < need, cand, lo)

    last = lax.fori_loop(0, idx_bits, index_step, jnp.zeros((n_e, 1), jnp.int32))
    bound = jnp.where(need > 0.0, last + 1, 0)
    sel = gt + jnp.where(idx < bound, eq, 0.0)

    blk = 128
    ti = lax.broadcasted_iota(jnp.int32, (blk, blk), 0)
    tj = lax.broadcasted_iota(jnp.int32, (blk, blk), 1)
    tri = jnp.where(ti < tj, 1.0, 0.0).astype(BF16)
    offset = jnp.zeros((n_e, 1), F32)
    for j in range(l // blk):
        sj = sel[:, j * blk:(j + 1) * blk]
        before = _dot(sj.astype(BF16), tri) + offset
        rank_ref[0, :, j * blk:(j + 1) * blk] = jnp.where(sj > 0.0, before, -1.0).astype(jnp.int32)
        offset = offset + jnp.sum(sj, axis=1, keepdims=True)


def _route(lt, cap):
    b, n_e, l = lt.shape
    spec = pl.BlockSpec((1, n_e, l), lambda i: (i, 0, 0))
    return pl.pallas_call(
        functools.partial(_route_kernel, cap),
        grid=(b,),
        in_specs=[spec],
        out_specs=(spec, spec),
        out_shape=(jax.ShapeDtypeStruct((b, n_e, l), jnp.int32),
                   jax.ShapeDtypeStruct((b, n_e, l), F32)),
        compiler_params=_params(("parallel",)),
        name="route_topc",
    )(lt)


def _moe_kernel(cap, alpha, rank_ref, aff_ref, h2_ref, wg_ref, wu_ref, wd_ref,
                x1_ref, g2_ref, lng_ref, lnb_ref, o_ref):
    e = pl.program_id(1)
    l = o_ref.shape[1]
    n_tiles = l // OUT_ROWS

    @pl.when(e == 0)
    def _():
        o_ref[0] = jnp.zeros(o_ref.shape[1:], F32)

    rank = rank_ref[0, 0]
    slot = lax.broadcasted_iota(jnp.int32, (cap, l), 0)
    hit = slot == rank
    pick = jnp.where(hit, 1.0, 0.0).astype(BF16)
    gate = jnp.sum(jnp.where(hit, aff_ref[0, 0], 0.0), axis=1, keepdims=True)
    xe = _dot(pick, h2_ref[0]).astype(BF16)
    he = (_silu(_dot(xe, wg_ref[0])) * _dot(xe, wu_ref[0])).astype(BF16)
    ye = (_dot(he, wd_ref[0]) * gate).astype(BF16)
    for i in range(n_tiles):
        r0 = i * OUT_ROWS
        o_ref[0, r0:r0 + OUT_ROWS, :] += _dot_tn(pick[:, r0:r0 + OUT_ROWS], ye)

    @pl.when(e == pl.num_programs(1) - 1)
    def _():
        def tile(i, carry):
            r0 = pl.multiple_of(i * OUT_ROWS, OUT_ROWS)
            rows = pl.ds(r0, OUT_ROWS)
            z = alpha * x1_ref[0, rows, :] + g2_ref[0] * o_ref[0, rows, :]
            o_ref[0, rows, :] = _norm(z) * lng_ref[...] + lnb_ref[...]
            return carry

        lax.fori_loop(0, n_tiles, tile, 0)


def _moe(alpha, rank, aff, h2, wgate, wup, wdown, x1, g2, ln_g, ln_b, cap):
    b, l, d = h2.shape
    n_e = wgate.shape[0]
    f = wgate.shape[2]
    row = pl.BlockSpec((1, 1, 1, l), lambda i, e: (i, e, 0, 0))
    vec = pl.BlockSpec((1, d), lambda i, e: (0, 0))
    once = pl.Buffered(1)
    return pl.pallas_call(
        functools.partial(_moe_kernel, cap, alpha),
        grid=(b, n_e),
        in_specs=[row, row,
                  pl.BlockSpec((1, l, d), lambda i, e: (i, 0, 0), pipeline_mode=once),
                  pl.BlockSpec((1, d, f), lambda i, e: (e, 0, 0)),
                  pl.BlockSpec((1, d, f), lambda i, e: (e, 0, 0)),
                  pl.BlockSpec((1, f, d), lambda i, e: (e, 0, 0)),
                  pl.BlockSpec((1, l, d), lambda i, e: (i, 0, 0), pipeline_mode=once),
                  pl.BlockSpec((1, 1, d), lambda i, e: (i, 0, 0)), vec, vec],
        out_specs=pl.BlockSpec((1, l, d), lambda i, e: (i, 0, 0)),
        out_shape=jax.ShapeDtypeStruct((b, l, d), F32),
        compiler_params=_params(("parallel", "arbitrary")),
        name="moe_experts",
    )(rank.reshape(b, n_e, 1, l), aff.reshape(b, n_e, 1, l), h2, wgate, wup, wdown, x1, g2, ln_g, ln_b)


def _rope_tables(l):
    n_axis = RET_DK // 4
    freqs = ROPE_BASE ** (-np.arange(n_axis, dtype=np.float64) / n_axis)
    pos = np.arange(l)
    ang = np.concatenate([(pos // GRID_W)[:, None] * freqs, (pos % GRID_W)[:, None] * freqs], axis=-1)
    cos, sin = np.cos(ang), np.sin(ang)
    return (np.concatenate([cos, cos], axis=-1).astype(np.float32),
            np.concatenate([-sin, sin], axis=-1).astype(np.float32))


def kernel(x, c, ctx, c_ctx, w_ada, b_ada, w_in, conv_w, conv_b, conv_ln_g, conv_ln_b, w_conv_out,
           log_decay_f, log_decay_b, ret_gn_g, w_ret_out, w_out, ln1_g, ln1_b,
           w_router, w_gate, w_up, w_down, ln2_g, ln2_b):
    depth = w_ada.shape[0]
    assert depth == 1, "single trunk layer"
    b, l, d = x.shape
    alpha = (2.0 * depth) ** 0.25
    cap = EC_FACTOR * l // N_EXPERTS
    u_end = 2 * CONV_DIM
    q_end = u_end + QK_DIM
    k_end = q_end + QK_DIM
    v_end = k_end + V_DIM
    row = lambda a: a.reshape(1, -1)

    n_mod = b + 1
    pad = (-n_mod) % 8
    cc = jnp.concatenate([c, c_ctx[None], jnp.zeros((pad, d), F32)], axis=0)
    mod = _ada(cc, w_ada[0], row(b_ada[0]))
    sh1, sc1, g1, sh2, sc2, g2 = [m.reshape(b, 1, d) for m in jnp.split(mod[:b], 6, axis=-1)]
    csh1, csc1 = mod[b:b + 1, :d], mod[b:b + 1, d:2 * d]

    perm = np.concatenate([np.arange(0, RET_DK, 2), np.arange(1, RET_DK, 2)])
    perm = (np.arange(RET_HEADS)[:, None] * RET_DK + perm[None, :]).reshape(-1)
    w = w_in[0]
    w_kvu = jnp.concatenate([w[:, q_end:k_end][:, perm], w[:, k_end:v_end], w[:, :u_end]], axis=1).astype(BF16)
    wq = w[:, u_end:q_end][:, perm].astype(BF16)
    wg = w[:, v_end:].astype(BF16)

    cos_t, sin_t = _rope_tables(l)
    cq, sq = jnp.asarray(cos_t), jnp.asarray(sin_t)
    k_scale = RET_DK ** -0.5
    ck, sk = jnp.asarray(cos_t * k_scale), jnp.asarray(sin_t * k_scale)

    lgf, lgb = log_decay_f[0], log_decay_b[0]
    sf0, sb0 = _ctx_states(lgf, lgb, ctx, csh1, csc1, w_kvu)
    h, k, v, ya, sf = _proj(lgf, x, sh1, sc1, w_kvu, ck, sk, sf0,
                            conv_w[0], row(conv_b[0]), row(conv_ln_g[0]), row(conv_ln_b[0]))
    x1, h2, lt = _mixer(alpha, lgf, lgb, x, h, k, v, sf, sb0, ya, wq, wg,
                        w_conv_out[0].astype(BF16), w_ret_out[0].astype(BF16), w_out[0].astype(BF16),
                        row(ret_gn_g[0]), g1, sh2, sc2, row(ln1_g[0]), row(ln1_b[0]), w_router[0].T, cq, sq)
    rank, aff = _route(lt, cap)
    return _moe(alpha, rank, aff, h2, w_gate[0].astype(BF16), w_up[0].astype(BF16), w_down[0].astype(BF16),
                x1, g2, row(ln2_g[0]), row(ln2_b[0]), cap)
```

```python
import functools

import numpy as np
import jax
import jax.numpy as jnp
from jax import lax
from jax.experimental import pallas as pl
from jax.experimental.pallas import tpu as pltpu

F32 = jnp.float32
BF16 = jnp.bfloat16

GRID_W = 64
CONV_DIM = 512
CONV_WIDTH = 31
CONV_HALO = CONV_WIDTH // 2
RET_HEADS = 4
RET_DK = 128
RET_DV = 256
RET_CHUNK = 128
ROPE_BASE = 10000.0
QK_DIM = RET_HEADS * RET_DK
V_DIM = RET_HEADS * RET_DV
N_EXPERTS = 16
EC_FACTOR = 2
LN_EPS = 1e-5

ADA_COLS = 768
PROJ_ROWS = 512
MIX_ROWS = 512
CONV_PAD = 16
CONV_BLOCK = 32
OUT_ROWS = 512
MOE_GROUPS = 2
V7X_VMEM_LIMIT = 56 * 1024 * 1024


def _dot(a, b):
    return jnp.dot(a, b, preferred_element_type=F32)


def _dot_nt(a, b):
    return lax.dot_general(a, b, (((1,), (1,)), ((), ())), preferred_element_type=F32)


def _dot_tn(a, b):
    return lax.dot_general(a, b, (((0,), (0,)), ((), ())), preferred_element_type=F32)


def _split(a):
    hi = a.astype(BF16)
    lo = (a - hi.astype(F32)).astype(BF16)
    return hi, lo


def _norm(x):
    mu = jnp.mean(x, axis=-1, keepdims=True)
    xc = x - mu
    var = jnp.mean(xc * xc, axis=-1, keepdims=True)
    return xc * lax.rsqrt(var + LN_EPS)


def _sigmoid(x):
    return 1.0 / (1.0 + jnp.exp(-x))


def _silu(x):
    return x * _sigmoid(x)


def _col_iota(n):
    return lax.broadcasted_iota(jnp.int32, (n, 1), 0).astype(F32)


def _smem():
    return pl.BlockSpec(memory_space=pltpu.SMEM)


def _params(sem, vmem=V7X_VMEM_LIMIT):
    return pltpu.CompilerParams(dimension_semantics=sem, vmem_limit_bytes=vmem)


def _ada_kernel(c_ref, w_ref, b_ref, o_ref):
    a_hi, a_lo = _split(_silu(c_ref[...]))
    w_hi, w_lo = _split(w_ref[...])
    o_ref[...] = _dot(a_hi, w_hi) + (_dot(a_hi, w_lo) + _dot(a_lo, w_hi)) + b_ref[...]


def _ada(cc, w, b):
    m, d = cc.shape
    n = w.shape[1]
    return pl.pallas_call(
        _ada_kernel,
        grid=(n // ADA_COLS,),
        in_specs=[pl.BlockSpec((m, d), lambda j: (0, 0)),
                  pl.BlockSpec((d, ADA_COLS), lambda j: (0, j)),
                  pl.BlockSpec((1, ADA_COLS), lambda j: (0, j))],
        out_specs=pl.BlockSpec((m, ADA_COLS), lambda j: (0, j)),
        out_shape=jax.ShapeDtypeStruct((m, n), F32),
        compiler_params=_params(("parallel",)),
        name="ada_proj",
    )(cc, w, b)


def _ctx_kernel(lgf_ref, lgb_ref, ctx_ref, sh_ref, sc_ref, w_ref, sf_ref, sb_ref):
    x = ctx_ref[0]
    lc = x.shape[0]
    h = (_norm(x) * (1.0 + sc_ref[...]) + sh_ref[...]).astype(BF16)
    kv = _dot(h, w_ref[...])
    t = _col_iota(lc)
    for hh in range(RET_HEADS):
        k = kv[:, hh * RET_DK:(hh + 1) * RET_DK] * (RET_DK ** -0.5)
        v = kv[:, QK_DIM + hh * RET_DV:QK_DIM + (hh + 1) * RET_DV].astype(BF16)
        wf = jnp.exp(lgf_ref[hh] * (lc - 1.0 - t))
        wb = jnp.exp(lgb_ref[hh] * t)
        sf_ref[0, hh] = _dot_tn((k * wf).astype(BF16), v)
        sb_ref[0, hh] = _dot_tn((k * wb).astype(BF16), v)


def _ctx_states(lgf, lgb, ctx, csh, csc, w_kvu):
    b, lc, d = ctx.shape
    kvw = QK_DIM + V_DIM
    st = jax.ShapeDtypeStruct((b, RET_HEADS, RET_DK, RET_DV), F32)
    st_spec = pl.BlockSpec((1, RET_HEADS, RET_DK, RET_DV), lambda i: (i, 0, 0, 0))
    return pl.pallas_call(
        _ctx_kernel,
        grid=(b,),
        in_specs=[_smem(), _smem(),
                  pl.BlockSpec((1, lc, d), lambda i: (i, 0, 0)),
                  pl.BlockSpec((1, d), lambda i: (0, 0)),
                  pl.BlockSpec((1, d), lambda i: (0, 0)),
                  pl.BlockSpec((d, kvw), lambda i: (0, 0))],
        out_specs=(st_spec, st_spec),
        out_shape=(st, st),
        compiler_params=_params(("parallel",)),
        name="ctx_states",
    )(lgf, lgb, ctx, csh, csc, w_kvu)


def _proj_kernel(lgf_ref, xp_ref, x_ref, xn_ref, sh_ref, sc_ref, w_ref, ck_ref, sk_ref, sf0_ref,
                 cw_ref, cb_ref, clg_ref, clb_ref,
                 h_ref, k_ref, v_ref, ya_ref, sf_ref, s_scr, y_scr, yres_scr, z_scr):
    t = pl.program_id(1)
    nt = pl.num_programs(1)

    @pl.when(t == 0)
    def _():
        s_scr[...] = sf0_ref[0]

    rows = x_ref.shape[1]
    ext = rows + 2 * CONV_PAD
    x_ext = jnp.concatenate([xp_ref[0], x_ref[0], xn_ref[0]], axis=0)
    h_ext = (_norm(x_ext) * (1.0 + sc_ref[0]) + sh_ref[0]).astype(BF16)
    h = h_ext[CONV_PAD:CONV_PAD + rows]
    h_ref[0] = h
    kvw = QK_DIM + V_DIM
    u = _dot(h_ext, w_ref[:, kvw:])
    row = lax.broadcasted_iota(jnp.int32, (ext, 1), 0)
    head_ok = jnp.where(t == 0, 0.0, 1.0)
    tail_ok = jnp.where(t == nt - 1, 0.0, 1.0)
    inside = jnp.where(row < CONV_PAD, head_ok, jnp.where(row >= CONV_PAD + rows, tail_ok, 1.0))
    y_scr[...] = u[:, :CONV_DIM] * _sigmoid(u[:, CONV_DIM:]) * inside

    kk = _dot(h, w_ref[:, :QK_DIM])
    vv = _dot(h, w_ref[:, QK_DIM:kvw]).astype(BF16)
    v_ref[0] = vv
    ck = ck_ref[...]
    sk = sk_ref[...]
    pos = _col_iota(RET_CHUNK)
    for hh in range(RET_HEADS):
        lg = lgf_ref[hh]
        kh = kk[:, hh * RET_DK:(hh + 1) * RET_DK]
        kr = kh * ck + pltpu.roll(kh, RET_DK // 2, 1) * sk
        k_ref[0, :, hh * RET_DK:(hh + 1) * RET_DK] = kr.astype(BF16)
        state_dec = jnp.exp(lg * (RET_CHUNK - 1.0 - pos))
        chunk_dec = jnp.exp(lg * jnp.full((1, RET_DV), float(RET_CHUNK), F32))
        for c in range(rows // RET_CHUNK):
            r0 = c * RET_CHUNK
            s = s_scr[hh]
            sf_ref[0, c, hh] = s.astype(BF16)
            kc = (kr[r0:r0 + RET_CHUNK] * state_dec).astype(BF16)
            vc = vv[r0:r0 + RET_CHUNK, hh * RET_DV:(hh + 1) * RET_DV]
            s_scr[hh] = s * chunk_dec + _dot_tn(kc, vc)

    span = ext - 8
    for r in range(8):
        yres_scr[r] = y_scr[pl.ds(r, span), :]

    for i in range(rows // CONV_BLOCK):
        base = i * CONV_BLOCK
        acc = None
        for w in range(CONV_WIDTH):
            a, r = divmod(w + 1, 8)
            wk = jnp.concatenate([cw_ref[8 * w:8 * w + 8, :]] * (CONV_BLOCK // 8), axis=0)
            tap = yres_scr[r, base + 8 * a:base + 8 * a + CONV_BLOCK, :] * wk
            acc = tap if acc is None else acc + tap
        z_scr[base:base + CONV_BLOCK, :] = acc
    z = _norm(z_scr[...] + cb_ref[...]) * clg_ref[...] + clb_ref[...]
    ya_ref[0] = _silu(z).astype(BF16)


def _proj(lgf, x, sh1, sc1, w_kvu, ck, sk, sf0, cw, cb, clg, clb):
    b, l, d = x.shape
    nt = l // PROJ_ROWS
    cpt = PROJ_ROWS // RET_CHUNK
    hpt = PROJ_ROWS // CONV_PAD
    n_halo = l // CONV_PAD
    ext = PROJ_ROWS + 2 * CONV_PAD
    tile = lambda w: pl.BlockSpec((1, PROJ_ROWS, w), lambda i, t: (i, t, 0))
    vec = pl.BlockSpec((1, 1, d), lambda i, t: (i, 0, 0))
    full = lambda a: pl.BlockSpec(a.shape, lambda i, t: (0,) * a.ndim, pipeline_mode=pl.Buffered(1))
    rope = pl.BlockSpec((PROJ_ROWS, RET_DK), lambda i, t: (t, 0))
    prev = pl.BlockSpec((1, CONV_PAD, d), lambda i, t: (i, jnp.maximum(t * hpt - 1, 0), 0))
    nxt = pl.BlockSpec((1, CONV_PAD, d), lambda i, t: (i, jnp.minimum((t + 1) * hpt, n_halo - 1), 0))
    return pl.pallas_call(
        _proj_kernel,
        grid=(b, nt),
        in_specs=[_smem(), prev, tile(d), nxt, vec, vec, full(w_kvu), rope, rope,
                  pl.BlockSpec((1, RET_HEADS, RET_DK, RET_DV), lambda i, t: (i, 0, 0, 0)),
                  full(cw), full(cb), full(clg), full(clb)],
        out_specs=(tile(d), tile(QK_DIM), tile(V_DIM), tile(CONV_DIM),
                   pl.BlockSpec((1, cpt, RET_HEADS, RET_DK, RET_DV), lambda i, t: (i, t, 0, 0, 0))),
        out_shape=(jax.ShapeDtypeStruct((b, l, d), BF16),
                   jax.ShapeDtypeStruct((b, l, QK_DIM), BF16),
                   jax.ShapeDtypeStruct((b, l, V_DIM), BF16),
                   jax.ShapeDtypeStruct((b, l, CONV_DIM), BF16),
                   jax.ShapeDtypeStruct((b, l // RET_CHUNK, RET_HEADS, RET_DK, RET_DV), BF16)),
        scratch_shapes=[pltpu.VMEM((RET_HEADS, RET_DK, RET_DV), F32),
                        pltpu.VMEM((ext, CONV_DIM), F32),
                        pltpu.VMEM((8, ext - 8, CONV_DIM), F32),
                        pltpu.VMEM((PROJ_ROWS, CONV_DIM), F32)],
        compiler_params=_params(("parallel", "arbitrary")),
        name="proj_kv_conv",
    )(lgf, x, x, x, sh1, sc1, w_kvu, ck, sk, sf0, cw, cb, clg, clb)


def _mixer_kernel(alpha, lgf_ref, lgb_ref, x_ref, h_ref, k_ref, v_ref, sf_ref, sb0_ref, ya_ref,
                  wq_ref, wg_ref, wco_ref, wro_ref, wo_ref, gng_ref,
                  g1_ref, sh2_ref, sc2_ref, l1g_ref, l1b_ref, wr_ref, cq_ref, sq_ref,
                  x1_ref, h2_ref, lt_ref, sb_scr):
    t = pl.program_id(1)

    @pl.when(t == 0)
    def _():
        sb_scr[...] = sb0_ref[0]

    h = h_ref[0]
    rows = h.shape[0]
    cq = cq_ref[...]
    sq = sq_ref[...]
    q_all = _dot(h, wq_ref[...])
    pos = _col_iota(RET_CHUNK)
    di = lax.broadcasted_iota(jnp.int32, (RET_CHUNK, RET_CHUNK), 0)
    dj = lax.broadcasted_iota(jnp.int32, (RET_CHUNK, RET_CHUNK), 1)
    dist = (di - dj).astype(F32)

    n_chunks = rows // RET_CHUNK
    o_parts = [[None] * RET_HEADS for _ in range(n_chunks)]
    for hh in range(RET_HEADS):
        lgf = lgf_ref[hh]
        lgb = lgb_ref[hh]
        qh = q_all[:, hh * RET_DK:(hh + 1) * RET_DK]
        qr = qh * cq + pltpu.roll(qh, RET_DK // 2, 1) * sq
        decay = jnp.where(dist >= 0.0, jnp.exp(lgf * jnp.maximum(dist, 0.0)),
                          jnp.exp(lgb * jnp.maximum(-dist, 0.0)))
        cross_f = jnp.exp(lgf * (pos + 1.0))
        cross_b = jnp.exp(lgb * (RET_CHUNK - pos))
        state_b = jnp.exp(lgb * pos)
        chunk_b = jnp.exp(lgb * jnp.full((1, RET_DV), float(RET_CHUNK), F32))
        for c in reversed(range(n_chunks)):
            r0 = c * RET_CHUNK
            qc = qr[r0:r0 + RET_CHUNK]
            kc = k_ref[0, r0:r0 + RET_CHUNK, hh * RET_DK:(hh + 1) * RET_DK]
            vc = v_ref[0, r0:r0 + RET_CHUNK, hh * RET_DV:(hh + 1) * RET_DV]
            sb = sb_scr[hh]
            att = _dot_nt(qc.astype(BF16), kc) * decay
            o = (_dot(att.astype(BF16), vc)
                 + _dot((qc * cross_f).astype(BF16), sf_ref[0, c, hh])
                 + _dot((qc * cross_b).astype(BF16), sb.astype(BF16)))
            kb = (kc.astype(F32) * state_b).astype(BF16)
            sb_scr[hh] = sb * chunk_b + _dot_tn(kb, vc)
            o_parts[c][hh] = _norm(o)
    on = jnp.concatenate([jnp.concatenate(o_parts[c], axis=1) for c in range(n_chunks)], axis=0)

    g_ret = _dot(h, wg_ref[:, :V_DIM])
    yb_in = (_silu(g_ret) * (on * gng_ref[...])).astype(BF16)
    y_b = _dot(yb_in, wro_ref[...])

    y_a = _dot(ya_ref[0], wco_ref[...])

    d = x_ref.shape[-1]
    g_a = _dot(h, wg_ref[:, V_DIM:V_DIM + d])
    g_b = _dot(h, wg_ref[:, V_DIM + d:])
    y = (_sigmoid(g_a) * y_a + _sigmoid(g_b) * y_b).astype(BF16)
    ym = _dot(y, wo_ref[...])

    x1 = _norm(alpha * x_ref[0] + g1_ref[0] * ym) * l1g_ref[...] + l1b_ref[...]
    x1_ref[0] = x1
    h2 = _norm(x1) * (1.0 + sc2_ref[0]) + sh2_ref[0]
    h2_hi, h2_lo = _split(h2)
    h2_ref[0] = h2_hi
    wr_hi, wr_lo = _split(wr_ref[...])
    lt_ref[0] = _dot_nt(wr_hi, h2_hi) + (_dot_nt(wr_hi, h2_lo) + _dot_nt(wr_lo, h2_hi))


def _mixer(alpha, lgf, lgb, x, h, k, v, sf, sb0, ya, wq, wg, wco, wro, wo, gng,
           g1, sh2, sc2, l1g, l1b, wr_t, cq, sq):
    b, l, d = x.shape
    nt = l // MIX_ROWS
    cpt = MIX_ROWS // RET_CHUNK
    rev = lambda w: pl.BlockSpec((1, MIX_ROWS, w), lambda i, t: (i, nt - 1 - t, 0))
    vec = pl.BlockSpec((1, 1, d), lambda i, t: (i, 0, 0))
    full = lambda a: pl.BlockSpec(a.shape, lambda i, t: (0,) * a.ndim, pipeline_mode=pl.Buffered(1))
    rope = pl.BlockSpec((MIX_ROWS, RET_DK), lambda i, t: (nt - 1 - t, 0))
    state = pl.BlockSpec((1, RET_HEADS, RET_DK, RET_DV), lambda i, t: (i, 0, 0, 0))
    return pl.pallas_call(
        functools.partial(_mixer_kernel, alpha),
        grid=(b, nt),
        in_specs=[_smem(), _smem(), rev(d), rev(d), rev(QK_DIM), rev(V_DIM),
                  pl.BlockSpec((1, cpt, RET_HEADS, RET_DK, RET_DV), lambda i, t: (i, nt - 1 - t, 0, 0, 0)),
                  state, rev(CONV_DIM),
                  full(wq), full(wg), full(wco), full(wro), full(wo), full(gng),
                  vec, vec, vec, full(l1g), full(l1b), full(wr_t), rope, rope],
        out_specs=(rev(d), rev(d),
                   pl.BlockSpec((1, N_EXPERTS, MIX_ROWS), lambda i, t: (i, 0, nt - 1 - t))),
        out_shape=(jax.ShapeDtypeStruct((b, l, d), F32),
                   jax.ShapeDtypeStruct((b, l, d), BF16),
                   jax.ShapeDtypeStruct((b, N_EXPERTS, l), F32)),
        scratch_shapes=[pltpu.VMEM((RET_HEADS, RET_DK, RET_DV), F32)],
        compiler_params=_params(("parallel", "arbitrary")),
        name="mixer",
    )(lgf, lgb, x, h, k, v, sf, sb0, ya, wq, wg, wco, wro, wo, gng,
      g1, sh2, sc2, l1g, l1b, wr_t, cq, sq)


def _route_kernel(cap, lt_ref, rank_ref, aff_ref):
    logits = lt_ref[0]
    n_e, l = logits.shape
    m = jnp.max(logits, axis=0, keepdims=True)
    p = jnp.exp(logits - m)
    aff = p / jnp.sum(p, axis=0, keepdims=True)
    aff_ref[0] = aff
    capf = float(cap)

    def count(ones):
        return jnp.sum(ones, axis=1, keepdims=True)

    def value_step(i, lo):
        cand = lo | jnp.left_shift(jnp.int32(1), 30 - i)
        n_ge = count(jnp.where(aff >= pltpu.bitcast(cand, F32), 1.0, 0.0))
        return jnp.where(n_ge >= capf, cand, lo)

    thr = pltpu.bitcast(lax.fori_loop(0, 31, value_step, jnp.zeros((n_e, 1), jnp.int32)), F32)
    gt = jnp.where(aff > thr, 1.0, 0.0)
    eq = jnp.where(aff == thr, 1.0, 0.0)
    need = capf - count(gt)
    idx = lax.broadcasted_iota(jnp.int32, (n_e, l), 1)
    idx_bits = int(l - 1).bit_length()

    def index_step(i, lo):
        cand = lo | jnp.left_shift(jnp.int32(1), idx_bits - 1 - i)
        below = count(jnp.where(idx < cand, eq, 0.0))
        return jnp.where(below < need, cand, lo)

    last = lax.fori_loop(0, idx_bits, index_step, jnp.zeros((n_e, 1), jnp.int32))
    bound = jnp.where(need > 0.0, last + 1, 0)
    sel = gt + jnp.where(idx < bound, eq, 0.0)

    blk = 128
    ti = lax.broadcasted_iota(jnp.int32, (blk, blk), 0)
    tj = lax.broadcasted_iota(jnp.int32, (blk, blk), 1)
    tri = jnp.where(ti < tj, 1.0, 0.0).astype(BF16)
    offset = jnp.zeros((n_e, 1), F32)
    for j in range(l // blk):
        sj = sel[:, j * blk:(j + 1) * blk]
        before = _dot(sj.astype(BF16), tri) + offset
        rank_ref[0, :, j * blk:(j + 1) * blk] = jnp.where(sj > 0.0, before, -1.0).astype(jnp.int32)
        offset = offset + jnp.sum(sj, axis=1, keepdims=True)


def _route(lt, cap):
    b, n_e, l = lt.shape
    spec = pl.BlockSpec((1, n_e, l), lambda i: (i, 0, 0))
    return pl.pallas_call(
        functools.partial(_route_kernel, cap),
        grid=(b,),
        in_specs=[spec],
        out_specs=(spec, spec),
        out_shape=(jax.ShapeDtypeStruct((b, n_e, l), jnp.int32),
                   jax.ShapeDtypeStruct((b, n_e, l), F32)),
        compiler_params=_params(("parallel",)),
        name="route_topc",
    )(lt)


def _moe_kernel(cap, alpha, rank_ref, aff_ref, h2_ref, wg_ref, wu_ref, wd_ref,
                x1_ref, g2_ref, lng_ref, lnb_ref, o_ref):
    e = pl.program_id(1)
    l = o_ref.shape[1]
    n_tiles = l // OUT_ROWS

    @pl.when(e == 0)
    def _():
        o_ref[0] = jnp.zeros(o_ref.shape[1:], F32)

    rank = rank_ref[0, 0]
    group = cap // MOE_GROUPS
    picks, yes = [], []
    for g in range(MOE_GROUPS):
        slot = lax.broadcasted_iota(jnp.int32, (group, l), 0) + g * group
        hit = slot == rank
        pick_g = jnp.where(hit, 1.0, 0.0).astype(BF16)
        gate = jnp.sum(jnp.where(hit, aff_ref[0, 0], 0.0), axis=1, keepdims=True)
        xe = _dot(pick_g, h2_ref[0]).astype(BF16)
        he = (_silu(_dot(xe, wg_ref[0])) * _dot(xe, wu_ref[0])).astype(BF16)
        picks.append(pick_g)
        yes.append((_dot(he, wd_ref[0]) * gate).astype(BF16))
    pick = jnp.concatenate(picks, axis=0)
    ye = jnp.concatenate(yes, axis=0)
    for i in range(n_tiles):
        r0 = i * OUT_ROWS
        o_ref[0, r0:r0 + OUT_ROWS, :] += _dot_tn(pick[:, r0:r0 + OUT_ROWS], ye)

    @pl.when(e == pl.num_programs(1) - 1)
    def _():
        def tile(i, carry):
            r0 = pl.multiple_of(i * OUT_ROWS, OUT_ROWS)
            rows = pl.ds(r0, OUT_ROWS)
            z = alpha * x1_ref[0, rows, :] + g2_ref[0] * o_ref[0, rows, :]
            o_ref[0, rows, :] = _norm(z) * lng_ref[...] + lnb_ref[...]
            return carry

        lax.fori_loop(0, n_tiles, tile, 0)


def _moe(alpha, rank, aff, h2, wgate, wup, wdown, x1, g2, ln_g, ln_b, cap):
    b, l, d = h2.shape
    n_e = wgate.shape[0]
    f = wgate.shape[2]
    row = pl.BlockSpec((1, 1, 1, l), lambda i, e: (i, e, 0, 0))
    vec = pl.BlockSpec((1, d), lambda i, e: (0, 0))
    once = pl.Buffered(1)
    return pl.pallas_call(
        functools.partial(_moe_kernel, cap, alpha),
        grid=(b, n_e),
        in_specs=[row, row,
                  pl.BlockSpec((1, l, d), lambda i, e: (i, 0, 0), pipeline_mode=once),
                  pl.BlockSpec((1, d, f), lambda i, e: (e, 0, 0)),
                  pl.BlockSpec((1, d, f), lambda i, e: (e, 0, 0)),
                  pl.BlockSpec((1, f, d), lambda i, e: (e, 0, 0)),
                  pl.BlockSpec((1, l, d), lambda i, e: (i, 0, 0), pipeline_mode=once),
                  pl.BlockSpec((1, 1, d), lambda i, e: (i, 0, 0)), vec, vec],
        out_specs=pl.BlockSpec((1, l, d), lambda i, e: (i, 0, 0)),
        out_shape=jax.ShapeDtypeStruct((b, l, d), F32),
        compiler_params=_params(("parallel", "arbitrary")),
        name="moe_experts",
    )(rank.reshape(b, n_e, 1, l), aff.reshape(b, n_e, 1, l), h2, wgate, wup, wdown, x1, g2, ln_g, ln_b)


def _rope_tables(l):
    n_axis = RET_DK // 4
    freqs = ROPE_BASE ** (-np.arange(n_axis, dtype=np.float64) / n_axis)
    pos = np.arange(l)
    ang = np.concatenate([(pos // GRID_W)[:, None] * freqs, (pos % GRID_W)[:, None] * freqs], axis=-1)
    cos, sin = np.cos(ang), np.sin(ang)
    return (np.concatenate([cos, cos], axis=-1).astype(np.float32),
            np.concatenate([-sin, sin], axis=-1).astype(np.float32))


def kernel(x, c, ctx, c_ctx, w_ada, b_ada, w_in, conv_w, conv_b, conv_ln_g, conv_ln_b, w_conv_out,
           log_decay_f, log_decay_b, ret_gn_g, w_ret_out, w_out, ln1_g, ln1_b,
           w_router, w_gate, w_up, w_down, ln2_g, ln2_b):
    depth = w_ada.shape[0]
    assert depth == 1, "single trunk layer"
    b, l, d = x.shape
    alpha = (2.0 * depth) ** 0.25
    cap = EC_FACTOR * l // N_EXPERTS
    u_end = 2 * CONV_DIM
    q_end = u_end + QK_DIM
    k_end = q_end + QK_DIM
    v_end = k_end + V_DIM
    row = lambda a: a.reshape(1, -1)

    n_mod = b + 1
    pad = (-n_mod) % 8
    cc = jnp.concatenate([c, c_ctx[None], jnp.zeros((pad, d), F32)], axis=0)
    mod = _ada(cc, w_ada[0], row(b_ada[0]))
    sh1, sc1, g1, sh2, sc2, g2 = [m.reshape(b, 1, d) for m in jnp.split(mod[:b], 6, axis=-1)]
    csh1, csc1 = mod[b:b + 1, :d], mod[b:b + 1, d:2 * d]

    perm = np.concatenate([np.arange(0, RET_DK, 2), np.arange(1, RET_DK, 2)])
    perm = (np.arange(RET_HEADS)[:, None] * RET_DK + perm[None, :]).reshape(-1)
    w = w_in[0]
    w_kvu = jnp.concatenate([w[:, q_end:k_end][:, perm], w[:, k_end:v_end], w[:, :u_end]], axis=1).astype(BF16)
    wq = w[:, u_end:q_end][:, perm].astype(BF16)
    wg = w[:, v_end:].astype(BF16)

    cos_t, sin_t = _rope_tables(l)
    cq, sq = jnp.asarray(cos_t), jnp.asarray(sin_t)
    k_scale = RET_DK ** -0.5
    ck, sk = jnp.asarray(cos_t * k_scale), jnp.asarray(sin_t * k_scale)

    lgf, lgb = log_decay_f[0], log_decay_b[0]
    sf0, sb0 = _ctx_states(lgf, lgb, ctx, csh1, csc1, w_kvu)
    cw8 = jnp.repeat(conv_w[0], 8, axis=0)
    h, k, v, ya, sf = _proj(lgf, x, sh1, sc1, w_kvu, ck, sk, sf0,
                            cw8, row(conv_b[0]), row(conv_ln_g[0]), row(conv_ln_b[0]))
    x1, h2, lt = _mixer(alpha, lgf, lgb, x, h, k, v, sf, sb0, ya, wq, wg,
                        w_conv_out[0].astype(BF16), w_ret_out[0].astype(BF16), w_out[0].astype(BF16),
                        row(ret_gn_g[0]), g1, sh2, sc2, row(ln1_g[0]), row(ln1_b[0]), w_router[0].T, cq, sq)
    rank, aff = _route(lt, cap)
    return _moe(alpha, rank, aff, h2, w_gate[0].astype(BF16), w_up[0].astype(BF16), w_down[0].astype(BF16),
                x1, g2, row(ln2_g[0]), row(ln2_b[0]), cap)
```

```python
import functools

import numpy as np
import jax
import jax.numpy as jnp
from jax import lax
from jax.experimental import pallas as pl
from jax.experimental.pallas import tpu as pltpu
from jax.experimental.pallas import tpu_sc as plsc

F32 = jnp.float32
BF16 = jnp.bfloat16
U32 = jnp.uint32

GRID_W = 64
CONV_DIM = 512
CONV_WIDTH = 31
CONV_HALO = CONV_WIDTH // 2
RET_HEADS = 4
RET_DK = 128
RET_DV = 256
RET_CHUNK = 128
ROPE_BASE = 10000.0
QK_DIM = RET_HEADS * RET_DK
V_DIM = RET_HEADS * RET_DV
N_EXPERTS = 16
EC_FACTOR = 2
LN_EPS = 1e-5

ADA_COLS = 768
PROJ_ROWS = 512
MIX_ROWS = 512
CONV_PAD = 16
CONV_BLOCK = 32
OUT_ROWS = 512
PACK_PARTS = 2
SC_WINDOW = 128
V7X_VMEM_LIMIT = 56 * 1024 * 1024


def _dot(a, b):
    return jnp.dot(a, b, preferred_element_type=F32)


def _dot_nt(a, b):
    return lax.dot_general(a, b, (((1,), (1,)), ((), ())), preferred_element_type=F32)


def _dot_tn(a, b):
    return lax.dot_general(a, b, (((0,), (0,)), ((), ())), preferred_element_type=F32)


def _split(a):
    hi = a.astype(BF16)
    lo = (a - hi.astype(F32)).astype(BF16)
    return hi, lo


def _norm(x):
    mu = jnp.mean(x, axis=-1, keepdims=True)
    xc = x - mu
    var = jnp.mean(xc * xc, axis=-1, keepdims=True)
    return xc * lax.rsqrt(var + LN_EPS)


def _sigmoid(x):
    return 1.0 / (1.0 + jnp.exp(-x))


def _silu(x):
    return x * _sigmoid(x)


def _col_iota(n):
    return lax.broadcasted_iota(jnp.int32, (n, 1), 0).astype(F32)


def _pack_pair(lo, hi):
    lo_bits = pltpu.bitcast(lo.astype(F32), U32)
    hi_bits = pltpu.bitcast(hi.astype(F32), U32)
    return lax.shift_right_logical(lo_bits, jnp.uint32(16)) | hi_bits


def _unpack_pair(word):
    lo = pltpu.bitcast(lax.shift_left(word, jnp.uint32(16)), F32)
    hi = pltpu.bitcast(word & jnp.uint32(0xFFFF0000), F32)
    return lo.astype(BF16), hi.astype(BF16)


def _smem():
    return pl.BlockSpec(memory_space=pltpu.SMEM)


def _params(sem, vmem=V7X_VMEM_LIMIT):
    return pltpu.CompilerParams(dimension_semantics=sem, vmem_limit_bytes=vmem)


def _ada_kernel(c_ref, w_ref, b_ref, o_ref):
    a_hi, a_lo = _split(_silu(c_ref[...]))
    w_hi, w_lo = _split(w_ref[...])
    o_ref[...] = _dot(a_hi, w_hi) + (_dot(a_hi, w_lo) + _dot(a_lo, w_hi)) + b_ref[...]


def _ada(cc, w, b):
    m, d = cc.shape
    n = w.shape[1]
    return pl.pallas_call(
        _ada_kernel,
        grid=(n // ADA_COLS,),
        in_specs=[pl.BlockSpec((m, d), lambda j: (0, 0)),
                  pl.BlockSpec((d, ADA_COLS), lambda j: (0, j)),
                  pl.BlockSpec((1, ADA_COLS), lambda j: (0, j))],
        out_specs=pl.BlockSpec((m, ADA_COLS), lambda j: (0, j)),
        out_shape=jax.ShapeDtypeStruct((m, n), F32),
        compiler_params=_params(("parallel",)),
        name="ada_proj",
    )(cc, w, b)


def _ctx_kernel(lgf_ref, lgb_ref, ctx_ref, sh_ref, sc_ref, w_ref, sf_ref, sb_ref):
    x = ctx_ref[0]
    lc = x.shape[0]
    h = (_norm(x) * (1.0 + sc_ref[...]) + sh_ref[...]).astype(BF16)
    kv = _dot(h, w_ref[...])
    t = _col_iota(lc)
    for hh in range(RET_HEADS):
        k = kv[:, hh * RET_DK:(hh + 1) * RET_DK] * (RET_DK ** -0.5)
        v = kv[:, QK_DIM + hh * RET_DV:QK_DIM + (hh + 1) * RET_DV].astype(BF16)
        wf = jnp.exp(lgf_ref[hh] * (lc - 1.0 - t))
        wb = jnp.exp(lgb_ref[hh] * t)
        sf_ref[0, hh] = _dot_tn((k * wf).astype(BF16), v)
        sb_ref[0, hh] = _dot_tn((k * wb).astype(BF16), v)


def _ctx_states(lgf, lgb, ctx, csh, csc, w_kvu):
    b, lc, d = ctx.shape
    kvw = QK_DIM + V_DIM
    st = jax.ShapeDtypeStruct((b, RET_HEADS, RET_DK, RET_DV), F32)
    st_spec = pl.BlockSpec((1, RET_HEADS, RET_DK, RET_DV), lambda i: (i, 0, 0, 0))
    return pl.pallas_call(
        _ctx_kernel,
        grid=(b,),
        in_specs=[_smem(), _smem(),
                  pl.BlockSpec((1, lc, d), lambda i: (i, 0, 0)),
                  pl.BlockSpec((1, d), lambda i: (0, 0)),
                  pl.BlockSpec((1, d), lambda i: (0, 0)),
                  pl.BlockSpec((d, kvw), lambda i: (0, 0))],
        out_specs=(st_spec, st_spec),
        out_shape=(st, st),
        compiler_params=_params(("parallel",)),
        name="ctx_states",
    )(lgf, lgb, ctx, csh, csc, w_kvu)


def _proj_kernel(lgf_ref, xp_ref, x_ref, xn_ref, sh_ref, sc_ref, w_ref, ck_ref, sk_ref, sf0_ref,
                 cw_ref, cb_ref, clg_ref, clb_ref,
                 h_ref, k_ref, v_ref, ya_ref, sf_ref, s_scr, y_scr, yres_scr, z_scr):
    t = pl.program_id(1)
    nt = pl.num_programs(1)

    @pl.when(t == 0)
    def _():
        s_scr[...] = sf0_ref[0]

    rows = x_ref.shape[1]
    ext = rows + 2 * CONV_PAD
    x_ext = jnp.concatenate([xp_ref[0], x_ref[0], xn_ref[0]], axis=0)
    h_ext = (_norm(x_ext) * (1.0 + sc_ref[0]) + sh_ref[0]).astype(BF16)
    h = h_ext[CONV_PAD:CONV_PAD + rows]
    h_ref[0] = h
    kvw = QK_DIM + V_DIM
    u = _dot(h_ext, w_ref[:, kvw:])
    row = lax.broadcasted_iota(jnp.int32, (ext, 1), 0)
    head_ok = jnp.where(t == 0, 0.0, 1.0)
    tail_ok = jnp.where(t == nt - 1, 0.0, 1.0)
    inside = jnp.where(row < CONV_PAD, head_ok, jnp.where(row >= CONV_PAD + rows, tail_ok, 1.0))
    y_scr[...] = u[:, :CONV_DIM] * _sigmoid(u[:, CONV_DIM:]) * inside

    kk = _dot(h, w_ref[:, :QK_DIM])
    vv = _dot(h, w_ref[:, QK_DIM:kvw]).astype(BF16)
    v_ref[0] = vv
    ck = ck_ref[...]
    sk = sk_ref[...]
    pos = _col_iota(RET_CHUNK)
    for hh in range(RET_HEADS):
        lg = lgf_ref[hh]
        kh = kk[:, hh * RET_DK:(hh + 1) * RET_DK]
        kr = kh * ck + pltpu.roll(kh, RET_DK // 2, 1) * sk
        k_ref[0, :, hh * RET_DK:(hh + 1) * RET_DK] = kr.astype(BF16)
        state_dec = jnp.exp(lg * (RET_CHUNK - 1.0 - pos))
        chunk_dec = jnp.exp(lg * jnp.full((1, RET_DV), float(RET_CHUNK), F32))
        for c in range(rows // RET_CHUNK):
            r0 = c * RET_CHUNK
            s = s_scr[hh]
            sf_ref[0, c, hh] = s.astype(BF16)
            kc = (kr[r0:r0 + RET_CHUNK] * state_dec).astype(BF16)
            vc = vv[r0:r0 + RET_CHUNK, hh * RET_DV:(hh + 1) * RET_DV]
            s_scr[hh] = s * chunk_dec + _dot_tn(kc, vc)

    span = ext - 8
    for r in range(8):
        yres_scr[r] = y_scr[pl.ds(r, span), :]

    for i in range(rows // CONV_BLOCK):
        base = i * CONV_BLOCK
        acc = None
        for w in range(CONV_WIDTH):
            a, r = divmod(w + 1, 8)
            wk = jnp.concatenate([cw_ref[8 * w:8 * w + 8, :]] * (CONV_BLOCK // 8), axis=0)
            tap = yres_scr[r, base + 8 * a:base + 8 * a + CONV_BLOCK, :] * wk
            acc = tap if acc is None else acc + tap
        z_scr[base:base + CONV_BLOCK, :] = acc
    z = _norm(z_scr[...] + cb_ref[...]) * clg_ref[...] + clb_ref[...]
    ya_ref[0] = _silu(z).astype(BF16)


def _proj(lgf, x, sh1, sc1, w_kvu, ck, sk, sf0, cw, cb, clg, clb):
    b, l, d = x.shape
    nt = l // PROJ_ROWS
    cpt = PROJ_ROWS // RET_CHUNK
    hpt = PROJ_ROWS // CONV_PAD
    n_halo = l // CONV_PAD
    ext = PROJ_ROWS + 2 * CONV_PAD
    tile = lambda w: pl.BlockSpec((1, PROJ_ROWS, w), lambda i, t: (i, t, 0))
    vec = pl.BlockSpec((1, 1, d), lambda i, t: (i, 0, 0))
    full = lambda a: pl.BlockSpec(a.shape, lambda i, t: (0,) * a.ndim, pipeline_mode=pl.Buffered(1))
    rope = pl.BlockSpec((PROJ_ROWS, RET_DK), lambda i, t: (t, 0))
    prev = pl.BlockSpec((1, CONV_PAD, d), lambda i, t: (i, jnp.maximum(t * hpt - 1, 0), 0))
    nxt = pl.BlockSpec((1, CONV_PAD, d), lambda i, t: (i, jnp.minimum((t + 1) * hpt, n_halo - 1), 0))
    return pl.pallas_call(
        _proj_kernel,
        grid=(b, nt),
        in_specs=[_smem(), prev, tile(d), nxt, vec, vec, full(w_kvu), rope, rope,
                  pl.BlockSpec((1, RET_HEADS, RET_DK, RET_DV), lambda i, t: (i, 0, 0, 0)),
                  full(cw), full(cb), full(clg), full(clb)],
        out_specs=(tile(d), tile(QK_DIM), tile(V_DIM), tile(CONV_DIM),
                   pl.BlockSpec((1, cpt, RET_HEADS, RET_DK, RET_DV), lambda i, t: (i, t, 0, 0, 0))),
        out_shape=(jax.ShapeDtypeStruct((b, l, d), BF16),
                   jax.ShapeDtypeStruct((b, l, QK_DIM), BF16),
                   jax.ShapeDtypeStruct((b, l, V_DIM), BF16),
                   jax.ShapeDtypeStruct((b, l, CONV_DIM), BF16),
                   jax.ShapeDtypeStruct((b, l // RET_CHUNK, RET_HEADS, RET_DK, RET_DV), BF16)),
        scratch_shapes=[pltpu.VMEM((RET_HEADS, RET_DK, RET_DV), F32),
                        pltpu.VMEM((ext, CONV_DIM), F32),
                        pltpu.VMEM((8, ext - 8, CONV_DIM), F32),
                        pltpu.VMEM((PROJ_ROWS, CONV_DIM), F32)],
        compiler_params=_params(("parallel", "arbitrary")),
        name="proj_kv_conv",
    )(lgf, x, x, x, sh1, sc1, w_kvu, ck, sk, sf0, cw, cb, clg, clb)


def _mixer_kernel(alpha, lgf_ref, lgb_ref, x_ref, h_ref, k_ref, v_ref, sf_ref, sb0_ref, ya_ref,
                  wq_ref, wg_ref, wco_ref, wro_ref, wo_ref, gng_ref,
                  g1_ref, sh2_ref, sc2_ref, l1g_ref, l1b_ref, wr_ref, cq_ref, sq_ref,
                  x1_ref, hp_ref, lt_ref, sb_scr):
    t = pl.program_id(1)

    @pl.when(t == 0)
    def _():
        sb_scr[...] = sb0_ref[0]

    h = h_ref[0]
    rows = h.shape[0]
    cq = cq_ref[...]
    sq = sq_ref[...]
    q_all = _dot(h, wq_ref[...])
    pos = _col_iota(RET_CHUNK)
    di = lax.broadcasted_iota(jnp.int32, (RET_CHUNK, RET_CHUNK), 0)
    dj = lax.broadcasted_iota(jnp.int32, (RET_CHUNK, RET_CHUNK), 1)
    dist = (di - dj).astype(F32)

    n_chunks = rows // RET_CHUNK
    o_parts = [[None] * RET_HEADS for _ in range(n_chunks)]
    for hh in range(RET_HEADS):
        lgf = lgf_ref[hh]
        lgb = lgb_ref[hh]
        qh = q_all[:, hh * RET_DK:(hh + 1) * RET_DK]
        qr = qh * cq + pltpu.roll(qh, RET_DK // 2, 1) * sq
        decay = jnp.where(dist >= 0.0, jnp.exp(lgf * jnp.maximum(dist, 0.0)),
                          jnp.exp(lgb * jnp.maximum(-dist, 0.0)))
        cross_f = jnp.exp(lgf * (pos + 1.0))
        cross_b = jnp.exp(lgb * (RET_CHUNK - pos))
        state_b = jnp.exp(lgb * pos)
        chunk_b = jnp.exp(lgb * jnp.full((1, RET_DV), float(RET_CHUNK), F32))
        for c in reversed(range(n_chunks)):
            r0 = c * RET_CHUNK
            qc = qr[r0:r0 + RET_CHUNK]
            kc = k_ref[0, r0:r0 + RET_CHUNK, hh * RET_DK:(hh + 1) * RET_DK]
            vc = v_ref[0, r0:r0 + RET_CHUNK, hh * RET_DV:(hh + 1) * RET_DV]
            sb = sb_scr[hh]
            att = _dot_nt(qc.astype(BF16), kc) * decay
            o = (_dot(att.astype(BF16), vc)
                 + _dot((qc * cross_f).astype(BF16), sf_ref[0, c, hh])
                 + _dot((qc * cross_b).astype(BF16), sb.astype(BF16)))
            kb = (kc.astype(F32) * state_b).astype(BF16)
            sb_scr[hh] = sb * chunk_b + _dot_tn(kb, vc)
            o_parts[c][hh] = _norm(o)
    on = jnp.concatenate([jnp.concatenate(o_parts[c], axis=1) for c in range(n_chunks)], axis=0)

    g_ret = _dot(h, wg_ref[:, :V_DIM])
    yb_in = (_silu(g_ret) * (on * gng_ref[...])).astype(BF16)
    y_b = _dot(yb_in, wro_ref[...])

    y_a = _dot(ya_ref[0], wco_ref[...])

    d = x_ref.shape[-1]
    g_a = _dot(h, wg_ref[:, V_DIM:V_DIM + d])
    g_b = _dot(h, wg_ref[:, V_DIM + d:])
    y = (_sigmoid(g_a) * y_a + _sigmoid(g_b) * y_b).astype(BF16)
    ym = _dot(y, wo_ref[...])

    x1 = _norm(alpha * x_ref[0] + g1_ref[0] * ym) * l1g_ref[...] + l1b_ref[...]
    x1_ref[0] = x1
    h2 = _norm(x1) * (1.0 + sc2_ref[0]) + sh2_ref[0]
    h2_hi, h2_lo = _split(h2)
    pw = d // (2 * PACK_PARTS)
    for p in range(PACK_PARTS):
        c0 = 2 * p * pw
        hp_ref[p, 0] = _pack_pair(h2_hi[:, c0:c0 + pw], h2_hi[:, c0 + pw:c0 + 2 * pw])
    wr_hi, wr_lo = _split(wr_ref[...])
    lt_ref[0] = _dot_nt(wr_hi, h2_hi) + (_dot_nt(wr_hi, h2_lo) + _dot_nt(wr_lo, h2_hi))


def _mixer(alpha, lgf, lgb, x, h, k, v, sf, sb0, ya, wq, wg, wco, wro, wo, gng,
           g1, sh2, sc2, l1g, l1b, wr_t, cq, sq):
    b, l, d = x.shape
    nt = l // MIX_ROWS
    cpt = MIX_ROWS // RET_CHUNK
    pw = d // (2 * PACK_PARTS)
    rev = lambda w: pl.BlockSpec((1, MIX_ROWS, w), lambda i, t: (i, nt - 1 - t, 0))
    vec = pl.BlockSpec((1, 1, d), lambda i, t: (i, 0, 0))
    full = lambda a: pl.BlockSpec(a.shape, lambda i, t: (0,) * a.ndim, pipeline_mode=pl.Buffered(1))
    rope = pl.BlockSpec((MIX_ROWS, RET_DK), lambda i, t: (nt - 1 - t, 0))
    state = pl.BlockSpec((1, RET_HEADS, RET_DK, RET_DV), lambda i, t: (i, 0, 0, 0))
    return pl.pallas_call(
        functools.partial(_mixer_kernel, alpha),
        grid=(b, nt),
        in_specs=[_smem(), _smem(), rev(d), rev(d), rev(QK_DIM), rev(V_DIM),
                  pl.BlockSpec((1, cpt, RET_HEADS, RET_DK, RET_DV), lambda i, t: (i, nt - 1 - t, 0, 0, 0)),
                  state, rev(CONV_DIM),
                  full(wq), full(wg), full(wco), full(wro), full(wo), full(gng),
                  vec, vec, vec, full(l1g), full(l1b), full(wr_t), rope, rope],
        out_specs=(rev(d),
                   pl.BlockSpec((PACK_PARTS, 1, MIX_ROWS, pw), lambda i, t: (0, i, nt - 1 - t, 0)),
                   pl.BlockSpec((1, N_EXPERTS, MIX_ROWS), lambda i, t: (i, 0, nt - 1 - t))),
        out_shape=(jax.ShapeDtypeStruct((b, l, d), F32),
                   jax.ShapeDtypeStruct((PACK_PARTS, b, l, pw), U32),
                   jax.ShapeDtypeStruct((b, N_EXPERTS, l), F32)),
        scratch_shapes=[pltpu.VMEM((RET_HEADS, RET_DK, RET_DV), F32)],
        compiler_params=_params(("parallel", "arbitrary")),
        name="mixer",
    )(lgf, lgb, x, h, k, v, sf, sb0, ya, wq, wg, wco, wro, wo, gng,
      g1, sh2, sc2, l1g, l1b, wr_t, cq, sq)


def _route_kernel(cap, lt_ref, rank_ref, aff_ref, idx_ref, cum_scr):
    logits = lt_ref[0]
    n_e, l = logits.shape
    m = jnp.max(logits, axis=0, keepdims=True)
    p = jnp.exp(logits - m)
    aff = p / jnp.sum(p, axis=0, keepdims=True)
    aff_ref[0] = aff
    capf = float(cap)

    def count(ones):
        return jnp.sum(ones, axis=1, keepdims=True)

    def value_step(i, lo):
        cand = lo | jnp.left_shift(jnp.int32(1), 30 - i)
        n_ge = count(jnp.where(aff >= pltpu.bitcast(cand, F32), 1.0, 0.0))
        return jnp.where(n_ge >= capf, cand, lo)

    thr = pltpu.bitcast(lax.fori_loop(0, 31, value_step, jnp.zeros((n_e, 1), jnp.int32)), F32)
    gt = jnp.where(aff > thr, 1.0, 0.0)
    eq = jnp.where(aff == thr, 1.0, 0.0)
    need = capf - count(gt)
    idx = lax.broadcasted_iota(jnp.int32, (n_e, l), 1)
    idx_bits = int(l - 1).bit_length()

    def index_step(i, lo):
        cand = lo | jnp.left_shift(jnp.int32(1), idx_bits - 1 - i)
        below = count(jnp.where(idx < cand, eq, 0.0))
        return jnp.where(below < need, cand, lo)

    last = lax.fori_loop(0, idx_bits, index_step, jnp.zeros((n_e, 1), jnp.int32))
    bound = jnp.where(need > 0.0, last + 1, 0)
    sel = gt + jnp.where(idx < bound, eq, 0.0)

    blk = 128
    ti = lax.broadcasted_iota(jnp.int32, (blk, blk), 0)
    tj = lax.broadcasted_iota(jnp.int32, (blk, blk), 1)
    tri = jnp.where(ti < tj, 1.0, 0.0).astype(BF16)
    offset = jnp.zeros((n_e, 1), F32)
    for j in range(l // blk):
        sj = sel[:, j * blk:(j + 1) * blk]
        before = _dot(sj.astype(BF16), tri) + offset
        rank_ref[0, :, j * blk:(j + 1) * blk] = jnp.where(sj > 0.0, before, -1.0).astype(jnp.int32)
        cum_scr[:, j * blk:(j + 1) * blk] = before + sj
        offset = offset + jnp.sum(sj, axis=1, keepdims=True)

    row0 = (pl.program_id(0) * l).astype(F32)
    slot_col = _col_iota(cap)
    ri = lax.broadcasted_iota(jnp.int32, (cap, cap), 0)
    ci = lax.broadcasted_iota(jnp.int32, (cap, cap), 1)

    def slots(e, carry):
        cum = cum_scr[pl.ds(e, 1), :]
        token = jnp.sum(jnp.where(cum <= slot_col, 1.0, 0.0), axis=1, keepdims=True)
        as_row = jnp.sum(jnp.where(ri == ci, token, 0.0), axis=0, keepdims=True)
        idx_ref[0, pl.ds(e, 1), :] = (as_row + row0).astype(jnp.int32)
        return carry

    lax.fori_loop(0, n_e, slots, 0)


def _route(lt, cap):
    b, n_e, l = lt.shape
    spec = pl.BlockSpec((1, n_e, l), lambda i: (i, 0, 0))
    return pl.pallas_call(
        functools.partial(_route_kernel, cap),
        grid=(b,),
        in_specs=[spec],
        out_specs=(spec, spec, pl.BlockSpec((1, n_e, cap), lambda i: (i, 0, 0))),
        out_shape=(jax.ShapeDtypeStruct((b, n_e, l), jnp.int32),
                   jax.ShapeDtypeStruct((b, n_e, l), F32),
                   jax.ShapeDtypeStruct((b, n_e, cap), jnp.int32)),
        scratch_shapes=[pltpu.VMEM((n_e, l), F32)],
        compiler_params=_params(("parallel",)),
        name="route_topc",
    )(lt)


def _sc_gather(rows, idx):
    n = idx.shape[0]
    w = rows.shape[1]
    mesh = plsc.VectorSubcoreMesh(core_axis_name="c", subcore_axis_name="s")

    @pl.kernel(out_type=jax.ShapeDtypeStruct((n, w), rows.dtype), mesh=mesh, scratch_types=[])
    def gather(rows_hbm, idx_hbm, out_hbm):
        def window(idx_vmem, out_vmem):
            pltpu.sync_copy(rows_hbm.at[idx_vmem.at[0]], out_vmem)

        pltpu.emit_pipeline(
            window,
            grid=(n // SC_WINDOW,),
            in_specs=[pl.BlockSpec((1, SC_WINDOW), index_map=lambda i: (0, i))],
            out_specs=[pl.BlockSpec((SC_WINDOW, w), index_map=lambda i: (i, 0))],
            core_axis_name=("c", "s"),
            dimension_semantics=(pltpu.PARALLEL,),
        )(idx_hbm, out_hbm)

    return gather(rows, idx.reshape(1, n))


def _moe_kernel(cap, alpha, rank_ref, aff_ref, xe_ref, wg_ref, wu_ref, wd_ref,
                x1_ref, g2_ref, lng_ref, lnb_ref, o_ref):
    e = pl.program_id(1)
    l = o_ref.shape[1]
    n_tiles = l // OUT_ROWS

    @pl.when(e == 0)
    def _():
        o_ref[0] = jnp.zeros(o_ref.shape[1:], F32)

    rank = rank_ref[0, 0]
    slot = lax.broadcasted_iota(jnp.int32, (cap, l), 0)
    hit = slot == rank
    pick = jnp.where(hit, 1.0, 0.0).astype(BF16)
    gate = jnp.sum(jnp.where(hit, aff_ref[0, 0], 0.0), axis=1, keepdims=True)
    halves = []
    for p in range(PACK_PARTS):
        halves.extend(_unpack_pair(xe_ref[p, 0, 0]))
    xe = jnp.concatenate(halves, axis=1)
    he = (_silu(_dot(xe, wg_ref[0])) * _dot(xe, wu_ref[0])).astype(BF16)
    ye = (_dot(he, wd_ref[0]) * gate).astype(BF16)
    for i in range(n_tiles):
        r0 = i * OUT_ROWS
        o_ref[0, r0:r0 + OUT_ROWS, :] += _dot_tn(pick[:, r0:r0 + OUT_ROWS], ye)

    @pl.when(e == pl.num_programs(1) - 1)
    def _():
        def tile(i, carry):
            r0 = pl.multiple_of(i * OUT_ROWS, OUT_ROWS)
            rows = pl.ds(r0, OUT_ROWS)
            z = alpha * x1_ref[0, rows, :] + g2_ref[0] * o_ref[0, rows, :]
            o_ref[0, rows, :] = _norm(z) * lng_ref[...] + lnb_ref[...]
            return carry

        lax.fori_loop(0, n_tiles, tile, 0)


def _moe(alpha, rank, aff, xe, wgate, wup, wdown, x1, g2, ln_g, ln_b, cap):
    b, l, d = x1.shape
    n_e = wgate.shape[0]
    f = wgate.shape[2]
    row = pl.BlockSpec((1, 1, 1, l), lambda i, e: (i, e, 0, 0))
    vec = pl.BlockSpec((1, d), lambda i, e: (0, 0))
    return pl.pallas_call(
        functools.partial(_moe_kernel, cap, alpha),
        grid=(b, n_e),
        in_specs=[row, row,
                  pl.BlockSpec((PACK_PARTS, 1, 1, cap, xe.shape[-1]), lambda i, e: (0, i, e, 0, 0)),
                  pl.BlockSpec((1, d, f), lambda i, e: (e, 0, 0)),
                  pl.BlockSpec((1, d, f), lambda i, e: (e, 0, 0)),
                  pl.BlockSpec((1, f, d), lambda i, e: (e, 0, 0)),
                  pl.BlockSpec((1, l, d), lambda i, e: (i, 0, 0), pipeline_mode=pl.Buffered(1)),
                  pl.BlockSpec((1, 1, d), lambda i, e: (i, 0, 0)), vec, vec],
        out_specs=pl.BlockSpec((1, l, d), lambda i, e: (i, 0, 0)),
        out_shape=jax.ShapeDtypeStruct((b, l, d), F32),
        compiler_params=_params(("parallel", "arbitrary")),
        name="moe_experts",
    )(rank.reshape(b, n_e, 1, l), aff.reshape(b, n_e, 1, l), xe, wgate, wup, wdown, x1, g2, ln_g, ln_b)


def _rope_tables(l):
    n_axis = RET_DK // 4
    freqs = ROPE_BASE ** (-np.arange(n_axis, dtype=np.float64) / n_axis)
    pos = np.arange(l)
    ang = np.concatenate([(pos // GRID_W)[:, None] * freqs, (pos % GRID_W)[:, None] * freqs], axis=-1)
    cos, sin = np.cos(ang), np.sin(ang)
    return (np.concatenate([cos, cos], axis=-1).astype(np.float32),
            np.concatenate([-sin, sin], axis=-1).astype(np.float32))


def kernel(x, c, ctx, c_ctx, w_ada, b_ada, w_in, conv_w, conv_b, conv_ln_g, conv_ln_b, w_conv_out,
           log_decay_f, log_decay_b, ret_gn_g, w_ret_out, w_out, ln1_g, ln1_b,
           w_router, w_gate, w_up, w_down, ln2_g, ln2_b):
    depth = w_ada.shape[0]
    assert depth == 1, "single trunk layer"
    b, l, d = x.shape
    alpha = (2.0 * depth) ** 0.25
    cap = EC_FACTOR * l // N_EXPERTS
    u_end = 2 * CONV_DIM
    q_end = u_end + QK_DIM
    k_end = q_end + QK_DIM
    v_end = k_end + V_DIM
    row = lambda a: a.reshape(1, -1)

    n_mod = b + 1
    pad = (-n_mod) % 8
    cc = jnp.concatenate([c, c_ctx[None], jnp.zeros((pad, d), F32)], axis=0)
    mod = _ada(cc, w_ada[0], row(b_ada[0]))
    sh1, sc1, g1, sh2, sc2, g2 = [m.reshape(b, 1, d) for m in jnp.split(mod[:b], 6, axis=-1)]
    csh1, csc1 = mod[b:b + 1, :d], mod[b:b + 1, d:2 * d]

    perm = np.concatenate([np.arange(0, RET_DK, 2), np.arange(1, RET_DK, 2)])
    perm = (np.arange(RET_HEADS)[:, None] * RET_DK + perm[None, :]).reshape(-1)
    w = w_in[0]
    w_kvu = jnp.concatenate([w[:, q_end:k_end][:, perm], w[:, k_end:v_end], w[:, :u_end]], axis=1).astype(BF16)
    wq = w[:, u_end:q_end][:, perm].astype(BF16)
    wg = w[:, v_end:].astype(BF16)

    cos_t, sin_t = _rope_tables(l)
    cq, sq = jnp.asarray(cos_t), jnp.asarray(sin_t)
    k_scale = RET_DK ** -0.5
    ck, sk = jnp.asarray(cos_t * k_scale), jnp.asarray(sin_t * k_scale)

    lgf, lgb = log_decay_f[0], log_decay_b[0]
    sf0, sb0 = _ctx_states(lgf, lgb, ctx, csh1, csc1, w_kvu)
    cw8 = jnp.repeat(conv_w[0], 8, axis=0)
    h, k, v, ya, sf = _proj(lgf, x, sh1, sc1, w_kvu, ck, sk, sf0,
                            cw8, row(conv_b[0]), row(conv_ln_g[0]), row(conv_ln_b[0]))
    x1, hp, lt = _mixer(alpha, lgf, lgb, x, h, k, v, sf, sb0, ya, wq, wg,
                        w_conv_out[0].astype(BF16), w_ret_out[0].astype(BF16), w_out[0].astype(BF16),
                        row(ret_gn_g[0]), g1, sh2, sc2, row(ln1_g[0]), row(ln1_b[0]), w_router[0].T, cq, sq)
    rank, aff, slots = _route(lt, cap)
    part_rows = b * l
    ids = jnp.concatenate([slots.reshape(-1) + p * part_rows for p in range(PACK_PARTS)])
    xe = _sc_gather(hp.reshape(PACK_PARTS * part_rows, hp.shape[-1]), ids)
    xe = xe.reshape(PACK_PARTS, b, N_EXPERTS, cap, hp.shape[-1])
    return _moe(alpha, rank, aff, xe, w_gate[0].astype(BF16), w_up[0].astype(BF16), w_down[0].astype(BF16),
                x1, g2, row(ln2_g[0]), row(ln2_b[0]), cap)
```

```python
import functools

import numpy as np
import jax
import jax.numpy as jnp
from jax import lax
from jax.experimental import pallas as pl
from jax.experimental.pallas import tpu as pltpu
from jax.experimental.pallas import tpu_sc as plsc

F32 = jnp.float32
BF16 = jnp.bfloat16
U32 = jnp.uint32

GRID_W = 64
CONV_DIM = 512
CONV_WIDTH = 31
CONV_HALO = CONV_WIDTH // 2
RET_HEADS = 4
RET_DK = 128
RET_DV = 256
RET_CHUNK = 128
ROPE_BASE = 10000.0
QK_DIM = RET_HEADS * RET_DK
V_DIM = RET_HEADS * RET_DV
N_EXPERTS = 16
EC_FACTOR = 2
LN_EPS = 1e-5

ADA_COLS = 768
PROJ_ROWS = 512
MIX_ROWS = 512
CONV_PAD = 16
CONV_BLOCK = 32
OUT_ROWS = 512
PACK_PARTS = 2
SC_WINDOW = 128
SLOT_SHIFT = 4
SLOT_RADIX = 1 << SLOT_SHIFT
TOKEN_SHIFT = 6
V7X_VMEM_LIMIT = 56 * 1024 * 1024


def _dot(a, b):
    return jnp.dot(a, b, preferred_element_type=F32)


def _dot_nt(a, b):
    return lax.dot_general(a, b, (((1,), (1,)), ((), ())), preferred_element_type=F32)


def _dot_tn(a, b):
    return lax.dot_general(a, b, (((0,), (0,)), ((), ())), preferred_element_type=F32)


def _split(a):
    hi = a.astype(BF16)
    lo = (a - hi.astype(F32)).astype(BF16)
    return hi, lo


def _norm(x):
    mu = jnp.mean(x, axis=-1, keepdims=True)
    xc = x - mu
    var = jnp.mean(xc * xc, axis=-1, keepdims=True)
    return xc * lax.rsqrt(var + LN_EPS)


def _sigmoid(x):
    return 1.0 / (1.0 + jnp.exp(-x))


def _silu(x):
    return x * _sigmoid(x)


def _col_iota(n):
    return lax.broadcasted_iota(jnp.int32, (n, 1), 0).astype(F32)


def _pack_pair(lo, hi):
    lo_bits = pltpu.bitcast(lo.astype(F32), U32)
    hi_bits = pltpu.bitcast(hi.astype(F32), U32)
    return lax.shift_right_logical(lo_bits, jnp.uint32(16)) | hi_bits


def _unpack_pair(word):
    lo = pltpu.bitcast(lax.shift_left(word, jnp.uint32(16)), F32)
    hi = pltpu.bitcast(word & jnp.uint32(0xFFFF0000), F32)
    return lo.astype(BF16), hi.astype(BF16)


def _smem():
    return pl.BlockSpec(memory_space=pltpu.SMEM)


def _params(sem, vmem=V7X_VMEM_LIMIT):
    return pltpu.CompilerParams(dimension_semantics=sem, vmem_limit_bytes=vmem)


def _ada_kernel(c_ref, w_ref, b_ref, o_ref):
    a_hi, a_lo = _split(_silu(c_ref[...]))
    w_hi, w_lo = _split(w_ref[...])
    o_ref[...] = _dot(a_hi, w_hi) + (_dot(a_hi, w_lo) + _dot(a_lo, w_hi)) + b_ref[...]


def _ada(cc, w, b):
    m, d = cc.shape
    n = w.shape[1]
    return pl.pallas_call(
        _ada_kernel,
        grid=(n // ADA_COLS,),
        in_specs=[pl.BlockSpec((m, d), lambda j: (0, 0)),
                  pl.BlockSpec((d, ADA_COLS), lambda j: (0, j)),
                  pl.BlockSpec((1, ADA_COLS), lambda j: (0, j))],
        out_specs=pl.BlockSpec((m, ADA_COLS), lambda j: (0, j)),
        out_shape=jax.ShapeDtypeStruct((m, n), F32),
        compiler_params=_params(("parallel",)),
        name="ada_proj",
    )(cc, w, b)


def _ctx_kernel(lgf_ref, lgb_ref, ctx_ref, sh_ref, sc_ref, w_ref, sf_ref, sb_ref):
    x = ctx_ref[0]
    lc = x.shape[0]
    h = (_norm(x) * (1.0 + sc_ref[...]) + sh_ref[...]).astype(BF16)
    kv = _dot(h, w_ref[...])
    t = _col_iota(lc)
    for hh in range(RET_HEADS):
        k = kv[:, hh * RET_DK:(hh + 1) * RET_DK] * (RET_DK ** -0.5)
        v = kv[:, QK_DIM + hh * RET_DV:QK_DIM + (hh + 1) * RET_DV].astype(BF16)
        wf = jnp.exp(lgf_ref[hh] * (lc - 1.0 - t))
        wb = jnp.exp(lgb_ref[hh] * t)
        sf_ref[0, hh] = _dot_tn((k * wf).astype(BF16), v)
        sb_ref[0, hh] = _dot_tn((k * wb).astype(BF16), v)


def _ctx_states(lgf, lgb, ctx, csh, csc, w_kvu):
    b, lc, d = ctx.shape
    kvw = QK_DIM + V_DIM
    st = jax.ShapeDtypeStruct((b, RET_HEADS, RET_DK, RET_DV), F32)
    st_spec = pl.BlockSpec((1, RET_HEADS, RET_DK, RET_DV), lambda i: (i, 0, 0, 0))
    return pl.pallas_call(
        _ctx_kernel,
        grid=(b,),
        in_specs=[_smem(), _smem(),
                  pl.BlockSpec((1, lc, d), lambda i: (i, 0, 0)),
                  pl.BlockSpec((1, d), lambda i: (0, 0)),
                  pl.BlockSpec((1, d), lambda i: (0, 0)),
                  pl.BlockSpec((d, kvw), lambda i: (0, 0))],
        out_specs=(st_spec, st_spec),
        out_shape=(st, st),
        compiler_params=_params(("parallel",)),
        name="ctx_states",
    )(lgf, lgb, ctx, csh, csc, w_kvu)


def _proj_kernel(lgf_ref, xp_ref, x_ref, xn_ref, sh_ref, sc_ref, w_ref, ck_ref, sk_ref, sf0_ref,
                 cw_ref, cb_ref, clg_ref, clb_ref,
                 h_ref, k_ref, v_ref, ya_ref, sf_ref, s_scr, y_scr, yres_scr, z_scr):
    t = pl.program_id(1)
    nt = pl.num_programs(1)

    @pl.when(t == 0)
    def _():
        s_scr[...] = sf0_ref[0]

    rows = x_ref.shape[1]
    ext = rows + 2 * CONV_PAD
    x_ext = jnp.concatenate([xp_ref[0], x_ref[0], xn_ref[0]], axis=0)
    h_ext = (_norm(x_ext) * (1.0 + sc_ref[0]) + sh_ref[0]).astype(BF16)
    h = h_ext[CONV_PAD:CONV_PAD + rows]
    h_ref[0] = h
    kvw = QK_DIM + V_DIM
    u = _dot(h_ext, w_ref[:, kvw:])
    row = lax.broadcasted_iota(jnp.int32, (ext, 1), 0)
    head_ok = jnp.where(t == 0, 0.0, 1.0)
    tail_ok = jnp.where(t == nt - 1, 0.0, 1.0)
    inside = jnp.where(row < CONV_PAD, head_ok, jnp.where(row >= CONV_PAD + rows, tail_ok, 1.0))
    y_scr[...] = u[:, :CONV_DIM] * _sigmoid(u[:, CONV_DIM:]) * inside

    kk = _dot(h, w_ref[:, :QK_DIM])
    vv = _dot(h, w_ref[:, QK_DIM:kvw]).astype(BF16)
    v_ref[0] = vv
    ck = ck_ref[...]
    sk = sk_ref[...]
    pos = _col_iota(RET_CHUNK)
    for hh in range(RET_HEADS):
        lg = lgf_ref[hh]
        kh = kk[:, hh * RET_DK:(hh + 1) * RET_DK]
        kr = kh * ck + pltpu.roll(kh, RET_DK // 2, 1) * sk
        k_ref[0, :, hh * RET_DK:(hh + 1) * RET_DK] = kr.astype(BF16)
        state_dec = jnp.exp(lg * (RET_CHUNK - 1.0 - pos))
        chunk_dec = jnp.exp(lg * jnp.full((1, RET_DV), float(RET_CHUNK), F32))
        for c in range(rows // RET_CHUNK):
            r0 = c * RET_CHUNK
            s = s_scr[hh]
            sf_ref[0, c, hh] = s.astype(BF16)
            kc = (kr[r0:r0 + RET_CHUNK] * state_dec).astype(BF16)
            vc = vv[r0:r0 + RET_CHUNK, hh * RET_DV:(hh + 1) * RET_DV]
            s_scr[hh] = s * chunk_dec + _dot_tn(kc, vc)

    span = ext - 8
    for r in range(8):
        yres_scr[r] = y_scr[pl.ds(r, span), :]

    for i in range(rows // CONV_BLOCK):
        base = i * CONV_BLOCK
        acc = None
        for w in range(CONV_WIDTH):
            a, r = divmod(w + 1, 8)
            wk = jnp.concatenate([cw_ref[8 * w:8 * w + 8, :]] * (CONV_BLOCK // 8), axis=0)
            tap = yres_scr[r, base + 8 * a:base + 8 * a + CONV_BLOCK, :] * wk
            acc = tap if acc is None else acc + tap
        z_scr[base:base + CONV_BLOCK, :] = acc
    z = _norm(z_scr[...] + cb_ref[...]) * clg_ref[...] + clb_ref[...]
    ya_ref[0] = _silu(z).astype(BF16)


def _proj(lgf, x, sh1, sc1, w_kvu, ck, sk, sf0, cw, cb, clg, clb):
    b, l, d = x.shape
    nt = l // PROJ_ROWS
    cpt = PROJ_ROWS // RET_CHUNK
    hpt = PROJ_ROWS // CONV_PAD
    n_halo = l // CONV_PAD
    ext = PROJ_ROWS + 2 * CONV_PAD
    tile = lambda w: pl.BlockSpec((1, PROJ_ROWS, w), lambda i, t: (i, t, 0))
    vec = pl.BlockSpec((1, 1, d), lambda i, t: (i, 0, 0))
    full = lambda a: pl.BlockSpec(a.shape, lambda i, t: (0,) * a.ndim, pipeline_mode=pl.Buffered(1))
    rope = pl.BlockSpec((PROJ_ROWS, RET_DK), lambda i, t: (t, 0))
    prev = pl.BlockSpec((1, CONV_PAD, d), lambda i, t: (i, jnp.maximum(t * hpt - 1, 0), 0))
    nxt = pl.BlockSpec((1, CONV_PAD, d), lambda i, t: (i, jnp.minimum((t + 1) * hpt, n_halo - 1), 0))
    return pl.pallas_call(
        _proj_kernel,
        grid=(b, nt),
        in_specs=[_smem(), prev, tile(d), nxt, vec, vec, full(w_kvu), rope, rope,
                  pl.BlockSpec((1, RET_HEADS, RET_DK, RET_DV), lambda i, t: (i, 0, 0, 0)),
                  full(cw), full(cb), full(clg), full(clb)],
        out_specs=(tile(d), tile(QK_DIM), tile(V_DIM), tile(CONV_DIM),
                   pl.BlockSpec((1, cpt, RET_HEADS, RET_DK, RET_DV), lambda i, t: (i, t, 0, 0, 0))),
        out_shape=(jax.ShapeDtypeStruct((b, l, d), BF16),
                   jax.ShapeDtypeStruct((b, l, QK_DIM), BF16),
                   jax.ShapeDtypeStruct((b, l, V_DIM), BF16),
                   jax.ShapeDtypeStruct((b, l, CONV_DIM), BF16),
                   jax.ShapeDtypeStruct((b, l // RET_CHUNK, RET_HEADS, RET_DK, RET_DV), BF16)),
        scratch_shapes=[pltpu.VMEM((RET_HEADS, RET_DK, RET_DV), F32),
                        pltpu.VMEM((ext, CONV_DIM), F32),
                        pltpu.VMEM((8, ext - 8, CONV_DIM), F32),
                        pltpu.VMEM((PROJ_ROWS, CONV_DIM), F32)],
        compiler_params=_params(("parallel", "arbitrary")),
        name="proj_kv_conv",
    )(lgf, x, x, x, sh1, sc1, w_kvu, ck, sk, sf0, cw, cb, clg, clb)


def _mixer_kernel(alpha, lgf_ref, lgb_ref, x_ref, h_ref, k_ref, v_ref, sf_ref, sb0_ref, ya_ref,
                  wq_ref, wg_ref, wco_ref, wro_ref, wo_ref, gng_ref,
                  g1_ref, sh2_ref, sc2_ref, l1g_ref, l1b_ref, wr_ref, cq_ref, sq_ref,
                  x1_ref, hp_ref, lt_ref, sb_scr):
    t = pl.program_id(1)

    @pl.when(t == 0)
    def _():
        sb_scr[...] = sb0_ref[0]

    h = h_ref[0]
    rows = h.shape[0]
    cq = cq_ref[...]
    sq = sq_ref[...]
    q_all = _dot(h, wq_ref[...])
    pos = _col_iota(RET_CHUNK)
    di = lax.broadcasted_iota(jnp.int32, (RET_CHUNK, RET_CHUNK), 0)
    dj = lax.broadcasted_iota(jnp.int32, (RET_CHUNK, RET_CHUNK), 1)
    dist = (di - dj).astype(F32)

    n_chunks = rows // RET_CHUNK
    o_parts = [[None] * RET_HEADS for _ in range(n_chunks)]
    for hh in range(RET_HEADS):
        lgf = lgf_ref[hh]
        lgb = lgb_ref[hh]
        qh = q_all[:, hh * RET_DK:(hh + 1) * RET_DK]
        qr = qh * cq + pltpu.roll(qh, RET_DK // 2, 1) * sq
        decay = jnp.where(dist >= 0.0, jnp.exp(lgf * jnp.maximum(dist, 0.0)),
                          jnp.exp(lgb * jnp.maximum(-dist, 0.0)))
        cross_f = jnp.exp(lgf * (pos + 1.0))
        cross_b = jnp.exp(lgb * (RET_CHUNK - pos))
        state_b = jnp.exp(lgb * pos)
        chunk_b = jnp.exp(lgb * jnp.full((1, RET_DV), float(RET_CHUNK), F32))
        for c in reversed(range(n_chunks)):
            r0 = c * RET_CHUNK
            qc = qr[r0:r0 + RET_CHUNK]
            kc = k_ref[0, r0:r0 + RET_CHUNK, hh * RET_DK:(hh + 1) * RET_DK]
            vc = v_ref[0, r0:r0 + RET_CHUNK, hh * RET_DV:(hh + 1) * RET_DV]
            sb = sb_scr[hh]
            att = _dot_nt(qc.astype(BF16), kc) * decay
            lhs = jnp.concatenate([att.astype(BF16), (qc * cross_f).astype(BF16), (qc * cross_b).astype(BF16)], axis=1)
            rhs = jnp.concatenate([vc, sf_ref[0, c, hh], sb.astype(BF16)], axis=0)
            o = _dot(lhs, rhs)
            kb = (kc.astype(F32) * state_b).astype(BF16)
            sb_scr[hh] = sb * chunk_b + _dot_tn(kb, vc)
            o_parts[c][hh] = _norm(o)
    on = jnp.concatenate([jnp.concatenate(o_parts[c], axis=1) for c in range(n_chunks)], axis=0)

    g_ret = _dot(h, wg_ref[:, :V_DIM])
    yb_in = (_silu(g_ret) * (on * gng_ref[...])).astype(BF16)
    y_b = _dot(yb_in, wro_ref[...])

    y_a = _dot(ya_ref[0], wco_ref[...])

    d = x_ref.shape[-1]
    g_a = _dot(h, wg_ref[:, V_DIM:V_DIM + d])
    g_b = _dot(h, wg_ref[:, V_DIM + d:])
    y = (_sigmoid(g_a) * y_a + _sigmoid(g_b) * y_b).astype(BF16)
    ym = _dot(y, wo_ref[...])

    x1 = _norm(alpha * x_ref[0] + g1_ref[0] * ym) * l1g_ref[...] + l1b_ref[...]
    x1_ref[0] = x1
    h2 = _norm(x1) * (1.0 + sc2_ref[0]) + sh2_ref[0]
    h2_hi, h2_lo = _split(h2)
    pw = d // (2 * PACK_PARTS)
    for p in range(PACK_PARTS):
        c0 = 2 * p * pw
        hp_ref[p, 0] = _pack_pair(h2_hi[:, c0:c0 + pw], h2_hi[:, c0 + pw:c0 + 2 * pw])
    wr_hi, wr_lo = _split(wr_ref[...])
    lt_ref[0] = _dot_nt(wr_hi, h2_hi) + (_dot_nt(wr_hi, h2_lo) + _dot_nt(wr_lo, h2_hi))


def _mixer(alpha, lgf, lgb, x, h, k, v, sf, sb0, ya, wq, wg, wco, wro, wo, gng,
           g1, sh2, sc2, l1g, l1b, wr_t, cq, sq):
    b, l, d = x.shape
    nt = l // MIX_ROWS
    cpt = MIX_ROWS // RET_CHUNK
    pw = d // (2 * PACK_PARTS)
    rev = lambda w: pl.BlockSpec((1, MIX_ROWS, w), lambda i, t: (i, nt - 1 - t, 0))
    vec = pl.BlockSpec((1, 1, d), lambda i, t: (i, 0, 0))
    full = lambda a: pl.BlockSpec(a.shape, lambda i, t: (0,) * a.ndim, pipeline_mode=pl.Buffered(1))
    rope = pl.BlockSpec((MIX_ROWS, RET_DK), lambda i, t: (nt - 1 - t, 0))
    state = pl.BlockSpec((1, RET_HEADS, RET_DK, RET_DV), lambda i, t: (i, 0, 0, 0))
    return pl.pallas_call(
        functools.partial(_mixer_kernel, alpha),
        grid=(b, nt),
        in_specs=[_smem(), _smem(), rev(d), rev(d), rev(QK_DIM), rev(V_DIM),
                  pl.BlockSpec((1, cpt, RET_HEADS, RET_DK, RET_DV), lambda i, t: (i, nt - 1 - t, 0, 0, 0)),
                  state, rev(CONV_DIM),
                  full(wq), full(wg), full(wco), full(wro), full(wo), full(gng),
                  vec, vec, vec, full(l1g), full(l1b), full(wr_t), rope, rope],
        out_specs=(rev(d),
                   pl.BlockSpec((PACK_PARTS, 1, MIX_ROWS, pw), lambda i, t: (0, i, nt - 1 - t, 0)),
                   pl.BlockSpec((1, N_EXPERTS, MIX_ROWS), lambda i, t: (i, 0, nt - 1 - t))),
        out_shape=(jax.ShapeDtypeStruct((b, l, d), F32),
                   jax.ShapeDtypeStruct((PACK_PARTS, b, l, pw), U32),
                   jax.ShapeDtypeStruct((b, N_EXPERTS, l), F32)),
        scratch_shapes=[pltpu.VMEM((RET_HEADS, RET_DK, RET_DV), F32)],
        compiler_params=_params(("parallel", "arbitrary")),
        name="mixer",
    )(lgf, lgb, x, h, k, v, sf, sb0, ya, wq, wg, wco, wro, wo, gng,
      g1, sh2, sc2, l1g, l1b, wr_t, cq, sq)


def _route_kernel(cap, lt_ref, rank_ref, aff_ref, idx_ref):
    logits = lt_ref[0]
    n_e, l = logits.shape
    m = jnp.max(logits, axis=0, keepdims=True)
    p = jnp.exp(logits - m)
    aff = p / jnp.sum(p, axis=0, keepdims=True)
    aff_ref[0] = aff
    capf = float(cap)

    def count(ones):
        return jnp.sum(ones, axis=1, keepdims=True)

    def refine(lo, shift, width, keep):
        best = lo
        for digit in range(1, 1 << width):
            cand = lo | jnp.left_shift(jnp.int32(digit), shift)
            best = jnp.where(keep(cand), cand, best)
        return best

    def bisect(n_bits, keep):
        lo = jnp.zeros((n_e, 1), jnp.int32)
        if n_bits % 2:
            lo = refine(lo, n_bits - 1, 1, keep)
        return lax.fori_loop(0, n_bits // 2, lambda i, v: refine(v, 2 * (n_bits // 2 - 1 - i), 2, keep), lo)

    thr = pltpu.bitcast(
        bisect(31, lambda cand: count(jnp.where(aff >= pltpu.bitcast(cand, F32), 1.0, 0.0)) >= capf), F32)
    gt = jnp.where(aff > thr, 1.0, 0.0)
    eq = jnp.where(aff == thr, 1.0, 0.0)
    need = capf - count(gt)
    idx = lax.broadcasted_iota(jnp.int32, (n_e, l), 1)
    idx_bits = int(l - 1).bit_length()

    last = bisect(idx_bits, lambda cand: count(jnp.where(idx < cand, eq, 0.0)) < need)
    bound = jnp.where(need > 0.0, last + 1, 0)
    sel = gt + jnp.where(idx < bound, eq, 0.0)

    blk = 128
    ti = lax.broadcasted_iota(jnp.int32, (blk, blk), 0)
    tj = lax.broadcasted_iota(jnp.int32, (blk, blk), 1)
    tri = jnp.where(ti < tj, 1.0, 0.0).astype(BF16)
    offset = jnp.zeros((n_e, 1), F32)
    for j in range(l // blk):
        sj = sel[:, j * blk:(j + 1) * blk]
        before = _dot(sj.astype(BF16), tri) + offset
        rank_ref[0, :, j * blk:(j + 1) * blk] = jnp.where(sj > 0.0, before, -1.0).astype(jnp.int32)
        offset = offset + jnp.sum(sj, axis=1, keepdims=True)

    rank = rank_ref[0]
    hi_digit = jnp.right_shift(rank, SLOT_SHIFT)
    lo_digit = jnp.bitwise_and(rank, SLOT_RADIX - 1)
    digit = lax.broadcasted_iota(jnp.int32, (SLOT_RADIX, l), 0)
    tok_hi = jnp.right_shift(idx[:1], TOKEN_SHIFT).astype(F32)
    tok_lo = jnp.bitwise_and(idx[:1], (1 << TOKEN_SHIFT) - 1).astype(F32)
    a_hi, a_lo, b_rows = [], [], []
    for e in range(n_e):
        is_a = hi_digit[e:e + 1] == digit
        a_hi.append(jnp.where(is_a, tok_hi, 0.0))
        a_lo.append(jnp.where(is_a, tok_lo, 0.0))
        b_rows.append(jnp.where(lo_digit[e:e + 1] == digit, 1.0, 0.0))
    b_all = jnp.concatenate(b_rows, axis=0).astype(BF16)
    cross_hi = _dot_nt(jnp.concatenate(a_hi, axis=0).astype(BF16), b_all)
    cross_lo = _dot_nt(jnp.concatenate(a_lo, axis=0).astype(BF16), b_all)
    cross = cross_hi * float(1 << TOKEN_SHIFT) + cross_lo + (pl.program_id(0) * l).astype(F32)
    for e in range(n_e):
        s0 = e * SLOT_RADIX
        idx_ref[0, e] = cross[s0:s0 + SLOT_RADIX, s0:s0 + SLOT_RADIX].astype(jnp.int32)


def _route(lt, cap):
    b, n_e, l = lt.shape
    assert cap == SLOT_RADIX * SLOT_RADIX and l <= (1 << TOKEN_SHIFT) * 256
    spec = pl.BlockSpec((1, n_e, l), lambda i: (i, 0, 0))
    return pl.pallas_call(
        functools.partial(_route_kernel, cap),
        grid=(b,),
        in_specs=[spec],
        out_specs=(spec, spec, pl.BlockSpec((1, n_e, SLOT_RADIX, SLOT_RADIX), lambda i: (i, 0, 0, 0))),
        out_shape=(jax.ShapeDtypeStruct((b, n_e, l), jnp.int32),
                   jax.ShapeDtypeStruct((b, n_e, l), F32),
                   jax.ShapeDtypeStruct((b, n_e, SLOT_RADIX, SLOT_RADIX), jnp.int32)),
        compiler_params=_params(("parallel",)),
        name="route_topc",
    )(lt)


def _sc_gather(rows, idx):
    n = idx.shape[0]
    w = rows.shape[1]
    mesh = plsc.VectorSubcoreMesh(core_axis_name="c", subcore_axis_name="s")

    @pl.kernel(out_type=jax.ShapeDtypeStruct((n, w), rows.dtype), mesh=mesh, scratch_types=[])
    def gather(rows_hbm, idx_hbm, out_hbm):
        def window(idx_vmem, out_vmem):
            pltpu.sync_copy(rows_hbm.at[idx_vmem.at[0]], out_vmem)

        pltpu.emit_pipeline(
            window,
            grid=(n // SC_WINDOW,),
            in_specs=[pl.BlockSpec((1, SC_WINDOW), index_map=lambda i: (0, i))],
            out_specs=[pl.BlockSpec((SC_WINDOW, w), index_map=lambda i: (i, 0))],
            core_axis_name=("c", "s"),
            dimension_semantics=(pltpu.PARALLEL,),
        )(idx_hbm, out_hbm)

    return gather(rows, idx.reshape(1, n))


def _moe_kernel(cap, alpha, rank_ref, aff_ref, xe_ref, wg_ref, wu_ref, wd_ref,
                x1_ref, g2_ref, lng_ref, lnb_ref, o_ref):
    e = pl.program_id(1)
    l = o_ref.shape[1]
    n_tiles = l // OUT_ROWS

    @pl.when(e == 0)
    def _():
        o_ref[0] = jnp.zeros(o_ref.shape[1:], F32)

    rank = rank_ref[0, 0]
    slot = lax.broadcasted_iota(jnp.int32, (cap, l), 0)
    hit = slot == rank
    pick = jnp.where(hit, 1.0, 0.0).astype(BF16)
    gate = jnp.sum(jnp.where(hit, aff_ref[0, 0], 0.0), axis=1, keepdims=True)
    halves = []
    for p in range(PACK_PARTS):
        halves.extend(_unpack_pair(xe_ref[p, 0, 0]))
    xe = jnp.concatenate(halves, axis=1)
    he = (_silu(_dot(xe, wg_ref[0])) * _dot(xe, wu_ref[0])).astype(BF16)
    ye = (_dot(he, wd_ref[0]) * gate).astype(BF16)
    for i in range(n_tiles):
        r0 = i * OUT_ROWS
        o_ref[0, r0:r0 + OUT_ROWS, :] += _dot_tn(pick[:, r0:r0 + OUT_ROWS], ye)

    @pl.when(e == pl.num_programs(1) - 1)
    def _():
        def tile(i, carry):
            r0 = pl.multiple_of(i * OUT_ROWS, OUT_ROWS)
            rows = pl.ds(r0, OUT_ROWS)
            z = alpha * x1_ref[0, rows, :] + g2_ref[0] * o_ref[0, rows, :]
            o_ref[0, rows, :] = _norm(z) * lng_ref[...] + lnb_ref[...]
            return carry

        lax.fori_loop(0, n_tiles, tile, 0)


def _moe(alpha, rank, aff, xe, wgate, wup, wdown, x1, g2, ln_g, ln_b, cap):
    b, l, d = x1.shape
    n_e = wgate.shape[0]
    f = wgate.shape[2]
    row = pl.BlockSpec((1, 1, 1, l), lambda i, e: (i, e, 0, 0))
    vec = pl.BlockSpec((1, d), lambda i, e: (0, 0))
    return pl.pallas_call(
        functools.partial(_moe_kernel, cap, alpha),
        grid=(b, n_e),
        in_specs=[row, row,
                  pl.BlockSpec((PACK_PARTS, 1, 1, cap, xe.shape[-1]), lambda i, e: (0, i, e, 0, 0)),
                  pl.BlockSpec((1, d, f), lambda i, e: (e, 0, 0)),
                  pl.BlockSpec((1, d, f), lambda i, e: (e, 0, 0)),
                  pl.BlockSpec((1, f, d), lambda i, e: (e, 0, 0)),
                  pl.BlockSpec((1, l, d), lambda i, e: (i, 0, 0), pipeline_mode=pl.Buffered(1)),
                  pl.BlockSpec((1, 1, d), lambda i, e: (i, 0, 0)), vec, vec],
        out_specs=pl.BlockSpec((1, l, d), lambda i, e: (i, 0, 0)),
        out_shape=jax.ShapeDtypeStruct((b, l, d), F32),
        compiler_params=_params(("parallel", "arbitrary")),
        name="moe_experts",
    )(rank.reshape(b, n_e, 1, l), aff.reshape(b, n_e, 1, l), xe, wgate, wup, wdown, x1, g2, ln_g, ln_b)


def _rope_tables(l):
    n_axis = RET_DK // 4
    freqs = ROPE_BASE ** (-np.arange(n_axis, dtype=np.float64) / n_axis)
    pos = np.arange(l)
    ang = np.concatenate([(pos // GRID_W)[:, None] * freqs, (pos % GRID_W)[:, None] * freqs], axis=-1)
    cos, sin = np.cos(ang), np.sin(ang)
    return (np.concatenate([cos, cos], axis=-1).astype(np.float32),
            np.concatenate([-sin, sin], axis=-1).astype(np.float32))


def kernel(x, c, ctx, c_ctx, w_ada, b_ada, w_in, conv_w, conv_b, conv_ln_g, conv_ln_b, w_conv_out,
           log_decay_f, log_decay_b, ret_gn_g, w_ret_out, w_out, ln1_g, ln1_b,
           w_router, w_gate, w_up, w_down, ln2_g, ln2_b):
    depth = w_ada.shape[0]
    assert depth == 1, "single trunk layer"
    b, l, d = x.shape
    alpha = (2.0 * depth) ** 0.25
    cap = EC_FACTOR * l // N_EXPERTS
    u_end = 2 * CONV_DIM
    q_end = u_end + QK_DIM
    k_end = q_end + QK_DIM
    v_end = k_end + V_DIM
    row = lambda a: a.reshape(1, -1)

    n_mod = b + 1
    pad = (-n_mod) % 8
    cc = jnp.concatenate([c, c_ctx[None], jnp.zeros((pad, d), F32)], axis=0)
    mod = _ada(cc, w_ada[0], row(b_ada[0]))
    sh1, sc1, g1, sh2, sc2, g2 = [m.reshape(b, 1, d) for m in jnp.split(mod[:b], 6, axis=-1)]
    csh1, csc1 = mod[b:b + 1, :d], mod[b:b + 1, d:2 * d]

    perm = np.concatenate([np.arange(0, RET_DK, 2), np.arange(1, RET_DK, 2)])
    perm = (np.arange(RET_HEADS)[:, None] * RET_DK + perm[None, :]).reshape(-1)
    w = w_in[0]
    w_kvu = jnp.concatenate([w[:, q_end:k_end][:, perm], w[:, k_end:v_end], w[:, :u_end]], axis=1).astype(BF16)
    wq = w[:, u_end:q_end][:, perm].astype(BF16)
    wg = w[:, v_end:].astype(BF16)

    cos_t, sin_t = _rope_tables(l)
    cq, sq = jnp.asarray(cos_t), jnp.asarray(sin_t)
    k_scale = RET_DK ** -0.5
    ck, sk = jnp.asarray(cos_t * k_scale), jnp.asarray(sin_t * k_scale)

    lgf, lgb = log_decay_f[0], log_decay_b[0]
    sf0, sb0 = _ctx_states(lgf, lgb, ctx, csh1, csc1, w_kvu)
    cw8 = jnp.repeat(conv_w[0], 8, axis=0)
    h, k, v, ya, sf = _proj(lgf, x, sh1, sc1, w_kvu, ck, sk, sf0,
                            cw8, row(conv_b[0]), row(conv_ln_g[0]), row(conv_ln_b[0]))
    x1, hp, lt = _mixer(alpha, lgf, lgb, x, h, k, v, sf, sb0, ya, wq, wg,
                        w_conv_out[0].astype(BF16), w_ret_out[0].astype(BF16), w_out[0].astype(BF16),
                        row(ret_gn_g[0]), g1, sh2, sc2, row(ln1_g[0]), row(ln1_b[0]), w_router[0].T, cq, sq)
    rank, aff, slots = _route(lt, cap)
    part_rows = b * l
    ids = jnp.concatenate([slots.reshape(-1) + p * part_rows for p in range(PACK_PARTS)])
    xe = _sc_gather(hp.reshape(PACK_PARTS * part_rows, hp.shape[-1]), ids)
    xe = xe.reshape(PACK_PARTS, b, N_EXPERTS, cap, hp.shape[-1])
    return _moe(alpha, rank, aff, xe, w_gate[0].astype(BF16), w_up[0].astype(BF16), w_down[0].astype(BF16),
                x1, g2, row(ln2_g[0]), row(ln2_b[0]), cap)
```

```python
import functools

import numpy as np
import jax
import jax.numpy as jnp
from jax import lax
from jax.experimental import pallas as pl
from jax.experimental.pallas import tpu as pltpu
from jax.experimental.pallas import tpu_sc as plsc

F32 = jnp.float32
BF16 = jnp.bfloat16
U32 = jnp.uint32

GRID_W = 64
CONV_DIM = 512
CONV_WIDTH = 31
CONV_HALO = CONV_WIDTH // 2
RET_HEADS = 4
RET_DK = 128
RET_DV = 256
RET_CHUNK = 128
ROPE_BASE = 10000.0
QK_DIM = RET_HEADS * RET_DK
V_DIM = RET_HEADS * RET_DV
N_EXPERTS = 16
EC_FACTOR = 2
LN_EPS = 1e-5

ADA_COLS = 768
PROJ_ROWS = 512
MIX_ROWS = 512
CONV_PAD = 16
CONV_BLOCK = 32
OUT_ROWS = 512
PACK_PARTS = 2
SC_WINDOW = 128
MOE_BATCH_GROUPS = 2
SLOT_SHIFT = 4
SLOT_RADIX = 1 << SLOT_SHIFT
TOKEN_SHIFT = 6
V7X_VMEM_LIMIT = 56 * 1024 * 1024


def _dot(a, b):
    return jnp.dot(a, b, preferred_element_type=F32)


def _dot_nt(a, b):
    return lax.dot_general(a, b, (((1,), (1,)), ((), ())), preferred_element_type=F32)


def _dot_tn(a, b):
    return lax.dot_general(a, b, (((0,), (0,)), ((), ())), preferred_element_type=F32)


def _split(a):
    hi = a.astype(BF16)
    lo = (a - hi.astype(F32)).astype(BF16)
    return hi, lo


def _norm(x):
    mu = jnp.mean(x, axis=-1, keepdims=True)
    xc = x - mu
    var = jnp.mean(xc * xc, axis=-1, keepdims=True)
    return xc * lax.rsqrt(var + LN_EPS)


def _sigmoid(x):
    return 1.0 / (1.0 + jnp.exp(-x))


def _silu(x):
    return x * _sigmoid(x)


def _col_iota(n):
    return lax.broadcasted_iota(jnp.int32, (n, 1), 0).astype(F32)


def _pack_pair(lo, hi):
    lo_bits = pltpu.bitcast(lo.astype(F32), U32)
    hi_bits = pltpu.bitcast(hi.astype(F32), U32)
    return lax.shift_right_logical(lo_bits, jnp.uint32(16)) | hi_bits


def _unpack_pair(word):
    lo = pltpu.bitcast(lax.shift_left(word, jnp.uint32(16)), F32)
    hi = pltpu.bitcast(word & jnp.uint32(0xFFFF0000), F32)
    return lo.astype(BF16), hi.astype(BF16)


def _smem():
    return pl.BlockSpec(memory_space=pltpu.SMEM)


def _params(sem, vmem=V7X_VMEM_LIMIT):
    return pltpu.CompilerParams(dimension_semantics=sem, vmem_limit_bytes=vmem)


def _ada_kernel(c_ref, w_ref, b_ref, o_ref):
    a_hi, a_lo = _split(_silu(c_ref[...]))
    w_hi, w_lo = _split(w_ref[...])
    o_ref[...] = _dot(a_hi, w_hi) + (_dot(a_hi, w_lo) + _dot(a_lo, w_hi)) + b_ref[...]


def _ada(cc, w, b):
    m, d = cc.shape
    n = w.shape[1]
    return pl.pallas_call(
        _ada_kernel,
        grid=(n // ADA_COLS,),
        in_specs=[pl.BlockSpec((m, d), lambda j: (0, 0)),
                  pl.BlockSpec((d, ADA_COLS), lambda j: (0, j)),
                  pl.BlockSpec((1, ADA_COLS), lambda j: (0, j))],
        out_specs=pl.BlockSpec((m, ADA_COLS), lambda j: (0, j)),
        out_shape=jax.ShapeDtypeStruct((m, n), F32),
        compiler_params=_params(("parallel",)),
        name="ada_proj",
    )(cc, w, b)


def _ctx_kernel(lgf_ref, lgb_ref, ctx_ref, sh_ref, sc_ref, w_ref, sf_ref, sb_ref):
    x = ctx_ref[0]
    lc = x.shape[0]
    h = (_norm(x) * (1.0 + sc_ref[...]) + sh_ref[...]).astype(BF16)
    kv = _dot(h, w_ref[...])
    t = _col_iota(lc)
    for hh in range(RET_HEADS):
        k = kv[:, hh * RET_DK:(hh + 1) * RET_DK] * (RET_DK ** -0.5)
        v = kv[:, QK_DIM + hh * RET_DV:QK_DIM + (hh + 1) * RET_DV].astype(BF16)
        wf = jnp.exp(lgf_ref[hh] * (lc - 1.0 - t))
        wb = jnp.exp(lgb_ref[hh] * t)
        sf_ref[0, hh] = _dot_tn((k * wf).astype(BF16), v)
        sb_ref[0, hh] = _dot_tn((k * wb).astype(BF16), v)


def _ctx_states(lgf, lgb, ctx, csh, csc, w_kvu):
    b, lc, d = ctx.shape
    kvw = QK_DIM + V_DIM
    st = jax.ShapeDtypeStruct((b, RET_HEADS, RET_DK, RET_DV), F32)
    st_spec = pl.BlockSpec((1, RET_HEADS, RET_DK, RET_DV), lambda i: (i, 0, 0, 0))
    return pl.pallas_call(
        _ctx_kernel,
        grid=(b,),
        in_specs=[_smem(), _smem(),
                  pl.BlockSpec((1, lc, d), lambda i: (i, 0, 0)),
                  pl.BlockSpec((1, d), lambda i: (0, 0)),
                  pl.BlockSpec((1, d), lambda i: (0, 0)),
                  pl.BlockSpec((d, kvw), lambda i: (0, 0))],
        out_specs=(st_spec, st_spec),
        out_shape=(st, st),
        compiler_params=_params(("parallel",)),
        name="ctx_states",
    )(lgf, lgb, ctx, csh, csc, w_kvu)


def _proj_kernel(lgf_ref, xp_ref, x_ref, xn_ref, sh_ref, sc_ref, w_ref, ck_ref, sk_ref, sf0_ref,
                 cw_ref, cb_ref, clg_ref, clb_ref,
                 h_ref, k_ref, v_ref, ya_ref, sf_ref, s_scr, y_scr, yres_scr, z_scr):
    t = pl.program_id(1)
    nt = pl.num_programs(1)

    @pl.when(t == 0)
    def _():
        s_scr[...] = sf0_ref[0]

    rows = x_ref.shape[1]
    ext = rows + 2 * CONV_PAD
    x_ext = jnp.concatenate([xp_ref[0], x_ref[0], xn_ref[0]], axis=0)
    h_ext = (_norm(x_ext) * (1.0 + sc_ref[0]) + sh_ref[0]).astype(BF16)
    h = h_ext[CONV_PAD:CONV_PAD + rows]
    h_ref[0] = h
    kvw = QK_DIM + V_DIM
    u = _dot(h_ext, w_ref[:, kvw:])
    row = lax.broadcasted_iota(jnp.int32, (ext, 1), 0)
    head_ok = jnp.where(t == 0, 0.0, 1.0)
    tail_ok = jnp.where(t == nt - 1, 0.0, 1.0)
    inside = jnp.where(row < CONV_PAD, head_ok, jnp.where(row >= CONV_PAD + rows, tail_ok, 1.0))
    y_scr[...] = u[:, :CONV_DIM] * _sigmoid(u[:, CONV_DIM:]) * inside

    kk = _dot(h, w_ref[:, :QK_DIM])
    vv = _dot(h, w_ref[:, QK_DIM:kvw]).astype(BF16)
    v_ref[0] = vv
    ck = ck_ref[...]
    sk = sk_ref[...]
    pos = _col_iota(RET_CHUNK)
    for hh in range(RET_HEADS):
        lg = lgf_ref[hh]
        kh = kk[:, hh * RET_DK:(hh + 1) * RET_DK]
        kr = kh * ck + pltpu.roll(kh, RET_DK // 2, 1) * sk
        k_ref[0, :, hh * RET_DK:(hh + 1) * RET_DK] = kr.astype(BF16)
        state_dec = jnp.exp(lg * (RET_CHUNK - 1.0 - pos))
        chunk_dec = jnp.exp(lg * jnp.full((1, RET_DV), float(RET_CHUNK), F32))
        for c in range(rows // RET_CHUNK):
            r0 = c * RET_CHUNK
            s = s_scr[hh]
            sf_ref[0, c, hh] = s.astype(BF16)
            kc = (kr[r0:r0 + RET_CHUNK] * state_dec).astype(BF16)
            vc = vv[r0:r0 + RET_CHUNK, hh * RET_DV:(hh + 1) * RET_DV]
            s_scr[hh] = s * chunk_dec + _dot_tn(kc, vc)

    span = ext - 8
    for r in range(8):
        yres_scr[r] = y_scr[pl.ds(r, span), :]

    for i in range(rows // CONV_BLOCK):
        base = i * CONV_BLOCK
        acc = None
        for w in range(CONV_WIDTH):
            a, r = divmod(w + 1, 8)
            wk = jnp.concatenate([cw_ref[8 * w:8 * w + 8, :]] * (CONV_BLOCK // 8), axis=0)
            tap = yres_scr[r, base + 8 * a:base + 8 * a + CONV_BLOCK, :] * wk
            acc = tap if acc is None else acc + tap
        z_scr[base:base + CONV_BLOCK, :] = acc
    z = _norm(z_scr[...] + cb_ref[...]) * clg_ref[...] + clb_ref[...]
    ya_ref[0] = _silu(z).astype(BF16)


def _proj(lgf, x, sh1, sc1, w_kvu, ck, sk, sf0, cw, cb, clg, clb):
    b, l, d = x.shape
    nt = l // PROJ_ROWS
    cpt = PROJ_ROWS // RET_CHUNK
    hpt = PROJ_ROWS // CONV_PAD
    n_halo = l // CONV_PAD
    ext = PROJ_ROWS + 2 * CONV_PAD
    tile = lambda w: pl.BlockSpec((1, PROJ_ROWS, w), lambda i, t: (i, t, 0))
    vec = pl.BlockSpec((1, 1, d), lambda i, t: (i, 0, 0))
    full = lambda a: pl.BlockSpec(a.shape, lambda i, t: (0,) * a.ndim, pipeline_mode=pl.Buffered(1))
    rope = pl.BlockSpec((PROJ_ROWS, RET_DK), lambda i, t: (t, 0))
    prev = pl.BlockSpec((1, CONV_PAD, d), lambda i, t: (i, jnp.maximum(t * hpt - 1, 0), 0))
    nxt = pl.BlockSpec((1, CONV_PAD, d), lambda i, t: (i, jnp.minimum((t + 1) * hpt, n_halo - 1), 0))
    return pl.pallas_call(
        _proj_kernel,
        grid=(b, nt),
        in_specs=[_smem(), prev, tile(d), nxt, vec, vec, full(w_kvu), rope, rope,
                  pl.BlockSpec((1, RET_HEADS, RET_DK, RET_DV), lambda i, t: (i, 0, 0, 0)),
                  full(cw), full(cb), full(clg), full(clb)],
        out_specs=(tile(d), tile(QK_DIM), tile(V_DIM), tile(CONV_DIM),
                   pl.BlockSpec((1, cpt, RET_HEADS, RET_DK, RET_DV), lambda i, t: (i, t, 0, 0, 0))),
        out_shape=(jax.ShapeDtypeStruct((b, l, d), BF16),
                   jax.ShapeDtypeStruct((b, l, QK_DIM), BF16),
                   jax.ShapeDtypeStruct((b, l, V_DIM), BF16),
                   jax.ShapeDtypeStruct((b, l, CONV_DIM), BF16),
                   jax.ShapeDtypeStruct((b, l // RET_CHUNK, RET_HEADS, RET_DK, RET_DV), BF16)),
        scratch_shapes=[pltpu.VMEM((RET_HEADS, RET_DK, RET_DV), F32),
                        pltpu.VMEM((ext, CONV_DIM), F32),
                        pltpu.VMEM((8, ext - 8, CONV_DIM), F32),
                        pltpu.VMEM((PROJ_ROWS, CONV_DIM), F32)],
        compiler_params=_params(("parallel", "arbitrary")),
        name="proj_kv_conv",
    )(lgf, x, x, x, sh1, sc1, w_kvu, ck, sk, sf0, cw, cb, clg, clb)


def _mixer_kernel(alpha, lgf_ref, lgb_ref, x_ref, h_ref, k_ref, v_ref, sf_ref, sb0_ref, ya_ref,
                  wq_ref, wg_ref, wco_ref, wro_ref, wo_ref, gng_ref,
                  g1_ref, sh2_ref, sc2_ref, l1g_ref, l1b_ref, wr_ref, cq_ref, sq_ref,
                  x1_ref, hp_ref, lt_ref, sb_scr):
    t = pl.program_id(1)

    @pl.when(t == 0)
    def _():
        sb_scr[...] = sb0_ref[0]

    h = h_ref[0]
    rows = h.shape[0]
    cq = cq_ref[...]
    sq = sq_ref[...]
    q_all = _dot(h, wq_ref[...])
    pos = _col_iota(RET_CHUNK)
    di = lax.broadcasted_iota(jnp.int32, (RET_CHUNK, RET_CHUNK), 0)
    dj = lax.broadcasted_iota(jnp.int32, (RET_CHUNK, RET_CHUNK), 1)
    dist = (di - dj).astype(F32)

    n_chunks = rows // RET_CHUNK
    o_parts = [[None] * RET_HEADS for _ in range(n_chunks)]
    for hh in range(RET_HEADS):
        lgf = lgf_ref[hh]
        lgb = lgb_ref[hh]
        qh = q_all[:, hh * RET_DK:(hh + 1) * RET_DK]
        qr = qh * cq + pltpu.roll(qh, RET_DK // 2, 1) * sq
        decay = jnp.where(dist >= 0.0, jnp.exp(lgf * jnp.maximum(dist, 0.0)),
                          jnp.exp(lgb * jnp.maximum(-dist, 0.0)))
        cross_f = jnp.exp(lgf * (pos + 1.0))
        cross_b = jnp.exp(lgb * (RET_CHUNK - pos))
        state_b = jnp.exp(lgb * pos)
        chunk_b = jnp.exp(lgb * jnp.full((1, RET_DV), float(RET_CHUNK), F32))
        for c in reversed(range(n_chunks)):
            r0 = c * RET_CHUNK
            qc = qr[r0:r0 + RET_CHUNK]
            kc = k_ref[0, r0:r0 + RET_CHUNK, hh * RET_DK:(hh + 1) * RET_DK]
            vc = v_ref[0, r0:r0 + RET_CHUNK, hh * RET_DV:(hh + 1) * RET_DV]
            sb = sb_scr[hh]
            att = _dot_nt(qc.astype(BF16), kc) * decay
            lhs = jnp.concatenate([att.astype(BF16), (qc * cross_f).astype(BF16), (qc * cross_b).astype(BF16)], axis=1)
            rhs = jnp.concatenate([vc, sf_ref[0, c, hh], sb.astype(BF16)], axis=0)
            o = _dot(lhs, rhs)
            kb = (kc.astype(F32) * state_b).astype(BF16)
            sb_scr[hh] = sb * chunk_b + _dot_tn(kb, vc)
            o_parts[c][hh] = _norm(o)
    on = jnp.concatenate([jnp.concatenate(o_parts[c], axis=1) for c in range(n_chunks)], axis=0)

    g_ret = _dot(h, wg_ref[:, :V_DIM])
    yb_in = (_silu(g_ret) * (on * gng_ref[...])).astype(BF16)
    y_b = _dot(yb_in, wro_ref[...])

    y_a = _dot(ya_ref[0], wco_ref[...])

    d = x_ref.shape[-1]
    g_a = _dot(h, wg_ref[:, V_DIM:V_DIM + d])
    g_b = _dot(h, wg_ref[:, V_DIM + d:])
    y = (_sigmoid(g_a) * y_a + _sigmoid(g_b) * y_b).astype(BF16)
    ym = _dot(y, wo_ref[...])

    x1 = _norm(alpha * x_ref[0] + g1_ref[0] * ym) * l1g_ref[...] + l1b_ref[...]
    x1_ref[0] = x1
    h2 = _norm(x1) * (1.0 + sc2_ref[0]) + sh2_ref[0]
    h2_hi, h2_lo = _split(h2)
    pw = d // (2 * PACK_PARTS)
    for p in range(PACK_PARTS):
        c0 = 2 * p * pw
        hp_ref[p, 0] = _pack_pair(h2_hi[:, c0:c0 + pw], h2_hi[:, c0 + pw:c0 + 2 * pw])
    wr_hi, wr_lo = _split(wr_ref[...])
    lt_ref[0] = _dot_nt(wr_hi, h2_hi) + (_dot_nt(wr_hi, h2_lo) + _dot_nt(wr_lo, h2_hi))


def _mixer(alpha, lgf, lgb, x, h, k, v, sf, sb0, ya, wq, wg, wco, wro, wo, gng,
           g1, sh2, sc2, l1g, l1b, wr_t, cq, sq):
    b, l, d = x.shape
    nt = l // MIX_ROWS
    cpt = MIX_ROWS // RET_CHUNK
    pw = d // (2 * PACK_PARTS)
    rev = lambda w: pl.BlockSpec((1, MIX_ROWS, w), lambda i, t: (i, nt - 1 - t, 0))
    vec = pl.BlockSpec((1, 1, d), lambda i, t: (i, 0, 0))
    full = lambda a: pl.BlockSpec(a.shape, lambda i, t: (0,) * a.ndim, pipeline_mode=pl.Buffered(1))
    rope = pl.BlockSpec((MIX_ROWS, RET_DK), lambda i, t: (nt - 1 - t, 0))
    state = pl.BlockSpec((1, RET_HEADS, RET_DK, RET_DV), lambda i, t: (i, 0, 0, 0))
    return pl.pallas_call(
        functools.partial(_mixer_kernel, alpha),
        grid=(b, nt),
        in_specs=[_smem(), _smem(), rev(d), rev(d), rev(QK_DIM), rev(V_DIM),
                  pl.BlockSpec((1, cpt, RET_HEADS, RET_DK, RET_DV), lambda i, t: (i, nt - 1 - t, 0, 0, 0)),
                  state, rev(CONV_DIM),
                  full(wq), full(wg), full(wco), full(wro), full(wo), full(gng),
                  vec, vec, vec, full(l1g), full(l1b), full(wr_t), rope, rope],
        out_specs=(rev(d),
                   pl.BlockSpec((PACK_PARTS, 1, MIX_ROWS, pw), lambda i, t: (0, i, nt - 1 - t, 0)),
                   pl.BlockSpec((1, N_EXPERTS, MIX_ROWS), lambda i, t: (i, 0, nt - 1 - t))),
        out_shape=(jax.ShapeDtypeStruct((b, l, d), F32),
                   jax.ShapeDtypeStruct((PACK_PARTS, b, l, pw), U32),
                   jax.ShapeDtypeStruct((b, N_EXPERTS, l), F32)),
        scratch_shapes=[pltpu.VMEM((RET_HEADS, RET_DK, RET_DV), F32)],
        compiler_params=_params(("parallel", "arbitrary")),
        name="mixer",
    )(lgf, lgb, x, h, k, v, sf, sb0, ya, wq, wg, wco, wro, wo, gng,
      g1, sh2, sc2, l1g, l1b, wr_t, cq, sq)


def _route_kernel(cap, lt_ref, rank_ref, aff_ref, idx_ref):
    logits = lt_ref[0]
    n_e, l = logits.shape
    m = jnp.max(logits, axis=0, keepdims=True)
    p = jnp.exp(logits - m)
    aff = p / jnp.sum(p, axis=0, keepdims=True)
    aff_ref[0] = aff
    capf = float(cap)

    def count(ones):
        return jnp.sum(ones, axis=1, keepdims=True)

    def refine(lo, shift, width, keep):
        best = lo
        for digit in range(1, 1 << width):
            cand = lo | jnp.left_shift(jnp.int32(digit), shift)
            best = jnp.where(keep(cand), cand, best)
        return best

    def bisect(n_bits, keep):
        lo = jnp.zeros((n_e, 1), jnp.int32)
        if n_bits % 2:
            lo = refine(lo, n_bits - 1, 1, keep)
        return lax.fori_loop(0, n_bits // 2, lambda i, v: refine(v, 2 * (n_bits // 2 - 1 - i), 2, keep), lo)

    thr = pltpu.bitcast(
        bisect(31, lambda cand: count(jnp.where(aff >= pltpu.bitcast(cand, F32), 1.0, 0.0)) >= capf), F32)
    gt = jnp.where(aff > thr, 1.0, 0.0)
    eq = jnp.where(aff == thr, 1.0, 0.0)
    need = capf - count(gt)
    idx = lax.broadcasted_iota(jnp.int32, (n_e, l), 1)
    idx_bits = int(l - 1).bit_length()

    last = bisect(idx_bits, lambda cand: count(jnp.where(idx < cand, eq, 0.0)) < need)
    bound = jnp.where(need > 0.0, last + 1, 0)
    sel = gt + jnp.where(idx < bound, eq, 0.0)

    blk = 128
    ti = lax.broadcasted_iota(jnp.int32, (blk, blk), 0)
    tj = lax.broadcasted_iota(jnp.int32, (blk, blk), 1)
    tri = jnp.where(ti < tj, 1.0, 0.0).astype(BF16)
    offset = jnp.zeros((n_e, 1), F32)
    for j in range(l // blk):
        sj = sel[:, j * blk:(j + 1) * blk]
        before = _dot(sj.astype(BF16), tri) + offset
        rank_ref[0, :, j * blk:(j + 1) * blk] = jnp.where(sj > 0.0, before, -1.0).astype(jnp.int32)
        offset = offset + jnp.sum(sj, axis=1, keepdims=True)

    rank = rank_ref[0]
    hi_digit = jnp.right_shift(rank, SLOT_SHIFT)
    lo_digit = jnp.bitwise_and(rank, SLOT_RADIX - 1)
    digit = lax.broadcasted_iota(jnp.int32, (SLOT_RADIX, l), 0)
    tok_hi = jnp.right_shift(idx[:1], TOKEN_SHIFT).astype(F32)
    tok_lo = jnp.bitwise_and(idx[:1], (1 << TOKEN_SHIFT) - 1).astype(F32)
    a_hi, a_lo, b_rows = [], [], []
    for e in range(n_e):
        is_a = hi_digit[e:e + 1] == digit
        a_hi.append(jnp.where(is_a, tok_hi, 0.0))
        a_lo.append(jnp.where(is_a, tok_lo, 0.0))
        b_rows.append(jnp.where(lo_digit[e:e + 1] == digit, 1.0, 0.0))
    b_all = jnp.concatenate(b_rows, axis=0).astype(BF16)
    cross_hi = _dot_nt(jnp.concatenate(a_hi, axis=0).astype(BF16), b_all)
    cross_lo = _dot_nt(jnp.concatenate(a_lo, axis=0).astype(BF16), b_all)
    cross = cross_hi * float(1 << TOKEN_SHIFT) + cross_lo + (pl.program_id(0) * l).astype(F32)
    for e in range(n_e):
        s0 = e * SLOT_RADIX
        idx_ref[0, e] = cross[s0:s0 + SLOT_RADIX, s0:s0 + SLOT_RADIX].astype(jnp.int32)


def _route(lt, cap):
    b, n_e, l = lt.shape
    assert cap == SLOT_RADIX * SLOT_RADIX and l <= (1 << TOKEN_SHIFT) * 256
    spec = pl.BlockSpec((1, n_e, l), lambda i: (i, 0, 0))
    return pl.pallas_call(
        functools.partial(_route_kernel, cap),
        grid=(b,),
        in_specs=[spec],
        out_specs=(spec, spec, pl.BlockSpec((1, n_e, SLOT_RADIX, SLOT_RADIX), lambda i: (i, 0, 0, 0))),
        out_shape=(jax.ShapeDtypeStruct((b, n_e, l), jnp.int32),
                   jax.ShapeDtypeStruct((b, n_e, l), F32),
                   jax.ShapeDtypeStruct((b, n_e, SLOT_RADIX, SLOT_RADIX), jnp.int32)),
        compiler_params=_params(("parallel",)),
        name="route_topc",
    )(lt)


def _sc_gather(rows, idx):
    n = idx.shape[0]
    w = rows.shape[1]
    mesh = plsc.VectorSubcoreMesh(core_axis_name="c", subcore_axis_name="s")

    @pl.kernel(out_type=jax.ShapeDtypeStruct((n, w), rows.dtype), mesh=mesh, scratch_types=[])
    def gather(rows_hbm, idx_hbm, out_hbm):
        def window(idx_vmem, out_vmem):
            pltpu.sync_copy(rows_hbm.at[idx_vmem.at[0]], out_vmem)

        pltpu.emit_pipeline(
            window,
            grid=(n // SC_WINDOW,),
            in_specs=[pl.BlockSpec((1, SC_WINDOW), index_map=lambda i: (0, i))],
            out_specs=[pl.BlockSpec((SC_WINDOW, w), index_map=lambda i: (i, 0))],
            core_axis_name=("c", "s"),
            dimension_semantics=(pltpu.PARALLEL,),
        )(idx_hbm, out_hbm)

    return gather(rows, idx.reshape(1, n))


def _moe_kernel(cap, alpha, rank_ref, aff_ref, xe_ref, wg_ref, wu_ref, wd_ref,
                x1_ref, g2_ref, lng_ref, lnb_ref, *rest):
    o_ref = rest[-1]
    e = pl.program_id(1)
    l = o_ref.shape[1]
    n_tiles = l // OUT_ROWS

    @pl.when(e == 0)
    def _():
        o_ref[0] = jnp.zeros(o_ref.shape[1:], F32)

    rank = rank_ref[0, 0]
    slot = lax.broadcasted_iota(jnp.int32, (cap, l), 0)
    hit = slot == rank
    pick = jnp.where(hit, 1.0, 0.0).astype(BF16)
    gate = jnp.sum(jnp.where(hit, aff_ref[0, 0], 0.0), axis=1, keepdims=True)
    halves = []
    for p in range(PACK_PARTS):
        halves.extend(_unpack_pair(xe_ref[p, 0, 0]))
    xe = jnp.concatenate(halves, axis=1)
    he = (_silu(_dot(xe, wg_ref[0])) * _dot(xe, wu_ref[0])).astype(BF16)
    ye = (_dot(he, wd_ref[0]) * gate).astype(BF16)
    for i in range(n_tiles):
        r0 = i * OUT_ROWS
        o_ref[0, r0:r0 + OUT_ROWS, :] += _dot_tn(pick[:, r0:r0 + OUT_ROWS], ye)

    @pl.when(e == pl.num_programs(1) - 1)
    def _():
        def tile(i, carry):
            r0 = pl.multiple_of(i * OUT_ROWS, OUT_ROWS)
            rows = pl.ds(r0, OUT_ROWS)
            z = alpha * x1_ref[0, rows, :] + g2_ref[0] * o_ref[0, rows, :]
            o_ref[0, rows, :] = _norm(z) * lng_ref[...] + lnb_ref[...]
            return carry

        lax.fori_loop(0, n_tiles, tile, 0)


def _moe(alpha, rank, aff, xe, wgate, wup, wdown, x1, g2, ln_g, ln_b, cap, b0, earlier):
    b, l, d = x1.shape
    nb = xe.shape[1]
    n_e = wgate.shape[0]
    f = wgate.shape[2]
    row = pl.BlockSpec((1, 1, 1, l), lambda i, e: (i + b0, e, 0, 0))
    vec = pl.BlockSpec((1, d), lambda i, e: (0, 0))
    in_specs = [row, row,
                pl.BlockSpec((PACK_PARTS, 1, 1, cap, xe.shape[-1]), lambda i, e: (0, i, e, 0, 0)),
                pl.BlockSpec((1, d, f), lambda i, e: (e, 0, 0)),
                pl.BlockSpec((1, d, f), lambda i, e: (e, 0, 0)),
                pl.BlockSpec((1, f, d), lambda i, e: (e, 0, 0)),
                pl.BlockSpec((1, l, d), lambda i, e: (i + b0, 0, 0), pipeline_mode=pl.Buffered(1)),
                pl.BlockSpec((1, 1, d), lambda i, e: (i + b0, 0, 0)), vec, vec]
    args = [rank.reshape(b, n_e, 1, l), aff.reshape(b, n_e, 1, l), xe, wgate, wup, wdown, x1, g2, ln_g, ln_b]
    aliases = {}
    if earlier is not None:
        in_specs.append(pl.BlockSpec(memory_space=pl.ANY))
        aliases = {len(args): 0}
        args.append(earlier)
    return pl.pallas_call(
        functools.partial(_moe_kernel, cap, alpha),
        grid=(nb, n_e),
        in_specs=in_specs,
        out_specs=pl.BlockSpec((1, l, d), lambda i, e: (i + b0, 0, 0)),
        out_shape=jax.ShapeDtypeStruct((b, l, d), F32),
        input_output_aliases=aliases,
        compiler_params=_params(("parallel", "arbitrary")),
        name="moe_experts",
    )(*args)


def _rope_tables(l):
    n_axis = RET_DK // 4
    freqs = ROPE_BASE ** (-np.arange(n_axis, dtype=np.float64) / n_axis)
    pos = np.arange(l)
    ang = np.concatenate([(pos // GRID_W)[:, None] * freqs, (pos % GRID_W)[:, None] * freqs], axis=-1)
    cos, sin = np.cos(ang), np.sin(ang)
    return (np.concatenate([cos, cos], axis=-1).astype(np.float32),
            np.concatenate([-sin, sin], axis=-1).astype(np.float32))


def kernel(x, c, ctx, c_ctx, w_ada, b_ada, w_in, conv_w, conv_b, conv_ln_g, conv_ln_b, w_conv_out,
           log_decay_f, log_decay_b, ret_gn_g, w_ret_out, w_out, ln1_g, ln1_b,
           w_router, w_gate, w_up, w_down, ln2_g, ln2_b):
    depth = w_ada.shape[0]
    assert depth == 1, "single trunk layer"
    b, l, d = x.shape
    alpha = (2.0 * depth) ** 0.25
    cap = EC_FACTOR * l // N_EXPERTS
    u_end = 2 * CONV_DIM
    q_end = u_end + QK_DIM
    k_end = q_end + QK_DIM
    v_end = k_end + V_DIM
    row = lambda a: a.reshape(1, -1)

    n_mod = b + 1
    pad = (-n_mod) % 8
    cc = jnp.concatenate([c, c_ctx[None], jnp.zeros((pad, d), F32)], axis=0)
    mod = _ada(cc, w_ada[0], row(b_ada[0]))
    sh1, sc1, g1, sh2, sc2, g2 = [m.reshape(b, 1, d) for m in jnp.split(mod[:b], 6, axis=-1)]
    csh1, csc1 = mod[b:b + 1, :d], mod[b:b + 1, d:2 * d]

    perm = np.concatenate([np.arange(0, RET_DK, 2), np.arange(1, RET_DK, 2)])
    perm = (np.arange(RET_HEADS)[:, None] * RET_DK + perm[None, :]).reshape(-1)
    w = w_in[0]
    w_kvu = jnp.concatenate([w[:, q_end:k_end][:, perm], w[:, k_end:v_end], w[:, :u_end]], axis=1).astype(BF16)
    wq = w[:, u_end:q_end][:, perm].astype(BF16)
    wg = w[:, v_end:].astype(BF16)

    cos_t, sin_t = _rope_tables(l)
    cq, sq = jnp.asarray(cos_t), jnp.asarray(sin_t)
    k_scale = RET_DK ** -0.5
    ck, sk = jnp.asarray(cos_t * k_scale), jnp.asarray(sin_t * k_scale)

    lgf, lgb = log_decay_f[0], log_decay_b[0]
    sf0, sb0 = _ctx_states(lgf, lgb, ctx, csh1, csc1, w_kvu)
    cw8 = jnp.repeat(conv_w[0], 8, axis=0)
    h, k, v, ya, sf = _proj(lgf, x, sh1, sc1, w_kvu, ck, sk, sf0,
                            cw8, row(conv_b[0]), row(conv_ln_g[0]), row(conv_ln_b[0]))
    x1, hp, lt = _mixer(alpha, lgf, lgb, x, h, k, v, sf, sb0, ya, wq, wg,
                        w_conv_out[0].astype(BF16), w_ret_out[0].astype(BF16), w_out[0].astype(BF16),
                        row(ret_gn_g[0]), g1, sh2, sc2, row(ln1_g[0]), row(ln1_b[0]), w_router[0].T, cq, sq)
    rank, aff, slots = _route(lt, cap)
    part_rows = b * l
    hp_rows = hp.reshape(PACK_PARTS * part_rows, hp.shape[-1])
    wge, wue, wde = w_gate[0].astype(BF16), w_up[0].astype(BF16), w_down[0].astype(BF16)
    groups = MOE_BATCH_GROUPS if b % MOE_BATCH_GROUPS == 0 else 1
    nb = b // groups
    out = None
    for gi in range(groups):
        b0 = gi * nb
        ids = jnp.concatenate([slots[b0:b0 + nb].reshape(-1) + p * part_rows for p in range(PACK_PARTS)])
        xe = _sc_gather(hp_rows, ids).reshape(PACK_PARTS, nb, N_EXPERTS, cap, hp.shape[-1])
        out = _moe(alpha, rank, aff, xe, wge, wue, wde, x1, g2, row(ln2_g[0]), row(ln2_b[0]), cap, b0, out)
    return out
```

```python
import functools

import numpy as np
import jax
import jax.numpy as jnp
from jax import lax
from jax.experimental import pallas as pl
from jax.experimental.pallas import tpu as pltpu
from jax.experimental.pallas import tpu_sc as plsc

F32 = jnp.float32
BF16 = jnp.bfloat16
U32 = jnp.uint32

GRID_W = 64
CONV_DIM = 512
CONV_WIDTH = 31
CONV_HALO = CONV_WIDTH // 2
RET_HEADS = 4
RET_DK = 128
RET_DV = 256
RET_CHUNK = 128
ROPE_BASE = 10000.0
QK_DIM = RET_HEADS * RET_DK
V_DIM = RET_HEADS * RET_DV
N_EXPERTS = 16
EC_FACTOR = 2
LN_EPS = 1e-5

ADA_COLS = 768
PROJ_ROWS = 512
MIX_ROWS = 512
BATCH_GROUPS = 2
CONV_PAD = 16
CONV_BLOCK = 32
OUT_ROWS = 512
PACK_PARTS = 2
SC_WINDOW = 128
SLOT_SHIFT = 4
SLOT_RADIX = 1 << SLOT_SHIFT
TOKEN_SHIFT = 6
V7X_VMEM_LIMIT = 56 * 1024 * 1024


def _dot(a, b):
    return jnp.dot(a, b, preferred_element_type=F32)


def _dot_nt(a, b):
    return lax.dot_general(a, b, (((1,), (1,)), ((), ())), preferred_element_type=F32)


def _dot_tn(a, b):
    return lax.dot_general(a, b, (((0,), (0,)), ((), ())), preferred_element_type=F32)


def _split(a):
    hi = a.astype(BF16)
    lo = (a - hi.astype(F32)).astype(BF16)
    return hi, lo


def _norm(x):
    mu = jnp.mean(x, axis=-1, keepdims=True)
    xc = x - mu
    var = jnp.mean(xc * xc, axis=-1, keepdims=True)
    return xc * lax.rsqrt(var + LN_EPS)


def _sigmoid(x):
    return 1.0 / (1.0 + jnp.exp(-x))


def _silu(x):
    return x * _sigmoid(x)


def _col_iota(n):
    return lax.broadcasted_iota(jnp.int32, (n, 1), 0).astype(F32)


def _pack_pair(lo, hi):
    lo_bits = pltpu.bitcast(lo.astype(F32), U32)
    hi_bits = pltpu.bitcast(hi.astype(F32), U32)
    return lax.shift_right_logical(lo_bits, jnp.uint32(16)) | hi_bits


def _unpack_pair(word):
    lo = pltpu.bitcast(lax.shift_left(word, jnp.uint32(16)), F32)
    hi = pltpu.bitcast(word & jnp.uint32(0xFFFF0000), F32)
    return lo.astype(BF16), hi.astype(BF16)


def _smem():
    return pl.BlockSpec(memory_space=pltpu.SMEM)


def _params(sem, vmem=V7X_VMEM_LIMIT):
    return pltpu.CompilerParams(dimension_semantics=sem, vmem_limit_bytes=vmem)


def _ada_kernel(c_ref, w_ref, b_ref, o_ref):
    a_hi, a_lo = _split(_silu(c_ref[...]))
    w_hi, w_lo = _split(w_ref[...])
    o_ref[...] = _dot(a_hi, w_hi) + (_dot(a_hi, w_lo) + _dot(a_lo, w_hi)) + b_ref[...]


def _ada(cc, w, b):
    m, d = cc.shape
    n = w.shape[1]
    return pl.pallas_call(
        _ada_kernel,
        grid=(n // ADA_COLS,),
        in_specs=[pl.BlockSpec((m, d), lambda j: (0, 0)),
                  pl.BlockSpec((d, ADA_COLS), lambda j: (0, j)),
                  pl.BlockSpec((1, ADA_COLS), lambda j: (0, j))],
        out_specs=pl.BlockSpec((m, ADA_COLS), lambda j: (0, j)),
        out_shape=jax.ShapeDtypeStruct((m, n), F32),
        compiler_params=_params(("parallel",)),
        name="ada_proj",
    )(cc, w, b)


def _ctx_kernel(lgf_ref, lgb_ref, ctx_ref, sh_ref, sc_ref, w_ref, sf_ref, sb_ref):
    x = ctx_ref[0]
    lc = x.shape[0]
    h = (_norm(x) * (1.0 + sc_ref[...]) + sh_ref[...]).astype(BF16)
    kv = _dot(h, w_ref[...])
    t = _col_iota(lc)
    for hh in range(RET_HEADS):
        k = kv[:, hh * RET_DK:(hh + 1) * RET_DK] * (RET_DK ** -0.5)
        v = kv[:, QK_DIM + hh * RET_DV:QK_DIM + (hh + 1) * RET_DV].astype(BF16)
        wf = jnp.exp(lgf_ref[hh] * (lc - 1.0 - t))
        wb = jnp.exp(lgb_ref[hh] * t)
        sf_ref[0, hh] = _dot_tn((k * wf).astype(BF16), v)
        sb_ref[0, hh] = _dot_tn((k * wb).astype(BF16), v)


def _ctx_states(lgf, lgb, ctx, csh, csc, w_kvu):
    b, lc, d = ctx.shape
    kvw = QK_DIM + V_DIM
    st = jax.ShapeDtypeStruct((b, RET_HEADS, RET_DK, RET_DV), F32)
    st_spec = pl.BlockSpec((1, RET_HEADS, RET_DK, RET_DV), lambda i: (i, 0, 0, 0))
    return pl.pallas_call(
        _ctx_kernel,
        grid=(b,),
        in_specs=[_smem(), _smem(),
                  pl.BlockSpec((1, lc, d), lambda i: (i, 0, 0)),
                  pl.BlockSpec((1, d), lambda i: (0, 0)),
                  pl.BlockSpec((1, d), lambda i: (0, 0)),
                  pl.BlockSpec((d, kvw), lambda i: (0, 0))],
        out_specs=(st_spec, st_spec),
        out_shape=(st, st),
        compiler_params=_params(("parallel",)),
        name="ctx_states",
    )(lgf, lgb, ctx, csh, csc, w_kvu)


def _proj_kernel(lgf_ref, xp_ref, x_ref, xn_ref, sh_ref, sc_ref, w_ref, ck_ref, sk_ref, sf0_ref,
                 cw_ref, cb_ref, clg_ref, clb_ref,
                 h_ref, k_ref, v_ref, ya_ref, sf_ref, s_scr, y_scr, yres_scr, z_scr):
    t = pl.program_id(1)
    nt = pl.num_programs(1)

    @pl.when(t == 0)
    def _():
        s_scr[...] = sf0_ref[0]

    rows = x_ref.shape[1]
    ext = rows + 2 * CONV_PAD
    x_ext = jnp.concatenate([xp_ref[0], x_ref[0], xn_ref[0]], axis=0)
    h_ext = (_norm(x_ext) * (1.0 + sc_ref[0]) + sh_ref[0]).astype(BF16)
    h = h_ext[CONV_PAD:CONV_PAD + rows]
    h_ref[0] = h
    kvw = QK_DIM + V_DIM
    u = _dot(h_ext, w_ref[:, kvw:])
    row = lax.broadcasted_iota(jnp.int32, (ext, 1), 0)
    head_ok = jnp.where(t == 0, 0.0, 1.0)
    tail_ok = jnp.where(t == nt - 1, 0.0, 1.0)
    inside = jnp.where(row < CONV_PAD, head_ok, jnp.where(row >= CONV_PAD + rows, tail_ok, 1.0))
    y_scr[...] = u[:, :CONV_DIM] * _sigmoid(u[:, CONV_DIM:]) * inside

    kk = _dot(h, w_ref[:, :QK_DIM])
    vv = _dot(h, w_ref[:, QK_DIM:kvw]).astype(BF16)
    v_ref[0] = vv
    ck = ck_ref[...]
    sk = sk_ref[...]
    pos = _col_iota(RET_CHUNK)
    for hh in range(RET_HEADS):
        lg = lgf_ref[hh]
        kh = kk[:, hh * RET_DK:(hh + 1) * RET_DK]
        kr = kh * ck + pltpu.roll(kh, RET_DK // 2, 1) * sk
        k_ref[0, :, hh * RET_DK:(hh + 1) * RET_DK] = kr.astype(BF16)
        state_dec = jnp.exp(lg * (RET_CHUNK - 1.0 - pos))
        chunk_dec = jnp.exp(lg * jnp.full((1, RET_DV), float(RET_CHUNK), F32))
        for c in range(rows // RET_CHUNK):
            r0 = c * RET_CHUNK
            s = s_scr[hh]
            sf_ref[0, c, hh] = s.astype(BF16)
            kc = (kr[r0:r0 + RET_CHUNK] * state_dec).astype(BF16)
            vc = vv[r0:r0 + RET_CHUNK, hh * RET_DV:(hh + 1) * RET_DV]
            s_scr[hh] = s * chunk_dec + _dot_tn(kc, vc)

    span = ext - 8
    for r in range(8):
        yres_scr[r] = y_scr[pl.ds(r, span), :]

    for i in range(rows // CONV_BLOCK):
        base = i * CONV_BLOCK
        acc = None
        for w in range(CONV_WIDTH):
            a, r = divmod(w + 1, 8)
            wk = jnp.concatenate([cw_ref[8 * w:8 * w + 8, :]] * (CONV_BLOCK // 8), axis=0)
            tap = yres_scr[r, base + 8 * a:base + 8 * a + CONV_BLOCK, :] * wk
            acc = tap if acc is None else acc + tap
        z_scr[base:base + CONV_BLOCK, :] = acc
    z = _norm(z_scr[...] + cb_ref[...]) * clg_ref[...] + clb_ref[...]
    ya_ref[0] = _silu(z).astype(BF16)


def _proj(lgf, x, b0, sh1, sc1, w_kvu, ck, sk, sf0, cw, cb, clg, clb):
    _, l, d = x.shape
    b = sh1.shape[0]
    nt = l // PROJ_ROWS
    cpt = PROJ_ROWS // RET_CHUNK
    hpt = PROJ_ROWS // CONV_PAD
    n_halo = l // CONV_PAD
    ext = PROJ_ROWS + 2 * CONV_PAD
    tile = lambda w: pl.BlockSpec((1, PROJ_ROWS, w), lambda i, t: (i, t, 0))
    vec = pl.BlockSpec((1, 1, d), lambda i, t: (i, 0, 0))
    full = lambda a: pl.BlockSpec(a.shape, lambda i, t: (0,) * a.ndim, pipeline_mode=pl.Buffered(1))
    rope = pl.BlockSpec((PROJ_ROWS, RET_DK), lambda i, t: (t, 0))
    x_tile = pl.BlockSpec((1, PROJ_ROWS, d), lambda i, t: (i + b0, t, 0))
    prev = pl.BlockSpec((1, CONV_PAD, d), lambda i, t: (i + b0, jnp.maximum(t * hpt - 1, 0), 0))
    nxt = pl.BlockSpec((1, CONV_PAD, d), lambda i, t: (i + b0, jnp.minimum((t + 1) * hpt, n_halo - 1), 0))
    return pl.pallas_call(
        _proj_kernel,
        grid=(b, nt),
        in_specs=[_smem(), prev, x_tile, nxt, vec, vec, full(w_kvu), rope, rope,
                  pl.BlockSpec((1, RET_HEADS, RET_DK, RET_DV), lambda i, t: (i, 0, 0, 0)),
                  full(cw), full(cb), full(clg), full(clb)],
        out_specs=(tile(d), tile(QK_DIM), tile(V_DIM), tile(CONV_DIM),
                   pl.BlockSpec((1, cpt, RET_HEADS, RET_DK, RET_DV), lambda i, t: (i, t, 0, 0, 0))),
        out_shape=(jax.ShapeDtypeStruct((b, l, d), BF16),
                   jax.ShapeDtypeStruct((b, l, QK_DIM), BF16),
                   jax.ShapeDtypeStruct((b, l, V_DIM), BF16),
                   jax.ShapeDtypeStruct((b, l, CONV_DIM), BF16),
                   jax.ShapeDtypeStruct((b, l // RET_CHUNK, RET_HEADS, RET_DK, RET_DV), BF16)),
        scratch_shapes=[pltpu.VMEM((RET_HEADS, RET_DK, RET_DV), F32),
                        pltpu.VMEM((ext, CONV_DIM), F32),
                        pltpu.VMEM((8, ext - 8, CONV_DIM), F32),
                        pltpu.VMEM((PROJ_ROWS, CONV_DIM), F32)],
        compiler_params=_params(("parallel", "arbitrary")),
        name="proj_kv_conv",
    )(lgf, x, x, x, sh1, sc1, w_kvu, ck, sk, sf0, cw, cb, clg, clb)


def _mixer_kernel(alpha, lgf_ref, lgb_ref, x_ref, h_ref, k_ref, v_ref, sf_ref, sb0_ref, ya_ref,
                  wq_ref, wg_ref, wco_ref, wro_ref, wo_ref, gng_ref,
                  g1_ref, sh2_ref, sc2_ref, l1g_ref, l1b_ref, wr_ref, cq_ref, sq_ref,
                  x1_ref, hp_ref, lt_ref, sb_scr):
    t = pl.program_id(1)

    @pl.when(t == 0)
    def _():
        sb_scr[...] = sb0_ref[0]

    h = h_ref[0]
    rows = h.shape[0]
    cq = cq_ref[...]
    sq = sq_ref[...]
    q_all = _dot(h, wq_ref[...])
    pos = _col_iota(RET_CHUNK)
    di = lax.broadcasted_iota(jnp.int32, (RET_CHUNK, RET_CHUNK), 0)
    dj = lax.broadcasted_iota(jnp.int32, (RET_CHUNK, RET_CHUNK), 1)
    dist = (di - dj).astype(F32)

    n_chunks = rows // RET_CHUNK
    o_parts = [[None] * RET_HEADS for _ in range(n_chunks)]
    for hh in range(RET_HEADS):
        lgf = lgf_ref[hh]
        lgb = lgb_ref[hh]
        qh = q_all[:, hh * RET_DK:(hh + 1) * RET_DK]
        qr = qh * cq + pltpu.roll(qh, RET_DK // 2, 1) * sq
        decay = jnp.where(dist >= 0.0, jnp.exp(lgf * jnp.maximum(dist, 0.0)),
                          jnp.exp(lgb * jnp.maximum(-dist, 0.0)))
        cross_f = jnp.exp(lgf * (pos + 1.0))
        cross_b = jnp.exp(lgb * (RET_CHUNK - pos))
        state_b = jnp.exp(lgb * pos)
        chunk_b = jnp.exp(lgb * jnp.full((1, RET_DV), float(RET_CHUNK), F32))
        for c in reversed(range(n_chunks)):
            r0 = c * RET_CHUNK
            qc = qr[r0:r0 + RET_CHUNK]
            kc = k_ref[0, r0:r0 + RET_CHUNK, hh * RET_DK:(hh + 1) * RET_DK]
            vc = v_ref[0, r0:r0 + RET_CHUNK, hh * RET_DV:(hh + 1) * RET_DV]
            sb = sb_scr[hh]
            att = _dot_nt(qc.astype(BF16), kc) * decay
            lhs = jnp.concatenate([att.astype(BF16), (qc * cross_f).astype(BF16), (qc * cross_b).astype(BF16)], axis=1)
            rhs = jnp.concatenate([vc, sf_ref[0, c, hh], sb.astype(BF16)], axis=0)
            o = _dot(lhs, rhs)
            kb = (kc.astype(F32) * state_b).astype(BF16)
            sb_scr[hh] = sb * chunk_b + _dot_tn(kb, vc)
            o_parts[c][hh] = _norm(o)
    on = jnp.concatenate([jnp.concatenate(o_parts[c], axis=1) for c in range(n_chunks)], axis=0)

    d = x_ref.shape[-1]
    pw = d // (2 * PACK_PARTS)
    wr_hi, wr_lo = _split(wr_ref[...])
    g_ret = _dot(h, wg_ref[:, :V_DIM])
    yb_in = (_silu(g_ret) * (on * gng_ref[...])).astype(BF16)
    y_b = _dot(yb_in, wro_ref[...])
    y_a = _dot(ya_ref[0], wco_ref[...])
    g_a = _dot(h, wg_ref[:, V_DIM:V_DIM + d])
    g_b = _dot(h, wg_ref[:, V_DIM + d:])
    y = (_sigmoid(g_a) * y_a + _sigmoid(g_b) * y_b).astype(BF16)
    ym = _dot(y, wo_ref[...])

    x1 = _norm(alpha * x_ref[0] + g1_ref[0] * ym) * l1g_ref[...] + l1b_ref[...]
    x1_ref[0] = x1
    h2 = _norm(x1) * (1.0 + sc2_ref[0]) + sh2_ref[0]
    h2_hi, h2_lo = _split(h2)
    for p in range(PACK_PARTS):
        c0 = 2 * p * pw
        hp_ref[p, 0] = _pack_pair(h2_hi[:, c0:c0 + pw], h2_hi[:, c0 + pw:c0 + 2 * pw])
    lt_ref[0] = _dot_nt(wr_hi, h2_hi) + (_dot_nt(wr_hi, h2_lo) + _dot_nt(wr_lo, h2_hi))


def _mixer(alpha, lgf, lgb, x, b0, h, k, v, sf, sb0, ya, wq, wg, wco, wro, wo, gng,
           g1, sh2, sc2, l1g, l1b, wr_t, cq, sq):
    b, l, d = h.shape
    nt = l // MIX_ROWS
    cpt = MIX_ROWS // RET_CHUNK
    pw = d // (2 * PACK_PARTS)
    rev = lambda w: pl.BlockSpec((1, MIX_ROWS, w), lambda i, t: (i, nt - 1 - t, 0))
    vec = pl.BlockSpec((1, 1, d), lambda i, t: (i, 0, 0))
    full = lambda a: pl.BlockSpec(a.shape, lambda i, t: (0,) * a.ndim, pipeline_mode=pl.Buffered(1))
    rope = pl.BlockSpec((MIX_ROWS, RET_DK), lambda i, t: (nt - 1 - t, 0))
    state = pl.BlockSpec((1, RET_HEADS, RET_DK, RET_DV), lambda i, t: (i, 0, 0, 0))
    return pl.pallas_call(
        functools.partial(_mixer_kernel, alpha),
        grid=(b, nt),
        in_specs=[_smem(), _smem(),
                  pl.BlockSpec((1, MIX_ROWS, d), lambda i, t: (i + b0, nt - 1 - t, 0)),
                  rev(d), rev(QK_DIM), rev(V_DIM),
                  pl.BlockSpec((1, cpt, RET_HEADS, RET_DK, RET_DV), lambda i, t: (i, nt - 1 - t, 0, 0, 0)),
                  state, rev(CONV_DIM),
                  full(wq), full(wg), full(wco), full(wro), full(wo), full(gng),
                  vec, vec, vec, full(l1g), full(l1b), full(wr_t), rope, rope],
        out_specs=(rev(d),
                   pl.BlockSpec((PACK_PARTS, 1, MIX_ROWS, pw), lambda i, t: (0, i, nt - 1 - t, 0)),
                   pl.BlockSpec((1, N_EXPERTS, MIX_ROWS), lambda i, t: (i, 0, nt - 1 - t))),
        out_shape=(jax.ShapeDtypeStruct((b, l, d), F32),
                   jax.ShapeDtypeStruct((PACK_PARTS, b, l, pw), U32),
                   jax.ShapeDtypeStruct((b, N_EXPERTS, l), F32)),
        scratch_shapes=[pltpu.VMEM((RET_HEADS, RET_DK, RET_DV), F32)],
        compiler_params=_params(("parallel", "arbitrary")),
        name="mixer",
    )(lgf, lgb, x, h, k, v, sf, sb0, ya, wq, wg, wco, wro, wo, gng,
      g1, sh2, sc2, l1g, l1b, wr_t, cq, sq)


def _route_kernel(cap, lt_ref, rank_ref, aff_ref, idx_ref):
    logits = lt_ref[0]
    n_e, l = logits.shape
    m = jnp.max(logits, axis=0, keepdims=True)
    p = jnp.exp(logits - m)
    aff = p / jnp.sum(p, axis=0, keepdims=True)
    aff_ref[0] = aff
    capf = float(cap)

    def count(ones):
        return jnp.sum(ones, axis=1, keepdims=True)

    def refine(lo, shift, width, keep):
        best = lo
        for digit in range(1, 1 << width):
            cand = lo | jnp.left_shift(jnp.int32(digit), shift)
            best = jnp.where(keep(cand), cand, best)
        return best

    def bisect(n_bits, keep):
        lo = jnp.zeros((n_e, 1), jnp.int32)
        if n_bits % 2:
            lo = refine(lo, n_bits - 1, 1, keep)
        return lax.fori_loop(0, n_bits // 2, lambda i, v: refine(v, 2 * (n_bits // 2 - 1 - i), 2, keep), lo)

    thr = pltpu.bitcast(
        bisect(31, lambda cand: count(jnp.where(aff >= pltpu.bitcast(cand, F32), 1.0, 0.0)) >= capf), F32)
    gt = jnp.where(aff > thr, 1.0, 0.0)
    eq = jnp.where(aff == thr, 1.0, 0.0)
    need = capf - count(gt)
    idx = lax.broadcasted_iota(jnp.int32, (n_e, l), 1)
    idx_bits = int(l - 1).bit_length()

    last = bisect(idx_bits, lambda cand: count(jnp.where(idx < cand, eq, 0.0)) < need)
    bound = jnp.where(need > 0.0, last + 1, 0)
    sel = gt + jnp.where(idx < bound, eq, 0.0)

    blk = 128
    ti = lax.broadcasted_iota(jnp.int32, (blk, blk), 0)
    tj = lax.broadcasted_iota(jnp.int32, (blk, blk), 1)
    tri = jnp.where(ti < tj, 1.0, 0.0).astype(BF16)
    offset = jnp.zeros((n_e, 1), F32)
    for j in range(l // blk):
        sj = sel[:, j * blk:(j + 1) * blk]
        before = _dot(sj.astype(BF16), tri) + offset
        rank_ref[0, :, j * blk:(j + 1) * blk] = jnp.where(sj > 0.0, before, -1.0).astype(jnp.int32)
        offset = offset + jnp.sum(sj, axis=1, keepdims=True)

    rank = rank_ref[0]
    hi_digit = jnp.right_shift(rank, SLOT_SHIFT)
    lo_digit = jnp.bitwise_and(rank, SLOT_RADIX - 1)
    digit = lax.broadcasted_iota(jnp.int32, (SLOT_RADIX, l), 0)
    tok_hi = jnp.right_shift(idx[:1], TOKEN_SHIFT).astype(F32)
    tok_lo = jnp.bitwise_and(idx[:1], (1 << TOKEN_SHIFT) - 1).astype(F32)
    a_hi, a_lo, b_rows = [], [], []
    for e in range(n_e):
        is_a = hi_digit[e:e + 1] == digit
        a_hi.append(jnp.where(is_a, tok_hi, 0.0))
        a_lo.append(jnp.where(is_a, tok_lo, 0.0))
        b_rows.append(jnp.where(lo_digit[e:e + 1] == digit, 1.0, 0.0))
    b_all = jnp.concatenate(b_rows, axis=0).astype(BF16)
    cross_hi = _dot_nt(jnp.concatenate(a_hi, axis=0).astype(BF16), b_all)
    cross_lo = _dot_nt(jnp.concatenate(a_lo, axis=0).astype(BF16), b_all)
    cross = cross_hi * float(1 << TOKEN_SHIFT) + cross_lo + (pl.program_id(0) * l).astype(F32)
    for e in range(n_e):
        s0 = e * SLOT_RADIX
        idx_ref[0, e] = cross[s0:s0 + SLOT_RADIX, s0:s0 + SLOT_RADIX].astype(jnp.int32)


def _route(lt, cap):
    b, n_e, l = lt.shape
    assert cap == SLOT_RADIX * SLOT_RADIX and l <= (1 << TOKEN_SHIFT) * 256
    spec = pl.BlockSpec((1, n_e, l), lambda i: (i, 0, 0))
    return pl.pallas_call(
        functools.partial(_route_kernel, cap),
        grid=(b,),
        in_specs=[spec],
        out_specs=(spec, spec, pl.BlockSpec((1, n_e, SLOT_RADIX, SLOT_RADIX), lambda i: (i, 0, 0, 0))),
        out_shape=(jax.ShapeDtypeStruct((b, n_e, l), jnp.int32),
                   jax.ShapeDtypeStruct((b, n_e, l), F32),
                   jax.ShapeDtypeStruct((b, n_e, SLOT_RADIX, SLOT_RADIX), jnp.int32)),
        compiler_params=_params(("parallel",)),
        name="route_topc",
    )(lt)


def _sc_gather(rows, idx):
    n = idx.shape[0]
    w = rows.shape[1]
    mesh = plsc.VectorSubcoreMesh(core_axis_name="c", subcore_axis_name="s")

    @pl.kernel(out_type=jax.ShapeDtypeStruct((n, w), rows.dtype), mesh=mesh, scratch_types=[])
    def gather(rows_hbm, idx_hbm, out_hbm):
        def window(idx_vmem, out_vmem):
            pltpu.sync_copy(rows_hbm.at[idx_vmem.at[0]], out_vmem)

        pltpu.emit_pipeline(
            window,
            grid=(n // SC_WINDOW,),
            in_specs=[pl.BlockSpec((1, SC_WINDOW), index_map=lambda i: (0, i))],
            out_specs=[pl.BlockSpec((SC_WINDOW, w), index_map=lambda i: (i, 0))],
            core_axis_name=("c", "s"),
            dimension_semantics=(pltpu.PARALLEL,),
        )(idx_hbm, out_hbm)

    return gather(rows, idx.reshape(1, n))


def _moe_kernel(cap, alpha, rank_ref, aff_ref, xe_ref, wg_ref, wu_ref, wd_ref,
                x1_ref, g2_ref, lng_ref, lnb_ref, *rest):
    o_ref = rest[-1]
    e = pl.program_id(1)
    l = o_ref.shape[1]
    n_tiles = l // OUT_ROWS

    @pl.when(e == 0)
    def _():
        o_ref[0] = jnp.zeros(o_ref.shape[1:], F32)

    rank = rank_ref[0, 0]
    slot = lax.broadcasted_iota(jnp.int32, (cap, l), 0)
    hit = slot == rank
    pick = jnp.where(hit, 1.0, 0.0).astype(BF16)
    gate = jnp.sum(jnp.where(hit, aff_ref[0, 0], 0.0), axis=1, keepdims=True)
    halves = []
    for p in range(PACK_PARTS):
        halves.extend(_unpack_pair(xe_ref[p, 0, 0]))
    xe = jnp.concatenate(halves, axis=1)
    he = (_silu(_dot(xe, wg_ref[0])) * _dot(xe, wu_ref[0])).astype(BF16)
    ye = (_dot(he, wd_ref[0]) * gate).astype(BF16)
    for i in range(n_tiles):
        r0 = i * OUT_ROWS
        o_ref[0, r0:r0 + OUT_ROWS, :] += _dot_tn(pick[:, r0:r0 + OUT_ROWS], ye)

    @pl.when(e == pl.num_programs(1) - 1)
    def _():
        def tile(i, carry):
            r0 = pl.multiple_of(i * OUT_ROWS, OUT_ROWS)
            rows = pl.ds(r0, OUT_ROWS)
            z = alpha * x1_ref[0, rows, :] + g2_ref[0] * o_ref[0, rows, :]
            o_ref[0, rows, :] = _norm(z) * lng_ref[...] + lnb_ref[...]
            return carry

        lax.fori_loop(0, n_tiles, tile, 0)


def _moe(alpha, rank, aff, xe, wgate, wup, wdown, x1, g2, ln_g, ln_b, cap, b_total, b0, earlier):
    b, l, d = x1.shape
    n_e = wgate.shape[0]
    f = wgate.shape[2]
    row = pl.BlockSpec((1, 1, 1, l), lambda i, e: (i, e, 0, 0))
    vec = pl.BlockSpec((1, d), lambda i, e: (0, 0))
    in_specs = [row, row,
                pl.BlockSpec((PACK_PARTS, 1, 1, cap, xe.shape[-1]), lambda i, e: (0, i, e, 0, 0)),
                pl.BlockSpec((1, d, f), lambda i, e: (e, 0, 0)),
                pl.BlockSpec((1, d, f), lambda i, e: (e, 0, 0)),
                pl.BlockSpec((1, f, d), lambda i, e: (e, 0, 0)),
                pl.BlockSpec((1, l, d), lambda i, e: (i, 0, 0)),
                pl.BlockSpec((1, 1, d), lambda i, e: (i, 0, 0)), vec, vec]
    args = [rank.reshape(b, n_e, 1, l), aff.reshape(b, n_e, 1, l), xe, wgate, wup, wdown, x1, g2, ln_g, ln_b]
    aliases = {}
    if earlier is not None:
        in_specs.append(pl.BlockSpec(memory_space=pl.ANY))
        aliases = {len(args): 0}
        args.append(earlier)
    return pl.pallas_call(
        functools.partial(_moe_kernel, cap, alpha),
        grid=(b, n_e),
        in_specs=in_specs,
        out_specs=pl.BlockSpec((1, l, d), lambda i, e: (i + b0, 0, 0)),
        out_shape=jax.ShapeDtypeStruct((b_total, l, d), F32),
        input_output_aliases=aliases,
        compiler_params=_params(("parallel", "arbitrary")),
        name="moe_experts",
    )(*args)


def _rope_tables(l):
    n_axis = RET_DK // 4
    freqs = ROPE_BASE ** (-np.arange(n_axis, dtype=np.float64) / n_axis)
    pos = np.arange(l)
    ang = np.concatenate([(pos // GRID_W)[:, None] * freqs, (pos % GRID_W)[:, None] * freqs], axis=-1)
    cos, sin = np.cos(ang), np.sin(ang)
    return (np.concatenate([cos, cos], axis=-1).astype(np.float32),
            np.concatenate([-sin, sin], axis=-1).astype(np.float32))


def kernel(x, c, ctx, c_ctx, w_ada, b_ada, w_in, conv_w, conv_b, conv_ln_g, conv_ln_b, w_conv_out,
           log_decay_f, log_decay_b, ret_gn_g, w_ret_out, w_out, ln1_g, ln1_b,
           w_router, w_gate, w_up, w_down, ln2_g, ln2_b):
    depth = w_ada.shape[0]
    assert depth == 1, "single trunk layer"
    b, l, d = x.shape
    alpha = (2.0 * depth) ** 0.25
    cap = EC_FACTOR * l // N_EXPERTS
    u_end = 2 * CONV_DIM
    q_end = u_end + QK_DIM
    k_end = q_end + QK_DIM
    v_end = k_end + V_DIM
    row = lambda a: a.reshape(1, -1)

    n_mod = b + 1
    pad = (-n_mod) % 8
    cc = jnp.concatenate([c, c_ctx[None], jnp.zeros((pad, d), F32)], axis=0)
    mod = _ada(cc, w_ada[0], row(b_ada[0]))
    sh1, sc1, g1, sh2, sc2, g2 = [m.reshape(b, 1, d) for m in jnp.split(mod[:b], 6, axis=-1)]
    csh1, csc1 = mod[b:b + 1, :d], mod[b:b + 1, d:2 * d]

    perm = np.concatenate([np.arange(0, RET_DK, 2), np.arange(1, RET_DK, 2)])
    perm = (np.arange(RET_HEADS)[:, None] * RET_DK + perm[None, :]).reshape(-1)
    w = w_in[0]
    w_kvu = jnp.concatenate([w[:, q_end:k_end][:, perm], w[:, k_end:v_end], w[:, :u_end]], axis=1).astype(BF16)
    wq = w[:, u_end:q_end][:, perm].astype(BF16)
    wg = w[:, v_end:].astype(BF16)

    cos_t, sin_t = _rope_tables(l)
    cq, sq = jnp.asarray(cos_t), jnp.asarray(sin_t)
    k_scale = RET_DK ** -0.5
    ck, sk = jnp.asarray(cos_t * k_scale), jnp.asarray(sin_t * k_scale)

    lgf, lgb = log_decay_f[0], log_decay_b[0]
    sf0, sb0 = _ctx_states(lgf, lgb, ctx, csh1, csc1, w_kvu)
    cw8 = jnp.repeat(conv_w[0], 8, axis=0)
    wco, wro, wo = w_conv_out[0].astype(BF16), w_ret_out[0].astype(BF16), w_out[0].astype(BF16)
    wge, wue, wde = w_gate[0].astype(BF16), w_up[0].astype(BF16), w_down[0].astype(BF16)
    wr_t = w_router[0].T
    groups = BATCH_GROUPS if b % BATCH_GROUPS == 0 else 1
    nb = b // groups
    out = None
    for gi in range(groups):
        b0 = gi * nb
        part = lambda a: a[b0:b0 + nb]
        h, k, v, ya, sf = _proj(lgf, x, b0, part(sh1), part(sc1), w_kvu, ck, sk, part(sf0),
                                cw8, row(conv_b[0]), row(conv_ln_g[0]), row(conv_ln_b[0]))
        x1, hp, lt = _mixer(alpha, lgf, lgb, x, b0, h, k, v, sf, part(sb0), ya, wq, wg, wco, wro, wo,
                            row(ret_gn_g[0]), part(g1), part(sh2), part(sc2), row(ln1_g[0]), row(ln1_b[0]),
                            wr_t, cq, sq)
        rank, aff, slots = _route(lt, cap)
        part_rows = nb * l
        ids = jnp.concatenate([slots.reshape(-1) + p * part_rows for p in range(PACK_PARTS)])
        xe = _sc_gather(hp.reshape(PACK_PARTS * part_rows, hp.shape[-1]), ids)
        xe = xe.reshape(PACK_PARTS, nb, N_EXPERTS, cap, hp.shape[-1])
        out = _moe(alpha, rank, aff, xe, wge, wue, wde, x1, part(g2), row(ln2_g[0]), row(ln2_b[0]),
                   cap, b, b0, out)
    return out
```

```python
import functools

import numpy as np
import jax
import jax.numpy as jnp
from jax import lax
from jax.experimental import pallas as pl
from jax.experimental.pallas import tpu as pltpu
from jax.experimental.pallas import tpu_sc as plsc

F32 = jnp.float32
BF16 = jnp.bfloat16
U32 = jnp.uint32

GRID_W = 64
CONV_DIM = 512
CONV_WIDTH = 31
CONV_HALO = CONV_WIDTH // 2
RET_HEADS = 4
RET_DK = 128
RET_DV = 256
RET_CHUNK = 128
ROPE_BASE = 10000.0
QK_DIM = RET_HEADS * RET_DK
V_DIM = RET_HEADS * RET_DV
N_EXPERTS = 16
EC_FACTOR = 2
LN_EPS = 1e-5

ADA_COLS = 768
PROJ_ROWS = 512
MIX_ROWS = 512
BATCH_GROUPS = 2
CONV_PAD = 16
CONV_BLOCK = 32
OUT_ROWS = 512
CAST_ROWS = 512
PACK_PARTS = 2
SC_WINDOW = 128
SLOT_SHIFT = 4
SLOT_RADIX = 1 << SLOT_SHIFT
TOKEN_SHIFT = 6
V7X_VMEM_LIMIT = 56 * 1024 * 1024


def _dot(a, b):
    return jnp.dot(a, b, preferred_element_type=F32)


def _dot_nt(a, b):
    return lax.dot_general(a, b, (((1,), (1,)), ((), ())), preferred_element_type=F32)


def _dot_tn(a, b):
    return lax.dot_general(a, b, (((0,), (0,)), ((), ())), preferred_element_type=F32)


def _split(a):
    hi = a.astype(BF16)
    lo = (a - hi.astype(F32)).astype(BF16)
    return hi, lo


def _norm(x):
    mu = jnp.mean(x, axis=-1, keepdims=True)
    xc = x - mu
    var = jnp.mean(xc * xc, axis=-1, keepdims=True)
    return xc * lax.rsqrt(var + LN_EPS)


def _sigmoid(x):
    return 1.0 / (1.0 + jnp.exp(-x))


def _silu(x):
    return x * _sigmoid(x)


def _col_iota(n):
    return lax.broadcasted_iota(jnp.int32, (n, 1), 0).astype(F32)


def _pack_pair(lo, hi):
    lo_bits = pltpu.bitcast(lo.astype(F32), U32)
    hi_bits = pltpu.bitcast(hi.astype(F32), U32)
    return lax.shift_right_logical(lo_bits, jnp.uint32(16)) | hi_bits


def _unpack_pair(word):
    lo = pltpu.bitcast(lax.shift_left(word, jnp.uint32(16)), F32)
    hi = pltpu.bitcast(word & jnp.uint32(0xFFFF0000), F32)
    return lo.astype(BF16), hi.astype(BF16)


def _smem():
    return pl.BlockSpec(memory_space=pltpu.SMEM)


def _params(sem, vmem=V7X_VMEM_LIMIT):
    return pltpu.CompilerParams(dimension_semantics=sem, vmem_limit_bytes=vmem)


def _ada_kernel(c_ref, w_ref, b_ref, o_ref):
    a_hi, a_lo = _split(_silu(c_ref[...]))
    w_hi, w_lo = _split(w_ref[...])
    o_ref[...] = _dot(a_hi, w_hi) + (_dot(a_hi, w_lo) + _dot(a_lo, w_hi)) + b_ref[...]


def _ada(cc, w, b):
    m, d = cc.shape
    n = w.shape[1]
    return pl.pallas_call(
        _ada_kernel,
        grid=(n // ADA_COLS,),
        in_specs=[pl.BlockSpec((m, d), lambda j: (0, 0)),
                  pl.BlockSpec((d, ADA_COLS), lambda j: (0, j)),
                  pl.BlockSpec((1, ADA_COLS), lambda j: (0, j))],
        out_specs=pl.BlockSpec((m, ADA_COLS), lambda j: (0, j)),
        out_shape=jax.ShapeDtypeStruct((m, n), F32),
        compiler_params=_params(("parallel",)),
        name="ada_proj",
    )(cc, w, b)


def _ctx_kernel(lgf_ref, lgb_ref, ctx_ref, sh_ref, sc_ref, w_ref, sf_ref, sb_ref):
    x = ctx_ref[0]
    lc = x.shape[0]
    h = (_norm(x) * (1.0 + sc_ref[...]) + sh_ref[...]).astype(BF16)
    kv = _dot(h, w_ref[...])
    t = _col_iota(lc)
    for hh in range(RET_HEADS):
        k = kv[:, hh * RET_DK:(hh + 1) * RET_DK] * (RET_DK ** -0.5)
        v = kv[:, QK_DIM + hh * RET_DV:QK_DIM + (hh + 1) * RET_DV].astype(BF16)
        wf = jnp.exp(lgf_ref[hh] * (lc - 1.0 - t))
        wb = jnp.exp(lgb_ref[hh] * t)
        sf_ref[0, hh] = _dot_tn((k * wf).astype(BF16), v)
        sb_ref[0, hh] = _dot_tn((k * wb).astype(BF16), v)


def _ctx_states(lgf, lgb, ctx, csh, csc, w_kvu):
    b, lc, d = ctx.shape
    kvw = QK_DIM + V_DIM
    st = jax.ShapeDtypeStruct((b, RET_HEADS, RET_DK, RET_DV), F32)
    st_spec = pl.BlockSpec((1, RET_HEADS, RET_DK, RET_DV), lambda i: (i, 0, 0, 0))
    return pl.pallas_call(
        _ctx_kernel,
        grid=(b,),
        in_specs=[_smem(), _smem(),
                  pl.BlockSpec((1, lc, d), lambda i: (i, 0, 0)),
                  pl.BlockSpec((1, d), lambda i: (0, 0)),
                  pl.BlockSpec((1, d), lambda i: (0, 0)),
                  pl.BlockSpec((d, kvw), lambda i: (0, 0))],
        out_specs=(st_spec, st_spec),
        out_shape=(st, st),
        compiler_params=_params(("parallel",)),
        name="ctx_states",
    )(lgf, lgb, ctx, csh, csc, w_kvu)


def _proj_kernel(lgf_ref, xp_ref, x_ref, xn_ref, sh_ref, sc_ref, w_ref, ck_ref, sk_ref, sf0_ref,
                 cw_ref, cb_ref, clg_ref, clb_ref,
                 h_ref, k_ref, v_ref, ya_ref, sf_ref, s_scr, y_scr, yres_scr, z_scr):
    t = pl.program_id(1)
    nt = pl.num_programs(1)

    @pl.when(t == 0)
    def _():
        s_scr[...] = sf0_ref[0]

    rows = x_ref.shape[1]
    ext = rows + 2 * CONV_PAD
    x_ext = jnp.concatenate([xp_ref[0], x_ref[0], xn_ref[0]], axis=0)
    h_ext = (_norm(x_ext) * (1.0 + sc_ref[0]) + sh_ref[0]).astype(BF16)
    h = h_ext[CONV_PAD:CONV_PAD + rows]
    h_ref[0] = h
    kvw = QK_DIM + V_DIM
    u = _dot(h_ext, w_ref[:, kvw:])
    row = lax.broadcasted_iota(jnp.int32, (ext, 1), 0)
    head_ok = jnp.where(t == 0, 0.0, 1.0)
    tail_ok = jnp.where(t == nt - 1, 0.0, 1.0)
    inside = jnp.where(row < CONV_PAD, head_ok, jnp.where(row >= CONV_PAD + rows, tail_ok, 1.0))
    y_scr[...] = u[:, :CONV_DIM] * _sigmoid(u[:, CONV_DIM:]) * inside

    kk = _dot(h, w_ref[:, :QK_DIM])
    vv = _dot(h, w_ref[:, QK_DIM:kvw]).astype(BF16)
    v_ref[0] = vv
    ck = ck_ref[...]
    sk = sk_ref[...]
    pos = _col_iota(RET_CHUNK)
    for hh in range(RET_HEADS):
        lg = lgf_ref[hh]
        kh = kk[:, hh * RET_DK:(hh + 1) * RET_DK]
        kr = kh * ck + pltpu.roll(kh, RET_DK // 2, 1) * sk
        k_ref[0, :, hh * RET_DK:(hh + 1) * RET_DK] = kr.astype(BF16)
        state_dec = jnp.exp(lg * (RET_CHUNK - 1.0 - pos))
        chunk_dec = jnp.exp(lg * jnp.full((1, RET_DV), float(RET_CHUNK), F32))
        for c in range(rows // RET_CHUNK):
            r0 = c * RET_CHUNK
            s = s_scr[hh]
            sf_ref[0, c, hh] = s.astype(BF16)
            kc = (kr[r0:r0 + RET_CHUNK] * state_dec).astype(BF16)
            vc = vv[r0:r0 + RET_CHUNK, hh * RET_DV:(hh + 1) * RET_DV]
            s_scr[hh] = s * chunk_dec + _dot_tn(kc, vc)

    span = ext - 8
    for r in range(8):
        yres_scr[r] = y_scr[pl.ds(r, span), :]

    for i in range(rows // CONV_BLOCK):
        base = i * CONV_BLOCK
        acc = None
        for w in range(CONV_WIDTH):
            a, r = divmod(w + 1, 8)
            wk = jnp.concatenate([cw_ref[8 * w:8 * w + 8, :]] * (CONV_BLOCK // 8), axis=0)
            tap = yres_scr[r, base + 8 * a:base + 8 * a + CONV_BLOCK, :] * wk
            acc = tap if acc is None else acc + tap
        z_scr[base:base + CONV_BLOCK, :] = acc
    z = _norm(z_scr[...] + cb_ref[...]) * clg_ref[...] + clb_ref[...]
    ya_ref[0] = _silu(z).astype(BF16)


def _proj(lgf, x, b0, sh1, sc1, w_kvu, ck, sk, sf0, cw, cb, clg, clb):
    _, l, d = x.shape
    b = sh1.shape[0]
    nt = l // PROJ_ROWS
    cpt = PROJ_ROWS // RET_CHUNK
    hpt = PROJ_ROWS // CONV_PAD
    n_halo = l // CONV_PAD
    ext = PROJ_ROWS + 2 * CONV_PAD
    tile = lambda w: pl.BlockSpec((1, PROJ_ROWS, w), lambda i, t: (i, t, 0))
    vec = pl.BlockSpec((1, 1, d), lambda i, t: (i, 0, 0))
    full = lambda a: pl.BlockSpec(a.shape, lambda i, t: (0,) * a.ndim, pipeline_mode=pl.Buffered(1))
    rope = pl.BlockSpec((PROJ_ROWS, RET_DK), lambda i, t: (t, 0))
    x_tile = pl.BlockSpec((1, PROJ_ROWS, d), lambda i, t: (i + b0, t, 0))
    prev = pl.BlockSpec((1, CONV_PAD, d), lambda i, t: (i + b0, jnp.maximum(t * hpt - 1, 0), 0))
    nxt = pl.BlockSpec((1, CONV_PAD, d), lambda i, t: (i + b0, jnp.minimum((t + 1) * hpt, n_halo - 1), 0))
    return pl.pallas_call(
        _proj_kernel,
        grid=(b, nt),
        in_specs=[_smem(), prev, x_tile, nxt, vec, vec, full(w_kvu), rope, rope,
                  pl.BlockSpec((1, RET_HEADS, RET_DK, RET_DV), lambda i, t: (i, 0, 0, 0)),
                  full(cw), full(cb), full(clg), full(clb)],
        out_specs=(tile(d), tile(QK_DIM), tile(V_DIM), tile(CONV_DIM),
                   pl.BlockSpec((1, cpt, RET_HEADS, RET_DK, RET_DV), lambda i, t: (i, t, 0, 0, 0))),
        out_shape=(jax.ShapeDtypeStruct((b, l, d), BF16),
                   jax.ShapeDtypeStruct((b, l, QK_DIM), BF16),
                   jax.ShapeDtypeStruct((b, l, V_DIM), BF16),
                   jax.ShapeDtypeStruct((b, l, CONV_DIM), BF16),
                   jax.ShapeDtypeStruct((b, l // RET_CHUNK, RET_HEADS, RET_DK, RET_DV), BF16)),
        scratch_shapes=[pltpu.VMEM((RET_HEADS, RET_DK, RET_DV), F32),
                        pltpu.VMEM((ext, CONV_DIM), F32),
                        pltpu.VMEM((8, ext - 8, CONV_DIM), F32),
                        pltpu.VMEM((PROJ_ROWS, CONV_DIM), F32)],
        compiler_params=_params(("parallel", "arbitrary")),
        name="proj_kv_conv",
    )(lgf, x, x, x, sh1, sc1, w_kvu, ck, sk, sf0, cw, cb, clg, clb)


def _mixer_kernel(alpha, lgf_ref, lgb_ref, x_ref, h_ref, k_ref, v_ref, sf_ref, sb0_ref, ya_ref,
                  wq_ref, wg_ref, wco_ref, wro_ref, wo_ref, gng_ref,
                  g1_ref, sh2_ref, sc2_ref, l1g_ref, l1b_ref, wr_ref, cq_ref, sq_ref,
                  x1_ref, hp_ref, lt_ref, sb_scr):
    t = pl.program_id(1)

    @pl.when(t == 0)
    def _():
        sb_scr[...] = sb0_ref[0]

    h = h_ref[0]
    rows = h.shape[0]
    cq = cq_ref[...]
    sq = sq_ref[...]
    q_all = _dot(h, wq_ref[...])
    pos = _col_iota(RET_CHUNK)
    di = lax.broadcasted_iota(jnp.int32, (RET_CHUNK, RET_CHUNK), 0)
    dj = lax.broadcasted_iota(jnp.int32, (RET_CHUNK, RET_CHUNK), 1)
    dist = (di - dj).astype(F32)

    n_chunks = rows // RET_CHUNK
    o_parts = [[None] * RET_HEADS for _ in range(n_chunks)]
    for hh in range(RET_HEADS):
        lgf = lgf_ref[hh]
        lgb = lgb_ref[hh]
        qh = q_all[:, hh * RET_DK:(hh + 1) * RET_DK]
        qr = qh * cq + pltpu.roll(qh, RET_DK // 2, 1) * sq
        decay = jnp.where(dist >= 0.0, jnp.exp(lgf * jnp.maximum(dist, 0.0)),
                          jnp.exp(lgb * jnp.maximum(-dist, 0.0)))
        cross_f = jnp.exp(lgf * (pos + 1.0))
        cross_b = jnp.exp(lgb * (RET_CHUNK - pos))
        state_b = jnp.exp(lgb * pos)
        chunk_b = jnp.exp(lgb * jnp.full((1, RET_DV), float(RET_CHUNK), F32))
        for c in reversed(range(n_chunks)):
            r0 = c * RET_CHUNK
            qc = qr[r0:r0 + RET_CHUNK]
            kc = k_ref[0, r0:r0 + RET_CHUNK, hh * RET_DK:(hh + 1) * RET_DK]
            vc = v_ref[0, r0:r0 + RET_CHUNK, hh * RET_DV:(hh + 1) * RET_DV]
            sb = sb_scr[hh]
            att = _dot_nt(qc.astype(BF16), kc) * decay
            lhs = jnp.concatenate([att.astype(BF16), (qc * cross_f).astype(BF16), (qc * cross_b).astype(BF16)], axis=1)
            rhs = jnp.concatenate([vc, sf_ref[0, c, hh], sb.astype(BF16)], axis=0)
            o = _dot(lhs, rhs)
            kb = (kc.astype(F32) * state_b).astype(BF16)
            sb_scr[hh] = sb * chunk_b + _dot_tn(kb, vc)
            o_parts[c][hh] = _norm(o)
    on = jnp.concatenate([jnp.concatenate(o_parts[c], axis=1) for c in range(n_chunks)], axis=0)

    d = x_ref.shape[-1]
    pw = d // (2 * PACK_PARTS)
    wr_hi, wr_lo = _split(wr_ref[...])
    g_ret = _dot(h, wg_ref[:, :V_DIM])
    yb_in = (_silu(g_ret) * (on * gng_ref[...])).astype(BF16)
    y_b = _dot(yb_in, wro_ref[...])
    y_a = _dot(ya_ref[0], wco_ref[...])
    g_a = _dot(h, wg_ref[:, V_DIM:V_DIM + d])
    g_b = _dot(h, wg_ref[:, V_DIM + d:])
    y = (_sigmoid(g_a) * y_a + _sigmoid(g_b) * y_b).astype(BF16)
    ym = _dot(y, wo_ref[...])

    x1 = _norm(alpha * x_ref[0] + g1_ref[0] * ym) * l1g_ref[...] + l1b_ref[...]
    x1_ref[0] = x1
    h2 = _norm(x1) * (1.0 + sc2_ref[0]) + sh2_ref[0]
    h2_hi, h2_lo = _split(h2)
    for p in range(PACK_PARTS):
        c0 = 2 * p * pw
        hp_ref[p, 0] = _pack_pair(h2_hi[:, c0:c0 + pw], h2_hi[:, c0 + pw:c0 + 2 * pw])
    lt_ref[0] = _dot_nt(wr_hi, h2_hi) + (_dot_nt(wr_hi, h2_lo) + _dot_nt(wr_lo, h2_hi))


def _mixer(alpha, lgf, lgb, x, b0, h, k, v, sf, sb0, ya, wq, wg, wco, wro, wo, gng,
           g1, sh2, sc2, l1g, l1b, wr_t, cq, sq):
    b, l, d = h.shape
    nt = l // MIX_ROWS
    cpt = MIX_ROWS // RET_CHUNK
    pw = d // (2 * PACK_PARTS)
    rev = lambda w: pl.BlockSpec((1, MIX_ROWS, w), lambda i, t: (i, nt - 1 - t, 0))
    vec = pl.BlockSpec((1, 1, d), lambda i, t: (i, 0, 0))
    full = lambda a: pl.BlockSpec(a.shape, lambda i, t: (0,) * a.ndim, pipeline_mode=pl.Buffered(1))
    rope = pl.BlockSpec((MIX_ROWS, RET_DK), lambda i, t: (nt - 1 - t, 0))
    state = pl.BlockSpec((1, RET_HEADS, RET_DK, RET_DV), lambda i, t: (i, 0, 0, 0))
    return pl.pallas_call(
        functools.partial(_mixer_kernel, alpha),
        grid=(b, nt),
        in_specs=[_smem(), _smem(),
                  pl.BlockSpec((1, MIX_ROWS, d), lambda i, t: (i + b0, nt - 1 - t, 0)),
                  rev(d), rev(QK_DIM), rev(V_DIM),
                  pl.BlockSpec((1, cpt, RET_HEADS, RET_DK, RET_DV), lambda i, t: (i, nt - 1 - t, 0, 0, 0)),
                  state, rev(CONV_DIM),
                  full(wq), full(wg), full(wco), full(wro), full(wo), full(gng),
                  vec, vec, vec, full(l1g), full(l1b), full(wr_t), rope, rope],
        out_specs=(rev(d),
                   pl.BlockSpec((PACK_PARTS, 1, MIX_ROWS, pw), lambda i, t: (0, i, nt - 1 - t, 0)),
                   pl.BlockSpec((1, N_EXPERTS, MIX_ROWS), lambda i, t: (i, 0, nt - 1 - t))),
        out_shape=(jax.ShapeDtypeStruct((b, l, d), F32),
                   jax.ShapeDtypeStruct((PACK_PARTS, b, l, pw), U32),
                   jax.ShapeDtypeStruct((b, N_EXPERTS, l), F32)),
        scratch_shapes=[pltpu.VMEM((RET_HEADS, RET_DK, RET_DV), F32)],
        compiler_params=_params(("parallel", "arbitrary")),
        name="mixer",
    )(lgf, lgb, x, h, k, v, sf, sb0, ya, wq, wg, wco, wro, wo, gng,
      g1, sh2, sc2, l1g, l1b, wr_t, cq, sq)


def _route_kernel(cap, lt_ref, rank_ref, aff_ref, idx_ref):
    logits = lt_ref[0]
    n_e, l = logits.shape
    m = jnp.max(logits, axis=0, keepdims=True)
    p = jnp.exp(logits - m)
    aff = p / jnp.sum(p, axis=0, keepdims=True)
    aff_ref[0] = aff
    capf = float(cap)

    def count(ones):
        return jnp.sum(ones, axis=1, keepdims=True)

    def refine(lo, shift, width, keep):
        best = lo
        for digit in range(1, 1 << width):
            cand = lo | jnp.left_shift(jnp.int32(digit), shift)
            best = jnp.where(keep(cand), cand, best)
        return best

    def bisect(n_bits, keep):
        lo = jnp.zeros((n_e, 1), jnp.int32)
        if n_bits % 2:
            lo = refine(lo, n_bits - 1, 1, keep)
        return lax.fori_loop(0, n_bits // 2, lambda i, v: refine(v, 2 * (n_bits // 2 - 1 - i), 2, keep), lo)

    thr = pltpu.bitcast(
        bisect(31, lambda cand: count(jnp.where(aff >= pltpu.bitcast(cand, F32), 1.0, 0.0)) >= capf), F32)
    gt = jnp.where(aff > thr, 1.0, 0.0)
    eq = jnp.where(aff == thr, 1.0, 0.0)
    need = capf - count(gt)
    idx = lax.broadcasted_iota(jnp.int32, (n_e, l), 1)
    idx_bits = int(l - 1).bit_length()

    last = bisect(idx_bits, lambda cand: count(jnp.where(idx < cand, eq, 0.0)) < need)
    bound = jnp.where(need > 0.0, last + 1, 0)
    sel = gt + jnp.where(idx < bound, eq, 0.0)

    blk = 128
    ti = lax.broadcasted_iota(jnp.int32, (blk, blk), 0)
    tj = lax.broadcasted_iota(jnp.int32, (blk, blk), 1)
    tri = jnp.where(ti < tj, 1.0, 0.0).astype(BF16)
    offset = jnp.zeros((n_e, 1), F32)
    for j in range(l // blk):
        sj = sel[:, j * blk:(j + 1) * blk]
        before = _dot(sj.astype(BF16), tri) + offset
        rank_ref[0, :, j * blk:(j + 1) * blk] = jnp.where(sj > 0.0, before, -1.0).astype(jnp.int32)
        offset = offset + jnp.sum(sj, axis=1, keepdims=True)

    rank = rank_ref[0]
    hi_digit = jnp.right_shift(rank, SLOT_SHIFT)
    lo_digit = jnp.bitwise_and(rank, SLOT_RADIX - 1)
    digit = lax.broadcasted_iota(jnp.int32, (SLOT_RADIX, l), 0)
    tok_hi = jnp.right_shift(idx[:1], TOKEN_SHIFT).astype(F32)
    tok_lo = jnp.bitwise_and(idx[:1], (1 << TOKEN_SHIFT) - 1).astype(F32)
    a_hi, a_lo, b_rows = [], [], []
    for e in range(n_e):
        is_a = hi_digit[e:e + 1] == digit
        a_hi.append(jnp.where(is_a, tok_hi, 0.0))
        a_lo.append(jnp.where(is_a, tok_lo, 0.0))
        b_rows.append(jnp.where(lo_digit[e:e + 1] == digit, 1.0, 0.0))
    b_all = jnp.concatenate(b_rows, axis=0).astype(BF16)
    cross_hi = _dot_nt(jnp.concatenate(a_hi, axis=0).astype(BF16), b_all)
    cross_lo = _dot_nt(jnp.concatenate(a_lo, axis=0).astype(BF16), b_all)
    cross = cross_hi * float(1 << TOKEN_SHIFT) + cross_lo + (pl.program_id(0) * l).astype(F32)
    for e in range(n_e):
        s0 = e * SLOT_RADIX
        idx_ref[0, e] = cross[s0:s0 + SLOT_RADIX, s0:s0 + SLOT_RADIX].astype(jnp.int32)


def _route(lt, cap):
    b, n_e, l = lt.shape
    assert cap == SLOT_RADIX * SLOT_RADIX and l <= (1 << TOKEN_SHIFT) * 256
    spec = pl.BlockSpec((1, n_e, l), lambda i: (i, 0, 0))
    return pl.pallas_call(
        functools.partial(_route_kernel, cap),
        grid=(b,),
        in_specs=[spec],
        out_specs=(spec, spec, pl.BlockSpec((1, n_e, SLOT_RADIX, SLOT_RADIX), lambda i: (i, 0, 0, 0))),
        out_shape=(jax.ShapeDtypeStruct((b, n_e, l), jnp.int32),
                   jax.ShapeDtypeStruct((b, n_e, l), F32),
                   jax.ShapeDtypeStruct((b, n_e, SLOT_RADIX, SLOT_RADIX), jnp.int32)),
        compiler_params=_params(("parallel",)),
        name="route_topc",
    )(lt)


def _sc_gather(rows, idx):
    n = idx.shape[0]
    w = rows.shape[1]
    mesh = plsc.VectorSubcoreMesh(core_axis_name="c", subcore_axis_name="s")

    @pl.kernel(out_type=jax.ShapeDtypeStruct((n, w), rows.dtype), mesh=mesh, scratch_types=[])
    def gather(rows_hbm, idx_hbm, out_hbm):
        def window(idx_vmem, out_vmem):
            pltpu.sync_copy(rows_hbm.at[idx_vmem.at[0]], out_vmem)

        pltpu.emit_pipeline(
            window,
            grid=(n // SC_WINDOW,),
            in_specs=[pl.BlockSpec((1, SC_WINDOW), index_map=lambda i: (0, i))],
            out_specs=[pl.BlockSpec((SC_WINDOW, w), index_map=lambda i: (i, 0))],
            core_axis_name=("c", "s"),
            dimension_semantics=(pltpu.PARALLEL,),
        )(idx_hbm, out_hbm)

    return gather(rows, idx.reshape(1, n))


def _cast_kernel(wg_ref, wu_ref, wd_ref, wgu_ref, wdo_ref):
    f = wg_ref.shape[-1]
    wgu_ref[0, :, :f] = wg_ref[0].astype(BF16)
    wgu_ref[0, :, f:] = wu_ref[0].astype(BF16)
    wdo_ref[0] = wd_ref[0].astype(BF16)


def _expert_weights(wgate, wup, wdown):
    n_e, d, f = wgate.shape
    rows = CAST_ROWS
    assert d % rows == 0 and f % rows == 0 and d == f
    src = lambda c: pl.BlockSpec((1, rows, c), lambda e, t: (e, t, 0))
    return pl.pallas_call(
        _cast_kernel,
        grid=(n_e, d // rows),
        in_specs=[src(f), src(f), src(d)],
        out_specs=(src(2 * f), src(d)),
        out_shape=(jax.ShapeDtypeStruct((n_e, d, 2 * f), BF16), jax.ShapeDtypeStruct((n_e, f, d), BF16)),
        compiler_params=_params(("parallel", "parallel")),
        name="expert_weight_cast",
    )(wgate, wup, wdown)


def _moe_kernel(cap, alpha, rank_ref, aff_ref, xe_ref, wgu_ref, wd_ref,
                x1_ref, g2_ref, lng_ref, lnb_ref, *rest):
    o_ref = rest[-1]
    e = pl.program_id(1)
    l = o_ref.shape[1]
    n_tiles = l // OUT_ROWS

    @pl.when(e == 0)
    def _():
        o_ref[0] = jnp.zeros(o_ref.shape[1:], F32)

    rank = rank_ref[0, 0]
    slot = lax.broadcasted_iota(jnp.int32, (cap, l), 0)
    hit = slot == rank
    pick = jnp.where(hit, 1.0, 0.0).astype(BF16)
    gate = jnp.sum(jnp.where(hit, aff_ref[0, 0], 0.0), axis=1, keepdims=True)
    halves = []
    for p in range(PACK_PARTS):
        halves.extend(_unpack_pair(xe_ref[p, 0, 0]))
    xe = jnp.concatenate(halves, axis=1)
    gu = _dot(xe, wgu_ref[0])
    f = gu.shape[1] // 2
    he = (_silu(gu[:, :f]) * gu[:, f:]).astype(BF16)
    ye = (_dot(he, wd_ref[0]) * gate).astype(BF16)
    for i in range(n_tiles):
        r0 = i * OUT_ROWS
        o_ref[0, r0:r0 + OUT_ROWS, :] += _dot_tn(pick[:, r0:r0 + OUT_ROWS], ye)

    @pl.when(e == pl.num_programs(1) - 1)
    def _():
        def tile(i, carry):
            r0 = pl.multiple_of(i * OUT_ROWS, OUT_ROWS)
            rows = pl.ds(r0, OUT_ROWS)
            z = alpha * x1_ref[0, rows, :] + g2_ref[0] * o_ref[0, rows, :]
            o_ref[0, rows, :] = _norm(z) * lng_ref[...] + lnb_ref[...]
            return carry

        lax.fori_loop(0, n_tiles, tile, 0)


def _moe(alpha, rank, aff, xe, wgu, wdown, x1, g2, ln_g, ln_b, cap, b_total, b0, earlier):
    b, l, d = x1.shape
    n_e, f, _ = wdown.shape
    row = pl.BlockSpec((1, 1, 1, l), lambda i, e: (i, e, 0, 0))
    vec = pl.BlockSpec((1, d), lambda i, e: (0, 0))
    in_specs = [row, row,
                pl.BlockSpec((PACK_PARTS, 1, 1, cap, xe.shape[-1]), lambda i, e: (0, i, e, 0, 0)),
                pl.BlockSpec((1, d, 2 * f), lambda i, e: (e, 0, 0)),
                pl.BlockSpec((1, f, d), lambda i, e: (e, 0, 0)),
                pl.BlockSpec((1, l, d), lambda i, e: (i, 0, 0)),
                pl.BlockSpec((1, 1, d), lambda i, e: (i, 0, 0)), vec, vec]
    args = [rank.reshape(b, n_e, 1, l), aff.reshape(b, n_e, 1, l), xe, wgu, wdown, x1, g2, ln_g, ln_b]
    aliases = {}
    if earlier is not None:
        in_specs.append(pl.BlockSpec(memory_space=pl.ANY))
        aliases = {len(args): 0}
        args.append(earlier)
    return pl.pallas_call(
        functools.partial(_moe_kernel, cap, alpha),
        grid=(b, n_e),
        in_specs=in_specs,
        out_specs=pl.BlockSpec((1, l, d), lambda i, e: (i + b0, 0, 0)),
        out_shape=jax.ShapeDtypeStruct((b_total, l, d), F32),
        input_output_aliases=aliases,
        compiler_params=_params(("parallel", "arbitrary")),
        name="moe_experts",
    )(*args)


def _rope_tables(l):
    n_axis = RET_DK // 4
    freqs = ROPE_BASE ** (-np.arange(n_axis, dtype=np.float64) / n_axis)
    pos = np.arange(l)
    ang = np.concatenate([(pos // GRID_W)[:, None] * freqs, (pos % GRID_W)[:, None] * freqs], axis=-1)
    cos, sin = np.cos(ang), np.sin(ang)
    return (np.concatenate([cos, cos], axis=-1).astype(np.float32),
            np.concatenate([-sin, sin], axis=-1).astype(np.float32))


def kernel(x, c, ctx, c_ctx, w_ada, b_ada, w_in, conv_w, conv_b, conv_ln_g, conv_ln_b, w_conv_out,
           log_decay_f, log_decay_b, ret_gn_g, w_ret_out, w_out, ln1_g, ln1_b,
           w_router, w_gate, w_up, w_down, ln2_g, ln2_b):
    depth = w_ada.shape[0]
    assert depth == 1, "single trunk layer"
    b, l, d = x.shape
    alpha = (2.0 * depth) ** 0.25
    cap = EC_FACTOR * l // N_EXPERTS
    u_end = 2 * CONV_DIM
    q_end = u_end + QK_DIM
    k_end = q_end + QK_DIM
    v_end = k_end + V_DIM
    row = lambda a: a.reshape(1, -1)

    n_mod = b + 1
    pad = (-n_mod) % 8
    cc = jnp.concatenate([c, c_ctx[None], jnp.zeros((pad, d), F32)], axis=0)
    mod = _ada(cc, w_ada[0], row(b_ada[0]))
    sh1, sc1, g1, sh2, sc2, g2 = [m.reshape(b, 1, d) for m in jnp.split(mod[:b], 6, axis=-1)]
    csh1, csc1 = mod[b:b + 1, :d], mod[b:b + 1, d:2 * d]

    perm = np.concatenate([np.arange(0, RET_DK, 2), np.arange(1, RET_DK, 2)])
    perm = (np.arange(RET_HEADS)[:, None] * RET_DK + perm[None, :]).reshape(-1)
    w = w_in[0]
    w_kvu = jnp.concatenate([w[:, q_end:k_end][:, perm], w[:, k_end:v_end], w[:, :u_end]], axis=1).astype(BF16)
    wq = w[:, u_end:q_end][:, perm].astype(BF16)
    wg = w[:, v_end:].astype(BF16)

    cos_t, sin_t = _rope_tables(l)
    cq, sq = jnp.asarray(cos_t), jnp.asarray(sin_t)
    k_scale = RET_DK ** -0.5
    ck, sk = jnp.asarray(cos_t * k_scale), jnp.asarray(sin_t * k_scale)

    lgf, lgb = log_decay_f[0], log_decay_b[0]
    sf0, sb0 = _ctx_states(lgf, lgb, ctx, csh1, csc1, w_kvu)
    cw8 = jnp.repeat(conv_w[0], 8, axis=0)
    wco, wro, wo = w_conv_out[0].astype(BF16), w_ret_out[0].astype(BF16), w_out[0].astype(BF16)
    wgu, wde = _expert_weights(w_gate[0], w_up[0], w_down[0])
    wr_t = w_router[0].T
    groups = BATCH_GROUPS if b % BATCH_GROUPS == 0 else 1
    nb = b // groups
    out = None
    for gi in range(groups):
        b0 = gi * nb
        part = lambda a: a[b0:b0 + nb]
        h, k, v, ya, sf = _proj(lgf, x, b0, part(sh1), part(sc1), w_kvu, ck, sk, part(sf0),
                                cw8, row(conv_b[0]), row(conv_ln_g[0]), row(conv_ln_b[0]))
        x1, hp, lt = _mixer(alpha, lgf, lgb, x, b0, h, k, v, sf, part(sb0), ya, wq, wg, wco, wro, wo,
                            row(ret_gn_g[0]), part(g1), part(sh2), part(sc2), row(ln1_g[0]), row(ln1_b[0]),
                            wr_t, cq, sq)
        rank, aff, slots = _route(lt, cap)
        part_rows = nb * l
        ids = jnp.concatenate([slots.reshape(-1) + p * part_rows for p in range(PACK_PARTS)])
        xe = _sc_gather(hp.reshape(PACK_PARTS * part_rows, hp.shape[-1]), ids)
        xe = xe.reshape(PACK_PARTS, nb, N_EXPERTS, cap, hp.shape[-1])
        out = _moe(alpha, rank, aff, xe, wgu, wde, x1, part(g2), row(ln2_g[0]), row(ln2_b[0]),
                   cap, b, b0, out)
    return out
```

```python
import functools

import numpy as np
import jax
import jax.numpy as jnp
from jax import lax
from jax.experimental import pallas as pl
from jax.experimental.pallas import tpu as pltpu
from jax.experimental.pallas import tpu_sc as plsc

F32 = jnp.float32
BF16 = jnp.bfloat16
U32 = jnp.uint32

GRID_W = 64
CONV_DIM = 512
CONV_WIDTH = 31
CONV_HALO = CONV_WIDTH // 2
RET_HEADS = 4
RET_DK = 128
RET_DV = 256
RET_CHUNK = 128
ROPE_BASE = 10000.0
QK_DIM = RET_HEADS * RET_DK
V_DIM = RET_HEADS * RET_DV
N_EXPERTS = 16
EC_FACTOR = 2
LN_EPS = 1e-5

ADA_COLS = 768
PROJ_ROWS = 512
MIX_ROWS = 512
BATCH_GROUPS = 2
CONV_PAD = 16
CONV_BLOCK = 32
OUT_ROWS = 512
FFN_ROWS = 1024
PACK_PARTS = 2
SC_WINDOW = 128
SLOT_SHIFT = 4
SLOT_RADIX = 1 << SLOT_SHIFT
TOKEN_SHIFT = 6
V7X_VMEM_LIMIT = 56 * 1024 * 1024


def _dot(a, b):
    return jnp.dot(a, b, preferred_element_type=F32)


def _dot_nt(a, b):
    return lax.dot_general(a, b, (((1,), (1,)), ((), ())), preferred_element_type=F32)


def _dot_tn(a, b):
    return lax.dot_general(a, b, (((0,), (0,)), ((), ())), preferred_element_type=F32)


def _split(a):
    hi = a.astype(BF16)
    lo = (a - hi.astype(F32)).astype(BF16)
    return hi, lo


def _norm(x):
    mu = jnp.mean(x, axis=-1, keepdims=True)
    xc = x - mu
    var = jnp.mean(xc * xc, axis=-1, keepdims=True)
    return xc * lax.rsqrt(var + LN_EPS)


def _sigmoid(x):
    return 1.0 / (1.0 + jnp.exp(-x))


def _silu(x):
    return x * _sigmoid(x)


def _col_iota(n):
    return lax.broadcasted_iota(jnp.int32, (n, 1), 0).astype(F32)


def _pack_pair(lo, hi):
    lo_bits = pltpu.bitcast(lo.astype(F32), U32)
    hi_bits = pltpu.bitcast(hi.astype(F32), U32)
    return lax.shift_right_logical(lo_bits, jnp.uint32(16)) | hi_bits


def _unpack_pair(word):
    lo = pltpu.bitcast(lax.shift_left(word, jnp.uint32(16)), F32)
    hi = pltpu.bitcast(word & jnp.uint32(0xFFFF0000), F32)
    return lo.astype(BF16), hi.astype(BF16)


def _smem():
    return pl.BlockSpec(memory_space=pltpu.SMEM)


def _params(sem, vmem=V7X_VMEM_LIMIT):
    return pltpu.CompilerParams(dimension_semantics=sem, vmem_limit_bytes=vmem)


def _ada_kernel(c_ref, w_ref, b_ref, o_ref):
    a_hi, a_lo = _split(_silu(c_ref[...]))
    w_hi, w_lo = _split(w_ref[...])
    o_ref[...] = _dot(a_hi, w_hi) + (_dot(a_hi, w_lo) + _dot(a_lo, w_hi)) + b_ref[...]


def _ada(cc, w, b):
    m, d = cc.shape
    n = w.shape[1]
    return pl.pallas_call(
        _ada_kernel,
        grid=(n // ADA_COLS,),
        in_specs=[pl.BlockSpec((m, d), lambda j: (0, 0)),
                  pl.BlockSpec((d, ADA_COLS), lambda j: (0, j)),
                  pl.BlockSpec((1, ADA_COLS), lambda j: (0, j))],
        out_specs=pl.BlockSpec((m, ADA_COLS), lambda j: (0, j)),
        out_shape=jax.ShapeDtypeStruct((m, n), F32),
        compiler_params=_params(("parallel",)),
        name="ada_proj",
    )(cc, w, b)


def _ctx_kernel(lgf_ref, lgb_ref, ctx_ref, sh_ref, sc_ref, w_ref, sf_ref, sb_ref):
    x = ctx_ref[0]
    lc = x.shape[0]
    h = (_norm(x) * (1.0 + sc_ref[...]) + sh_ref[...]).astype(BF16)
    kv = _dot(h, w_ref[...])
    t = _col_iota(lc)
    for hh in range(RET_HEADS):
        k = kv[:, hh * RET_DK:(hh + 1) * RET_DK] * (RET_DK ** -0.5)
        v = kv[:, QK_DIM + hh * RET_DV:QK_DIM + (hh + 1) * RET_DV].astype(BF16)
        wf = jnp.exp(lgf_ref[hh] * (lc - 1.0 - t))
        wb = jnp.exp(lgb_ref[hh] * t)
        sf_ref[0, hh] = _dot_tn((k * wf).astype(BF16), v)
        sb_ref[0, hh] = _dot_tn((k * wb).astype(BF16), v)


def _ctx_states(lgf, lgb, ctx, csh, csc, w_kvu):
    b, lc, d = ctx.shape
    kvw = QK_DIM + V_DIM
    st = jax.ShapeDtypeStruct((b, RET_HEADS, RET_DK, RET_DV), F32)
    st_spec = pl.BlockSpec((1, RET_HEADS, RET_DK, RET_DV), lambda i: (i, 0, 0, 0))
    return pl.pallas_call(
        _ctx_kernel,
        grid=(b,),
        in_specs=[_smem(), _smem(),
                  pl.BlockSpec((1, lc, d), lambda i: (i, 0, 0)),
                  pl.BlockSpec((1, d), lambda i: (0, 0)),
                  pl.BlockSpec((1, d), lambda i: (0, 0)),
                  pl.BlockSpec((d, kvw), lambda i: (0, 0))],
        out_specs=(st_spec, st_spec),
        out_shape=(st, st),
        compiler_params=_params(("parallel",)),
        name="ctx_states",
    )(lgf, lgb, ctx, csh, csc, w_kvu)


def _proj_kernel(lgf_ref, xp_ref, x_ref, xn_ref, sh_ref, sc_ref, w_ref, ck_ref, sk_ref, sf0_ref,
                 cw_ref, cb_ref, clg_ref, clb_ref,
                 h_ref, k_ref, v_ref, ya_ref, sf_ref, s_scr, y_scr, yres_scr, z_scr):
    t = pl.program_id(1)
    nt = pl.num_programs(1)

    @pl.when(t == 0)
    def _():
        s_scr[...] = sf0_ref[0]

    rows = x_ref.shape[1]
    ext = rows + 2 * CONV_PAD
    x_ext = jnp.concatenate([xp_ref[0], x_ref[0], xn_ref[0]], axis=0)
    h_ext = (_norm(x_ext) * (1.0 + sc_ref[0]) + sh_ref[0]).astype(BF16)
    h = h_ext[CONV_PAD:CONV_PAD + rows]
    h_ref[0] = h
    kvw = QK_DIM + V_DIM
    u = _dot(h_ext, w_ref[:, kvw:])
    row = lax.broadcasted_iota(jnp.int32, (ext, 1), 0)
    head_ok = jnp.where(t == 0, 0.0, 1.0)
    tail_ok = jnp.where(t == nt - 1, 0.0, 1.0)
    inside = jnp.where(row < CONV_PAD, head_ok, jnp.where(row >= CONV_PAD + rows, tail_ok, 1.0))
    y_scr[...] = u[:, :CONV_DIM] * _sigmoid(u[:, CONV_DIM:]) * inside

    kk = _dot(h, w_ref[:, :QK_DIM])
    vv = _dot(h, w_ref[:, QK_DIM:kvw]).astype(BF16)
    v_ref[0] = vv
    ck = ck_ref[...]
    sk = sk_ref[...]
    pos = _col_iota(RET_CHUNK)
    for hh in range(RET_HEADS):
        lg = lgf_ref[hh]
        kh = kk[:, hh * RET_DK:(hh + 1) * RET_DK]
        kr = kh * ck + pltpu.roll(kh, RET_DK // 2, 1) * sk
        k_ref[0, :, hh * RET_DK:(hh + 1) * RET_DK] = kr.astype(BF16)
        state_dec = jnp.exp(lg * (RET_CHUNK - 1.0 - pos))
        chunk_dec = jnp.exp(lg * jnp.full((1, RET_DV), float(RET_CHUNK), F32))
        for c in range(rows // RET_CHUNK):
            r0 = c * RET_CHUNK
            s = s_scr[hh]
            sf_ref[0, c, hh] = s.astype(BF16)
            kc = (kr[r0:r0 + RET_CHUNK] * state_dec).astype(BF16)
            vc = vv[r0:r0 + RET_CHUNK, hh * RET_DV:(hh + 1) * RET_DV]
            s_scr[hh] = s * chunk_dec + _dot_tn(kc, vc)

    span = ext - 8
    for r in range(8):
        yres_scr[r] = y_scr[pl.ds(r, span), :]

    for i in range(rows // CONV_BLOCK):
        base = i * CONV_BLOCK
        acc = None
        for w in range(CONV_WIDTH):
            a, r = divmod(w + 1, 8)
            wk = jnp.concatenate([cw_ref[8 * w:8 * w + 8, :]] * (CONV_BLOCK // 8), axis=0)
            tap = yres_scr[r, base + 8 * a:base + 8 * a + CONV_BLOCK, :] * wk
            acc = tap if acc is None else acc + tap
        z_scr[base:base + CONV_BLOCK, :] = acc
    z = _norm(z_scr[...] + cb_ref[...]) * clg_ref[...] + clb_ref[...]
    ya_ref[0] = _silu(z).astype(BF16)


def _proj(lgf, x, b0, sh1, sc1, w_kvu, ck, sk, sf0, cw, cb, clg, clb):
    _, l, d = x.shape
    b = sh1.shape[0]
    nt = l // PROJ_ROWS
    cpt = PROJ_ROWS // RET_CHUNK
    hpt = PROJ_ROWS // CONV_PAD
    n_halo = l // CONV_PAD
    ext = PROJ_ROWS + 2 * CONV_PAD
    tile = lambda w: pl.BlockSpec((1, PROJ_ROWS, w), lambda i, t: (i, t, 0))
    vec = pl.BlockSpec((1, 1, d), lambda i, t: (i, 0, 0))
    full = lambda a: pl.BlockSpec(a.shape, lambda i, t: (0,) * a.ndim, pipeline_mode=pl.Buffered(1))
    rope = pl.BlockSpec((PROJ_ROWS, RET_DK), lambda i, t: (t, 0))
    x_tile = pl.BlockSpec((1, PROJ_ROWS, d), lambda i, t: (i + b0, t, 0))
    prev = pl.BlockSpec((1, CONV_PAD, d), lambda i, t: (i + b0, jnp.maximum(t * hpt - 1, 0), 0))
    nxt = pl.BlockSpec((1, CONV_PAD, d), lambda i, t: (i + b0, jnp.minimum((t + 1) * hpt, n_halo - 1), 0))
    return pl.pallas_call(
        _proj_kernel,
        grid=(b, nt),
        in_specs=[_smem(), prev, x_tile, nxt, vec, vec, full(w_kvu), rope, rope,
                  pl.BlockSpec((1, RET_HEADS, RET_DK, RET_DV), lambda i, t: (i, 0, 0, 0)),
                  full(cw), full(cb), full(clg), full(clb)],
        out_specs=(tile(d), tile(QK_DIM), tile(V_DIM), tile(CONV_DIM),
                   pl.BlockSpec((1, cpt, RET_HEADS, RET_DK, RET_DV), lambda i, t: (i, t, 0, 0, 0))),
        out_shape=(jax.ShapeDtypeStruct((b, l, d), BF16),
                   jax.ShapeDtypeStruct((b, l, QK_DIM), BF16),
                   jax.ShapeDtypeStruct((b, l, V_DIM), BF16),
                   jax.ShapeDtypeStruct((b, l, CONV_DIM), BF16),
                   jax.ShapeDtypeStruct((b, l // RET_CHUNK, RET_HEADS, RET_DK, RET_DV), BF16)),
        scratch_shapes=[pltpu.VMEM((RET_HEADS, RET_DK, RET_DV), F32),
                        pltpu.VMEM((ext, CONV_DIM), F32),
                        pltpu.VMEM((8, ext - 8, CONV_DIM), F32),
                        pltpu.VMEM((PROJ_ROWS, CONV_DIM), F32)],
        compiler_params=_params(("parallel", "arbitrary")),
        name="proj_kv_conv",
    )(lgf, x, x, x, sh1, sc1, w_kvu, ck, sk, sf0, cw, cb, clg, clb)


def _mixer_kernel(alpha, lgf_ref, lgb_ref, x_ref, h_ref, k_ref, v_ref, sf_ref, sb0_ref, ya_ref,
                  wq_ref, wg_ref, wco_ref, wro_ref, wo_ref, gng_ref,
                  g1_ref, sh2_ref, sc2_ref, l1g_ref, l1b_ref, wr_ref, cq_ref, sq_ref,
                  x1_ref, hp_ref, lt_ref, sb_scr):
    t = pl.program_id(1)

    @pl.when(t == 0)
    def _():
        sb_scr[...] = sb0_ref[0]

    h = h_ref[0]
    rows = h.shape[0]
    cq = cq_ref[...]
    sq = sq_ref[...]
    q_all = _dot(h, wq_ref[...])
    pos = _col_iota(RET_CHUNK)
    di = lax.broadcasted_iota(jnp.int32, (RET_CHUNK, RET_CHUNK), 0)
    dj = lax.broadcasted_iota(jnp.int32, (RET_CHUNK, RET_CHUNK), 1)
    dist = (di - dj).astype(F32)

    n_chunks = rows // RET_CHUNK
    o_parts = [[None] * RET_HEADS for _ in range(n_chunks)]
    for hh in range(RET_HEADS):
        lgf = lgf_ref[hh]
        lgb = lgb_ref[hh]
        qh = q_all[:, hh * RET_DK:(hh + 1) * RET_DK]
        qr = qh * cq + pltpu.roll(qh, RET_DK // 2, 1) * sq
        decay = jnp.where(dist >= 0.0, jnp.exp(lgf * jnp.maximum(dist, 0.0)),
                          jnp.exp(lgb * jnp.maximum(-dist, 0.0)))
        cross_f = jnp.exp(lgf * (pos + 1.0))
        cross_b = jnp.exp(lgb * (RET_CHUNK - pos))
        state_b = jnp.exp(lgb * pos)
        chunk_b = jnp.exp(lgb * jnp.full((1, RET_DV), float(RET_CHUNK), F32))
        for c in reversed(range(n_chunks)):
            r0 = c * RET_CHUNK
            qc = qr[r0:r0 + RET_CHUNK]
            kc = k_ref[0, r0:r0 + RET_CHUNK, hh * RET_DK:(hh + 1) * RET_DK]
            vc = v_ref[0, r0:r0 + RET_CHUNK, hh * RET_DV:(hh + 1) * RET_DV]
            sb = sb_scr[hh]
            att = _dot_nt(qc.astype(BF16), kc) * decay
            lhs = jnp.concatenate([att.astype(BF16), (qc * cross_f).astype(BF16), (qc * cross_b).astype(BF16)], axis=1)
            rhs = jnp.concatenate([vc, sf_ref[0, c, hh], sb.astype(BF16)], axis=0)
            o = _dot(lhs, rhs)
            kb = (kc.astype(F32) * state_b).astype(BF16)
            sb_scr[hh] = sb * chunk_b + _dot_tn(kb, vc)
            o_parts[c][hh] = _norm(o)
    on = jnp.concatenate([jnp.concatenate(o_parts[c], axis=1) for c in range(n_chunks)], axis=0)

    d = x_ref.shape[-1]
    pw = d // (2 * PACK_PARTS)
    wr_hi, wr_lo = _split(wr_ref[...])
    g_ret = _dot(h, wg_ref[:, :V_DIM])
    yb_in = (_silu(g_ret) * (on * gng_ref[...])).astype(BF16)
    y_b = _dot(yb_in, wro_ref[...])
    y_a = _dot(ya_ref[0], wco_ref[...])
    g_a = _dot(h, wg_ref[:, V_DIM:V_DIM + d])
    g_b = _dot(h, wg_ref[:, V_DIM + d:])
    y = (_sigmoid(g_a) * y_a + _sigmoid(g_b) * y_b).astype(BF16)
    ym = _dot(y, wo_ref[...])

    x1 = _norm(alpha * x_ref[0] + g1_ref[0] * ym) * l1g_ref[...] + l1b_ref[...]
    x1_ref[0] = x1
    h2 = _norm(x1) * (1.0 + sc2_ref[0]) + sh2_ref[0]
    h2_hi, h2_lo = _split(h2)
    for p in range(PACK_PARTS):
        c0 = 2 * p * pw
        hp_ref[p, 0] = _pack_pair(h2_hi[:, c0:c0 + pw], h2_hi[:, c0 + pw:c0 + 2 * pw])
    lt_ref[0] = _dot_nt(wr_hi, h2_hi) + (_dot_nt(wr_hi, h2_lo) + _dot_nt(wr_lo, h2_hi))


def _mixer(alpha, lgf, lgb, x, b0, h, k, v, sf, sb0, ya, wq, wg, wco, wro, wo, gng,
           g1, sh2, sc2, l1g, l1b, wr_t, cq, sq):
    b, l, d = h.shape
    nt = l // MIX_ROWS
    cpt = MIX_ROWS // RET_CHUNK
    pw = d // (2 * PACK_PARTS)
    rev = lambda w: pl.BlockSpec((1, MIX_ROWS, w), lambda i, t: (i, nt - 1 - t, 0))
    vec = pl.BlockSpec((1, 1, d), lambda i, t: (i, 0, 0))
    full = lambda a: pl.BlockSpec(a.shape, lambda i, t: (0,) * a.ndim, pipeline_mode=pl.Buffered(1))
    rope = pl.BlockSpec((MIX_ROWS, RET_DK), lambda i, t: (nt - 1 - t, 0))
    state = pl.BlockSpec((1, RET_HEADS, RET_DK, RET_DV), lambda i, t: (i, 0, 0, 0))
    return pl.pallas_call(
        functools.partial(_mixer_kernel, alpha),
        grid=(b, nt),
        in_specs=[_smem(), _smem(),
                  pl.BlockSpec((1, MIX_ROWS, d), lambda i, t: (i + b0, nt - 1 - t, 0)),
                  rev(d), rev(QK_DIM), rev(V_DIM),
                  pl.BlockSpec((1, cpt, RET_HEADS, RET_DK, RET_DV), lambda i, t: (i, nt - 1 - t, 0, 0, 0)),
                  state, rev(CONV_DIM),
                  full(wq), full(wg), full(wco), full(wro), full(wo), full(gng),
                  vec, vec, vec, full(l1g), full(l1b), full(wr_t), rope, rope],
        out_specs=(rev(d),
                   pl.BlockSpec((PACK_PARTS, 1, MIX_ROWS, pw), lambda i, t: (0, i, nt - 1 - t, 0)),
                   pl.BlockSpec((1, N_EXPERTS, MIX_ROWS), lambda i, t: (i, 0, nt - 1 - t))),
        out_shape=(jax.ShapeDtypeStruct((b, l, d), F32),
                   jax.ShapeDtypeStruct((PACK_PARTS, b, l, pw), U32),
                   jax.ShapeDtypeStruct((b, N_EXPERTS, l), F32)),
        scratch_shapes=[pltpu.VMEM((RET_HEADS, RET_DK, RET_DV), F32)],
        compiler_params=_params(("parallel", "arbitrary")),
        name="mixer",
    )(lgf, lgb, x, h, k, v, sf, sb0, ya, wq, wg, wco, wro, wo, gng,
      g1, sh2, sc2, l1g, l1b, wr_t, cq, sq)


def _route_kernel(cap, lt_ref, rank_ref, aff_ref, idx_ref):
    logits = lt_ref[0]
    n_e, l = logits.shape
    m = jnp.max(logits, axis=0, keepdims=True)
    p = jnp.exp(logits - m)
    aff = p / jnp.sum(p, axis=0, keepdims=True)
    aff_ref[0] = aff
    capf = float(cap)

    def count(ones):
        return jnp.sum(ones, axis=1, keepdims=True)

    def refine(lo, shift, width, keep):
        best = lo
        for digit in range(1, 1 << width):
            cand = lo | jnp.left_shift(jnp.int32(digit), shift)
            best = jnp.where(keep(cand), cand, best)
        return best

    def bisect(n_bits, keep):
        lo = jnp.zeros((n_e, 1), jnp.int32)
        if n_bits % 2:
            lo = refine(lo, n_bits - 1, 1, keep)
        return lax.fori_loop(0, n_bits // 2, lambda i, v: refine(v, 2 * (n_bits // 2 - 1 - i), 2, keep), lo)

    thr = pltpu.bitcast(
        bisect(31, lambda cand: count(jnp.where(aff >= pltpu.bitcast(cand, F32), 1.0, 0.0)) >= capf), F32)
    gt = jnp.where(aff > thr, 1.0, 0.0)
    eq = jnp.where(aff == thr, 1.0, 0.0)
    need = capf - count(gt)
    idx = lax.broadcasted_iota(jnp.int32, (n_e, l), 1)
    idx_bits = int(l - 1).bit_length()

    last = bisect(idx_bits, lambda cand: count(jnp.where(idx < cand, eq, 0.0)) < need)
    bound = jnp.where(need > 0.0, last + 1, 0)
    sel = gt + jnp.where(idx < bound, eq, 0.0)

    blk = 128
    ti = lax.broadcasted_iota(jnp.int32, (blk, blk), 0)
    tj = lax.broadcasted_iota(jnp.int32, (blk, blk), 1)
    tri = jnp.where(ti < tj, 1.0, 0.0).astype(BF16)
    offset = jnp.zeros((n_e, 1), F32)
    for j in range(l // blk):
        sj = sel[:, j * blk:(j + 1) * blk]
        before = _dot(sj.astype(BF16), tri) + offset
        rank_ref[0, :, j * blk:(j + 1) * blk] = jnp.where(sj > 0.0, before, -1.0).astype(jnp.int32)
        offset = offset + jnp.sum(sj, axis=1, keepdims=True)

    rank = rank_ref[0]
    hi_digit = jnp.right_shift(rank, SLOT_SHIFT)
    lo_digit = jnp.bitwise_and(rank, SLOT_RADIX - 1)
    digit = lax.broadcasted_iota(jnp.int32, (SLOT_RADIX, l), 0)
    tok_hi = jnp.right_shift(idx[:1], TOKEN_SHIFT).astype(F32)
    tok_lo = jnp.bitwise_and(idx[:1], (1 << TOKEN_SHIFT) - 1).astype(F32)
    a_hi, a_lo, b_rows = [], [], []
    for e in range(n_e):
        is_a = hi_digit[e:e + 1] == digit
        a_hi.append(jnp.where(is_a, tok_hi, 0.0))
        a_lo.append(jnp.where(is_a, tok_lo, 0.0))
        b_rows.append(jnp.where(lo_digit[e:e + 1] == digit, 1.0, 0.0))
    b_all = jnp.concatenate(b_rows, axis=0).astype(BF16)
    cross_hi = _dot_nt(jnp.concatenate(a_hi, axis=0).astype(BF16), b_all)
    cross_lo = _dot_nt(jnp.concatenate(a_lo, axis=0).astype(BF16), b_all)
    cross = cross_hi * float(1 << TOKEN_SHIFT) + cross_lo + (pl.program_id(0) * l).astype(F32)
    for e in range(n_e):
        s0 = e * SLOT_RADIX
        idx_ref[0, e] = cross[s0:s0 + SLOT_RADIX, s0:s0 + SLOT_RADIX].astype(jnp.int32)


def _route(lt, cap):
    b, n_e, l = lt.shape
    assert cap == SLOT_RADIX * SLOT_RADIX and l <= (1 << TOKEN_SHIFT) * 256
    spec = pl.BlockSpec((1, n_e, l), lambda i: (i, 0, 0))
    return pl.pallas_call(
        functools.partial(_route_kernel, cap),
        grid=(b,),
        in_specs=[spec],
        out_specs=(spec, spec, pl.BlockSpec((1, n_e, SLOT_RADIX, SLOT_RADIX), lambda i: (i, 0, 0, 0))),
        out_shape=(jax.ShapeDtypeStruct((b, n_e, l), jnp.int32),
                   jax.ShapeDtypeStruct((b, n_e, l), F32),
                   jax.ShapeDtypeStruct((b, n_e, SLOT_RADIX, SLOT_RADIX), jnp.int32)),
        compiler_params=_params(("parallel",)),
        name="route_topc",
    )(lt)


def _sc_gather(rows, idx):
    n = idx.shape[0]
    w = rows.shape[1]
    mesh = plsc.VectorSubcoreMesh(core_axis_name="c", subcore_axis_name="s")

    @pl.kernel(out_type=jax.ShapeDtypeStruct((n, w), rows.dtype), mesh=mesh, scratch_types=[])
    def gather(rows_hbm, idx_hbm, out_hbm):
        def window(idx_vmem, out_vmem):
            pltpu.sync_copy(rows_hbm.at[idx_vmem.at[0]], out_vmem)

        pltpu.emit_pipeline(
            window,
            grid=(n // SC_WINDOW,),
            in_specs=[pl.BlockSpec((1, SC_WINDOW), index_map=lambda i: (0, i))],
            out_specs=[pl.BlockSpec((SC_WINDOW, w), index_map=lambda i: (i, 0))],
            core_axis_name=("c", "s"),
            dimension_semantics=(pltpu.PARALLEL,),
        )(idx_hbm, out_hbm)

    return gather(rows, idx.reshape(1, n))


def _ffn_kernel(xe_ref, wg_ref, wu_ref, wd_ref, ye_ref):
    halves = []
    for p in range(PACK_PARTS):
        halves.extend(_unpack_pair(xe_ref[p, 0]))
    xe = jnp.concatenate(halves, axis=1)
    he = (_silu(_dot(xe, wg_ref[0])) * _dot(xe, wu_ref[0])).astype(BF16)
    ye_ref[0] = _dot(he, wd_ref[0]).astype(BF16)


def _ffn(xe, wgate, wup, wdown):
    _, n_e, rows, w = xe.shape
    _, d, f = wgate.shape
    tile = min(FFN_ROWS, rows)
    return pl.pallas_call(
        _ffn_kernel,
        grid=(n_e, rows // tile),
        in_specs=[pl.BlockSpec((PACK_PARTS, 1, tile, w), lambda e, m: (0, e, m, 0)),
                  pl.BlockSpec((1, d, f), lambda e, m: (e, 0, 0)),
                  pl.BlockSpec((1, d, f), lambda e, m: (e, 0, 0)),
                  pl.BlockSpec((1, f, d), lambda e, m: (e, 0, 0))],
        out_specs=pl.BlockSpec((1, tile, d), lambda e, m: (e, m, 0)),
        out_shape=jax.ShapeDtypeStruct((n_e, rows, d), BF16),
        compiler_params=_params(("parallel", "arbitrary")),
        name="expert_ffn",
    )(xe, wgate, wup, wdown)


def _combine_kernel(cap, alpha, rank_ref, aff_ref, ye_ref, x1_ref, g2_ref, lng_ref, lnb_ref, *rest):
    o_ref = rest[-1]
    n_e = rank_ref.shape[1]
    slot = lax.broadcasted_iota(jnp.int32, (cap, OUT_ROWS), 0)
    acc = None
    for e in range(n_e):
        hit = slot == rank_ref[0, e:e + 1, :]
        gated = jnp.where(hit, aff_ref[0, e:e + 1, :], 0.0).astype(BF16)
        part = _dot_tn(gated, ye_ref[e, 0])
        acc = part if acc is None else acc + part
    z = alpha * x1_ref[0] + g2_ref[0] * acc
    o_ref[0] = _norm(z) * lng_ref[...] + lnb_ref[...]


def _combine(alpha, rank, aff, ye, x1, g2, ln_g, ln_b, cap, b_total, b0, earlier):
    b, l, d = x1.shape
    n_e = rank.shape[1]
    row = pl.BlockSpec((1, n_e, OUT_ROWS), lambda i, t: (i, 0, t))
    vec = pl.BlockSpec((1, d), lambda i, t: (0, 0))
    in_specs = [row, row,
                pl.BlockSpec((n_e, 1, cap, d), lambda i, t: (0, i, 0, 0)),
                pl.BlockSpec((1, OUT_ROWS, d), lambda i, t: (i, t, 0)),
                pl.BlockSpec((1, 1, d), lambda i, t: (i, 0, 0)), vec, vec]
    args = [rank, aff, ye.reshape(n_e, b, cap, d), x1, g2, ln_g, ln_b]
    aliases = {}
    if earlier is not None:
        in_specs.append(pl.BlockSpec(memory_space=pl.ANY))
        aliases = {len(args): 0}
        args.append(earlier)
    return pl.pallas_call(
        functools.partial(_combine_kernel, cap, alpha),
        grid=(b, l // OUT_ROWS),
        in_specs=in_specs,
        out_specs=pl.BlockSpec((1, OUT_ROWS, d), lambda i, t: (i + b0, t, 0)),
        out_shape=jax.ShapeDtypeStruct((b_total, l, d), F32),
        input_output_aliases=aliases,
        compiler_params=_params(("parallel", "arbitrary")),
        name="expert_combine",
    )(*args)


def _rope_tables(l):
    n_axis = RET_DK // 4
    freqs = ROPE_BASE ** (-np.arange(n_axis, dtype=np.float64) / n_axis)
    pos = np.arange(l)
    ang = np.concatenate([(pos // GRID_W)[:, None] * freqs, (pos % GRID_W)[:, None] * freqs], axis=-1)
    cos, sin = np.cos(ang), np.sin(ang)
    return (np.concatenate([cos, cos], axis=-1).astype(np.float32),
            np.concatenate([-sin, sin], axis=-1).astype(np.float32))


def kernel(x, c, ctx, c_ctx, w_ada, b_ada, w_in, conv_w, conv_b, conv_ln_g, conv_ln_b, w_conv_out,
           log_decay_f, log_decay_b, ret_gn_g, w_ret_out, w_out, ln1_g, ln1_b,
           w_router, w_gate, w_up, w_down, ln2_g, ln2_b):
    depth = w_ada.shape[0]
    assert depth == 1, "single trunk layer"
    b, l, d = x.shape
    alpha = (2.0 * depth) ** 0.25
    cap = EC_FACTOR * l // N_EXPERTS
    u_end = 2 * CONV_DIM
    q_end = u_end + QK_DIM
    k_end = q_end + QK_DIM
    v_end = k_end + V_DIM
    row = lambda a: a.reshape(1, -1)

    n_mod = b + 1
    pad = (-n_mod) % 8
    cc = jnp.concatenate([c, c_ctx[None], jnp.zeros((pad, d), F32)], axis=0)
    mod = _ada(cc, w_ada[0], row(b_ada[0]))
    sh1, sc1, g1, sh2, sc2, g2 = [m.reshape(b, 1, d) for m in jnp.split(mod[:b], 6, axis=-1)]
    csh1, csc1 = mod[b:b + 1, :d], mod[b:b + 1, d:2 * d]

    perm = np.concatenate([np.arange(0, RET_DK, 2), np.arange(1, RET_DK, 2)])
    perm = (np.arange(RET_HEADS)[:, None] * RET_DK + perm[None, :]).reshape(-1)
    w = w_in[0]
    w_kvu = jnp.concatenate([w[:, q_end:k_end][:, perm], w[:, k_end:v_end], w[:, :u_end]], axis=1).astype(BF16)
    wq = w[:, u_end:q_end][:, perm].astype(BF16)
    wg = w[:, v_end:].astype(BF16)

    cos_t, sin_t = _rope_tables(l)
    cq, sq = jnp.asarray(cos_t), jnp.asarray(sin_t)
    k_scale = RET_DK ** -0.5
    ck, sk = jnp.asarray(cos_t * k_scale), jnp.asarray(sin_t * k_scale)

    lgf, lgb = log_decay_f[0], log_decay_b[0]
    sf0, sb0 = _ctx_states(lgf, lgb, ctx, csh1, csc1, w_kvu)
    cw8 = jnp.repeat(conv_w[0], 8, axis=0)
    wco, wro, wo = w_conv_out[0].astype(BF16), w_ret_out[0].astype(BF16), w_out[0].astype(BF16)
    wge, wue, wde = w_gate[0].astype(BF16), w_up[0].astype(BF16), w_down[0].astype(BF16)
    wr_t = w_router[0].T
    groups = BATCH_GROUPS if b % BATCH_GROUPS == 0 else 1
    nb = b // groups
    out = None
    for gi in range(groups):
        b0 = gi * nb
        part = lambda a: a[b0:b0 + nb]
        h, k, v, ya, sf = _proj(lgf, x, b0, part(sh1), part(sc1), w_kvu, ck, sk, part(sf0),
                                cw8, row(conv_b[0]), row(conv_ln_g[0]), row(conv_ln_b[0]))
        x1, hp, lt = _mixer(alpha, lgf, lgb, x, b0, h, k, v, sf, part(sb0), ya, wq, wg, wco, wro, wo,
                            row(ret_gn_g[0]), part(g1), part(sh2), part(sc2), row(ln1_g[0]), row(ln1_b[0]),
                            wr_t, cq, sq)
        rank, aff, slots = _route(lt, cap)
        part_rows = nb * l
        by_expert = slots.reshape(nb, N_EXPERTS, cap).transpose(1, 0, 2).reshape(-1)
        ids = jnp.concatenate([by_expert + p * part_rows for p in range(PACK_PARTS)])
        xe = _sc_gather(hp.reshape(PACK_PARTS * part_rows, hp.shape[-1]), ids)
        ye = _ffn(xe.reshape(PACK_PARTS, N_EXPERTS, nb * cap, hp.shape[-1]), wge, wue, wde)
        out = _combine(alpha, rank, aff, ye, x1, part(g2), row(ln2_g[0]), row(ln2_b[0]), cap, b, b0, out)
    return out
```

```python
import functools

import numpy as np
import jax
import jax.numpy as jnp
from jax import lax
from jax.experimental import pallas as pl
from jax.experimental.pallas import tpu as pltpu
from jax.experimental.pallas import tpu_sc as plsc

F32 = jnp.float32
BF16 = jnp.bfloat16
U32 = jnp.uint32

GRID_W = 64
CONV_DIM = 512
CONV_WIDTH = 31
CONV_HALO = CONV_WIDTH // 2
RET_HEADS = 4
RET_DK = 128
RET_DV = 256
RET_CHUNK = 128
ROPE_BASE = 10000.0
QK_DIM = RET_HEADS * RET_DK
V_DIM = RET_HEADS * RET_DV
N_EXPERTS = 16
EC_FACTOR = 2
LN_EPS = 1e-5

ADA_COLS = 768
PROJ_ROWS = 512
MIX_ROWS = 512
BATCH_GROUPS = 2
CONV_PAD = 16
CONV_BLOCK = 64
MIX_CHUNK = 256
OUT_ROWS = 512
FFN_ROWS = 1024
PACK_PARTS = 2
SC_WINDOW = 128
SLOT_SHIFT = 4
SLOT_RADIX = 1 << SLOT_SHIFT
TOKEN_SHIFT = 6
V7X_VMEM_LIMIT = 56 * 1024 * 1024


def _dot(a, b):
    return jnp.dot(a, b, preferred_element_type=F32)


def _dot_nt(a, b):
    return lax.dot_general(a, b, (((1,), (1,)), ((), ())), preferred_element_type=F32)


def _dot_tn(a, b):
    return lax.dot_general(a, b, (((0,), (0,)), ((), ())), preferred_element_type=F32)


def _split(a):
    hi = a.astype(BF16)
    lo = (a - hi.astype(F32)).astype(BF16)
    return hi, lo


def _norm(x):
    mu = jnp.mean(x, axis=-1, keepdims=True)
    xc = x - mu
    var = jnp.mean(xc * xc, axis=-1, keepdims=True)
    return xc * lax.rsqrt(var + LN_EPS)


def _sigmoid(x):
    return 1.0 / (1.0 + jnp.exp(-x))


def _silu(x):
    return x * _sigmoid(x)


def _col_iota(n):
    return lax.broadcasted_iota(jnp.int32, (n, 1), 0).astype(F32)


def _pack_pair(lo, hi):
    lo_bits = pltpu.bitcast(lo.astype(F32), U32)
    hi_bits = pltpu.bitcast(hi.astype(F32), U32)
    return lax.shift_right_logical(lo_bits, jnp.uint32(16)) | hi_bits


def _unpack_pair(word):
    lo = pltpu.bitcast(lax.shift_left(word, jnp.uint32(16)), F32)
    hi = pltpu.bitcast(word & jnp.uint32(0xFFFF0000), F32)
    return lo.astype(BF16), hi.astype(BF16)


def _smem():
    return pl.BlockSpec(memory_space=pltpu.SMEM)


def _params(sem, vmem=V7X_VMEM_LIMIT):
    return pltpu.CompilerParams(dimension_semantics=sem, vmem_limit_bytes=vmem)


def _ada_kernel(c_ref, w_ref, b_ref, o_ref):
    a_hi, a_lo = _split(_silu(c_ref[...]))
    w_hi, w_lo = _split(w_ref[...])
    o_ref[...] = _dot(a_hi, w_hi) + (_dot(a_hi, w_lo) + _dot(a_lo, w_hi)) + b_ref[...]


def _ada(cc, w, b):
    m, d = cc.shape
    n = w.shape[1]
    return pl.pallas_call(
        _ada_kernel,
        grid=(n // ADA_COLS,),
        in_specs=[pl.BlockSpec((m, d), lambda j: (0, 0)),
                  pl.BlockSpec((d, ADA_COLS), lambda j: (0, j)),
                  pl.BlockSpec((1, ADA_COLS), lambda j: (0, j))],
        out_specs=pl.BlockSpec((m, ADA_COLS), lambda j: (0, j)),
        out_shape=jax.ShapeDtypeStruct((m, n), F32),
        compiler_params=_params(("parallel",)),
        name="ada_proj",
    )(cc, w, b)


def _ctx_kernel(lgf_ref, lgb_ref, ctx_ref, sh_ref, sc_ref, w_ref, sf_ref, sb_ref):
    x = ctx_ref[0]
    lc = x.shape[0]
    h = (_norm(x) * (1.0 + sc_ref[...]) + sh_ref[...]).astype(BF16)
    kv = _dot(h, w_ref[...])
    t = _col_iota(lc)
    for hh in range(RET_HEADS):
        k = kv[:, hh * RET_DK:(hh + 1) * RET_DK] * (RET_DK ** -0.5)
        v = kv[:, QK_DIM + hh * RET_DV:QK_DIM + (hh + 1) * RET_DV].astype(BF16)
        wf = jnp.exp(lgf_ref[hh] * (lc - 1.0 - t))
        wb = jnp.exp(lgb_ref[hh] * t)
        sf_ref[0, hh] = _dot_tn((k * wf).astype(BF16), v)
        sb_ref[0, hh] = _dot_tn((k * wb).astype(BF16), v)


def _ctx_states(lgf, lgb, ctx, csh, csc, w_kvu):
    b, lc, d = ctx.shape
    kvw = QK_DIM + V_DIM
    st = jax.ShapeDtypeStruct((b, RET_HEADS, RET_DK, RET_DV), F32)
    st_spec = pl.BlockSpec((1, RET_HEADS, RET_DK, RET_DV), lambda i: (i, 0, 0, 0))
    return pl.pallas_call(
        _ctx_kernel,
        grid=(b,),
        in_specs=[_smem(), _smem(),
                  pl.BlockSpec((1, lc, d), lambda i: (i, 0, 0)),
                  pl.BlockSpec((1, d), lambda i: (0, 0)),
                  pl.BlockSpec((1, d), lambda i: (0, 0)),
                  pl.BlockSpec((d, kvw), lambda i: (0, 0))],
        out_specs=(st_spec, st_spec),
        out_shape=(st, st),
        compiler_params=_params(("parallel",)),
        name="ctx_states",
    )(lgf, lgb, ctx, csh, csc, w_kvu)


def _proj_kernel(lgf_ref, xp_ref, x_ref, xn_ref, sh_ref, sc_ref, w_ref, ck_ref, sk_ref, sf0_ref,
                 cw_ref, cb_ref, clg_ref, clb_ref,
                 h_ref, k_ref, v_ref, ya_ref, sf_ref, s_scr, y_scr, yres_scr, z_scr):
    t = pl.program_id(1)
    nt = pl.num_programs(1)

    @pl.when(t == 0)
    def _():
        s_scr[...] = sf0_ref[0]

    rows = x_ref.shape[1]
    ext = rows + 2 * CONV_PAD
    x_ext = jnp.concatenate([xp_ref[0], x_ref[0], xn_ref[0]], axis=0)
    h_ext = (_norm(x_ext) * (1.0 + sc_ref[0]) + sh_ref[0]).astype(BF16)
    h = h_ext[CONV_PAD:CONV_PAD + rows]
    h_ref[0] = h
    kvw = QK_DIM + V_DIM
    u = _dot(h_ext, w_ref[:, kvw:])
    row = lax.broadcasted_iota(jnp.int32, (ext, 1), 0)
    head_ok = jnp.where(t == 0, 0.0, 1.0)
    tail_ok = jnp.where(t == nt - 1, 0.0, 1.0)
    inside = jnp.where(row < CONV_PAD, head_ok, jnp.where(row >= CONV_PAD + rows, tail_ok, 1.0))
    y_scr[...] = u[:, :CONV_DIM] * _sigmoid(u[:, CONV_DIM:]) * inside

    kk = _dot(h, w_ref[:, :QK_DIM])
    vv = _dot(h, w_ref[:, QK_DIM:kvw]).astype(BF16)
    v_ref[0] = vv
    ck = ck_ref[...]
    sk = sk_ref[...]
    pos = _col_iota(RET_CHUNK)
    for hh in range(RET_HEADS):
        lg = lgf_ref[hh]
        kh = kk[:, hh * RET_DK:(hh + 1) * RET_DK]
        kr = kh * ck + pltpu.roll(kh, RET_DK // 2, 1) * sk
        k_ref[0, :, hh * RET_DK:(hh + 1) * RET_DK] = kr.astype(BF16)
        state_dec = jnp.exp(lg * (RET_CHUNK - 1.0 - pos))
        chunk_dec = jnp.exp(lg * jnp.full((1, RET_DV), float(RET_CHUNK), F32))
        for c in range(rows // RET_CHUNK):
            r0 = c * RET_CHUNK
            s = s_scr[hh]
            sf_ref[0, c, hh] = s.astype(BF16)
            kc = (kr[r0:r0 + RET_CHUNK] * state_dec).astype(BF16)
            vc = vv[r0:r0 + RET_CHUNK, hh * RET_DV:(hh + 1) * RET_DV]
            s_scr[hh] = s * chunk_dec + _dot_tn(kc, vc)

    span = ext - 8
    for r in range(8):
        yres_scr[r] = y_scr[pl.ds(r, span), :]

    for i in range(rows // CONV_BLOCK):
        base = i * CONV_BLOCK
        acc = None
        for w in range(CONV_WIDTH):
            a, r = divmod(w + 1, 8)
            wk = jnp.concatenate([cw_ref[8 * w:8 * w + 8, :]] * (CONV_BLOCK // 8), axis=0)
            tap = yres_scr[r, base + 8 * a:base + 8 * a + CONV_BLOCK, :] * wk
            acc = tap if acc is None else acc + tap
        z_scr[base:base + CONV_BLOCK, :] = acc
    z = _norm(z_scr[...] + cb_ref[...]) * clg_ref[...] + clb_ref[...]
    ya_ref[0] = _silu(z).astype(BF16)


def _proj(lgf, x, b0, sh1, sc1, w_kvu, ck, sk, sf0, cw, cb, clg, clb):
    _, l, d = x.shape
    b = sh1.shape[0]
    nt = l // PROJ_ROWS
    cpt = PROJ_ROWS // RET_CHUNK
    hpt = PROJ_ROWS // CONV_PAD
    n_halo = l // CONV_PAD
    ext = PROJ_ROWS + 2 * CONV_PAD
    tile = lambda w: pl.BlockSpec((1, PROJ_ROWS, w), lambda i, t: (i, t, 0))
    vec = pl.BlockSpec((1, 1, d), lambda i, t: (i, 0, 0))
    full = lambda a: pl.BlockSpec(a.shape, lambda i, t: (0,) * a.ndim, pipeline_mode=pl.Buffered(1))
    rope = pl.BlockSpec((PROJ_ROWS, RET_DK), lambda i, t: (t, 0))
    x_tile = pl.BlockSpec((1, PROJ_ROWS, d), lambda i, t: (i + b0, t, 0))
    prev = pl.BlockSpec((1, CONV_PAD, d), lambda i, t: (i + b0, jnp.maximum(t * hpt - 1, 0), 0))
    nxt = pl.BlockSpec((1, CONV_PAD, d), lambda i, t: (i + b0, jnp.minimum((t + 1) * hpt, n_halo - 1), 0))
    return pl.pallas_call(
        _proj_kernel,
        grid=(b, nt),
        in_specs=[_smem(), prev, x_tile, nxt, vec, vec, full(w_kvu), rope, rope,
                  pl.BlockSpec((1, RET_HEADS, RET_DK, RET_DV), lambda i, t: (i, 0, 0, 0)),
                  full(cw), full(cb), full(clg), full(clb)],
        out_specs=(tile(d), tile(QK_DIM), tile(V_DIM), tile(CONV_DIM),
                   pl.BlockSpec((1, cpt, RET_HEADS, RET_DK, RET_DV), lambda i, t: (i, t, 0, 0, 0))),
        out_shape=(jax.ShapeDtypeStruct((b, l, d), BF16),
                   jax.ShapeDtypeStruct((b, l, QK_DIM), BF16),
                   jax.ShapeDtypeStruct((b, l, V_DIM), BF16),
                   jax.ShapeDtypeStruct((b, l, CONV_DIM), BF16),
                   jax.ShapeDtypeStruct((b, l // RET_CHUNK, RET_HEADS, RET_DK, RET_DV), BF16)),
        scratch_shapes=[pltpu.VMEM((RET_HEADS, RET_DK, RET_DV), F32),
                        pltpu.VMEM((ext, CONV_DIM), F32),
                        pltpu.VMEM((8, ext - 8, CONV_DIM), F32),
                        pltpu.VMEM((PROJ_ROWS, CONV_DIM), F32)],
        compiler_params=_params(("parallel", "arbitrary")),
        name="proj_kv_conv",
    )(lgf, x, x, x, sh1, sc1, w_kvu, ck, sk, sf0, cw, cb, clg, clb)


def _mixer_kernel(alpha, lgf_ref, lgb_ref, x_ref, h_ref, k_ref, v_ref, sf_ref, sb0_ref, ya_ref,
                  wq_ref, wg_ref, wco_ref, wro_ref, wo_ref, gng_ref,
                  g1_ref, sh2_ref, sc2_ref, l1g_ref, l1b_ref, wr_ref, cq_ref, sq_ref,
                  x1_ref, hp_ref, lt_ref, sb_scr):
    t = pl.program_id(1)

    @pl.when(t == 0)
    def _():
        sb_scr[...] = sb0_ref[0]

    h = h_ref[0]
    rows = h.shape[0]
    cq = cq_ref[...]
    sq = sq_ref[...]
    q_all = _dot(h, wq_ref[...])
    chunk = MIX_CHUNK
    sf_step = chunk // RET_CHUNK
    pos = _col_iota(chunk)
    di = lax.broadcasted_iota(jnp.int32, (chunk, chunk), 0)
    dj = lax.broadcasted_iota(jnp.int32, (chunk, chunk), 1)
    dist = (di - dj).astype(F32)

    n_chunks = rows // chunk
    o_parts = [[None] * RET_HEADS for _ in range(n_chunks)]
    for hh in range(RET_HEADS):
        lgf = lgf_ref[hh]
        lgb = lgb_ref[hh]
        qh = q_all[:, hh * RET_DK:(hh + 1) * RET_DK]
        qr = qh * cq + pltpu.roll(qh, RET_DK // 2, 1) * sq
        decay = jnp.where(dist >= 0.0, jnp.exp(lgf * jnp.maximum(dist, 0.0)),
                          jnp.exp(lgb * jnp.maximum(-dist, 0.0)))
        cross_f = jnp.exp(lgf * (pos + 1.0))
        cross_b = jnp.exp(lgb * (chunk - pos))
        state_b = jnp.exp(lgb * pos)
        chunk_b = jnp.exp(lgb * jnp.full((1, RET_DV), float(chunk), F32))
        for c in reversed(range(n_chunks)):
            r0 = c * chunk
            qc = qr[r0:r0 + chunk]
            kc = k_ref[0, r0:r0 + chunk, hh * RET_DK:(hh + 1) * RET_DK]
            vc = v_ref[0, r0:r0 + chunk, hh * RET_DV:(hh + 1) * RET_DV]
            sb = sb_scr[hh]
            att = _dot_nt(qc.astype(BF16), kc) * decay
            lhs = jnp.concatenate([att.astype(BF16), (qc * cross_f).astype(BF16), (qc * cross_b).astype(BF16)], axis=1)
            rhs = jnp.concatenate([vc, sf_ref[0, c * sf_step, hh], sb.astype(BF16)], axis=0)
            o = _dot(lhs, rhs)
            kb = (kc.astype(F32) * state_b).astype(BF16)
            sb_scr[hh] = sb * chunk_b + _dot_tn(kb, vc)
            o_parts[c][hh] = _norm(o)
    on = jnp.concatenate([jnp.concatenate(o_parts[c], axis=1) for c in range(n_chunks)], axis=0)

    d = x_ref.shape[-1]
    pw = d // (2 * PACK_PARTS)
    wr_hi, wr_lo = _split(wr_ref[...])
    g_ret = _dot(h, wg_ref[:, :V_DIM])
    yb_in = (_silu(g_ret) * (on * gng_ref[...])).astype(BF16)
    y_b = _dot(yb_in, wro_ref[...])
    y_a = _dot(ya_ref[0], wco_ref[...])
    g_a = _dot(h, wg_ref[:, V_DIM:V_DIM + d])
    g_b = _dot(h, wg_ref[:, V_DIM + d:])
    y = (_sigmoid(g_a) * y_a + _sigmoid(g_b) * y_b).astype(BF16)
    ym = _dot(y, wo_ref[...])

    x1 = _norm(alpha * x_ref[0] + g1_ref[0] * ym) * l1g_ref[...] + l1b_ref[...]
    x1_ref[0] = x1
    h2 = _norm(x1) * (1.0 + sc2_ref[0]) + sh2_ref[0]
    h2_hi, h2_lo = _split(h2)
    for p in range(PACK_PARTS):
        c0 = 2 * p * pw
        hp_ref[p, 0] = _pack_pair(h2_hi[:, c0:c0 + pw], h2_hi[:, c0 + pw:c0 + 2 * pw])
    lt_ref[0] = _dot_nt(wr_hi, h2_hi) + (_dot_nt(wr_hi, h2_lo) + _dot_nt(wr_lo, h2_hi))


def _mixer(alpha, lgf, lgb, x, b0, h, k, v, sf, sb0, ya, wq, wg, wco, wro, wo, gng,
           g1, sh2, sc2, l1g, l1b, wr_t, cq, sq):
    b, l, d = h.shape
    nt = l // MIX_ROWS
    cpt = MIX_ROWS // RET_CHUNK
    pw = d // (2 * PACK_PARTS)
    rev = lambda w: pl.BlockSpec((1, MIX_ROWS, w), lambda i, t: (i, nt - 1 - t, 0))
    vec = pl.BlockSpec((1, 1, d), lambda i, t: (i, 0, 0))
    full = lambda a: pl.BlockSpec(a.shape, lambda i, t: (0,) * a.ndim, pipeline_mode=pl.Buffered(1))
    rope = pl.BlockSpec((MIX_ROWS, RET_DK), lambda i, t: (nt - 1 - t, 0))
    state = pl.BlockSpec((1, RET_HEADS, RET_DK, RET_DV), lambda i, t: (i, 0, 0, 0))
    return pl.pallas_call(
        functools.partial(_mixer_kernel, alpha),
        grid=(b, nt),
        in_specs=[_smem(), _smem(),
                  pl.BlockSpec((1, MIX_ROWS, d), lambda i, t: (i + b0, nt - 1 - t, 0)),
                  rev(d), rev(QK_DIM), rev(V_DIM),
                  pl.BlockSpec((1, cpt, RET_HEADS, RET_DK, RET_DV), lambda i, t: (i, nt - 1 - t, 0, 0, 0)),
                  state, rev(CONV_DIM),
                  full(wq), full(wg), full(wco), full(wro), full(wo), full(gng),
                  vec, vec, vec, full(l1g), full(l1b), full(wr_t), rope, rope],
        out_specs=(rev(d),
                   pl.BlockSpec((PACK_PARTS, 1, MIX_ROWS, pw), lambda i, t: (0, i, nt - 1 - t, 0)),
                   pl.BlockSpec((1, N_EXPERTS, MIX_ROWS), lambda i, t: (i, 0, nt - 1 - t))),
        out_shape=(jax.ShapeDtypeStruct((b, l, d), F32),
                   jax.ShapeDtypeStruct((PACK_PARTS, b, l, pw), U32),
                   jax.ShapeDtypeStruct((b, N_EXPERTS, l), F32)),
        scratch_shapes=[pltpu.VMEM((RET_HEADS, RET_DK, RET_DV), F32)],
        compiler_params=_params(("parallel", "arbitrary")),
        name="mixer",
    )(lgf, lgb, x, h, k, v, sf, sb0, ya, wq, wg, wco, wro, wo, gng,
      g1, sh2, sc2, l1g, l1b, wr_t, cq, sq)


def _route_kernel(cap, lt_ref, rank_ref, aff_ref, idx_ref):
    logits = lt_ref[0]
    n_e, l = logits.shape
    m = jnp.max(logits, axis=0, keepdims=True)
    p = jnp.exp(logits - m)
    aff = p / jnp.sum(p, axis=0, keepdims=True)
    aff_ref[0] = aff
    capf = float(cap)

    def count(ones):
        return jnp.sum(ones, axis=1, keepdims=True)

    def refine(lo, shift, width, keep):
        best = lo
        for digit in range(1, 1 << width):
            cand = lo | jnp.left_shift(jnp.int32(digit), shift)
            best = jnp.where(keep(cand), cand, best)
        return best

    def bisect(n_bits, keep):
        lo = jnp.zeros((n_e, 1), jnp.int32)
        if n_bits % 2:
            lo = refine(lo, n_bits - 1, 1, keep)
        return lax.fori_loop(0, n_bits // 2, lambda i, v: refine(v, 2 * (n_bits // 2 - 1 - i), 2, keep), lo)

    thr = pltpu.bitcast(
        bisect(31, lambda cand: count(jnp.where(aff >= pltpu.bitcast(cand, F32), 1.0, 0.0)) >= capf), F32)
    gt = jnp.where(aff > thr, 1.0, 0.0)
    eq = jnp.where(aff == thr, 1.0, 0.0)
    need = capf - count(gt)
    idx = lax.broadcasted_iota(jnp.int32, (n_e, l), 1)
    idx_bits = int(l - 1).bit_length()

    last = bisect(idx_bits, lambda cand: count(jnp.where(idx < cand, eq, 0.0)) < need)
    bound = jnp.where(need > 0.0, last + 1, 0)
    sel = gt + jnp.where(idx < bound, eq, 0.0)

    blk = 128
    ti = lax.broadcasted_iota(jnp.int32, (blk, blk), 0)
    tj = lax.broadcasted_iota(jnp.int32, (blk, blk), 1)
    tri = jnp.where(ti < tj, 1.0, 0.0).astype(BF16)
    offset = jnp.zeros((n_e, 1), F32)
    for j in range(l // blk):
        sj = sel[:, j * blk:(j + 1) * blk]
        before = _dot(sj.astype(BF16), tri) + offset
        rank_ref[0, :, j * blk:(j + 1) * blk] = jnp.where(sj > 0.0, before, -1.0).astype(jnp.int32)
        offset = offset + jnp.sum(sj, axis=1, keepdims=True)

    rank = rank_ref[0]
    hi_digit = jnp.right_shift(rank, SLOT_SHIFT)
    lo_digit = jnp.bitwise_and(rank, SLOT_RADIX - 1)
    digit = lax.broadcasted_iota(jnp.int32, (SLOT_RADIX, l), 0)
    tok_hi = jnp.right_shift(idx[:1], TOKEN_SHIFT).astype(F32)
    tok_lo = jnp.bitwise_and(idx[:1], (1 << TOKEN_SHIFT) - 1).astype(F32)
    a_hi, a_lo, b_rows = [], [], []
    for e in range(n_e):
        is_a = hi_digit[e:e + 1] == digit
        a_hi.append(jnp.where(is_a, tok_hi, 0.0))
        a_lo.append(jnp.where(is_a, tok_lo, 0.0))
        b_rows.append(jnp.where(lo_digit[e:e + 1] == digit, 1.0, 0.0))
    b_all = jnp.concatenate(b_rows, axis=0).astype(BF16)
    cross_hi = _dot_nt(jnp.concatenate(a_hi, axis=0).astype(BF16), b_all)
    cross_lo = _dot_nt(jnp.concatenate(a_lo, axis=0).astype(BF16), b_all)
    cross = cross_hi * float(1 << TOKEN_SHIFT) + cross_lo + (pl.program_id(0) * l).astype(F32)
    for e in range(n_e):
        s0 = e * SLOT_RADIX
        idx_ref[0, e] = cross[s0:s0 + SLOT_RADIX, s0:s0 + SLOT_RADIX].astype(jnp.int32)


def _route(lt, cap):
    b, n_e, l = lt.shape
    assert cap == SLOT_RADIX * SLOT_RADIX and l <= (1 << TOKEN_SHIFT) * 256
    spec = pl.BlockSpec((1, n_e, l), lambda i: (i, 0, 0))
    return pl.pallas_call(
        functools.partial(_route_kernel, cap),
        grid=(b,),
        in_specs=[spec],
        out_specs=(spec, spec, pl.BlockSpec((1, n_e, SLOT_RADIX, SLOT_RADIX), lambda i: (i, 0, 0, 0))),
        out_shape=(jax.ShapeDtypeStruct((b, n_e, l), jnp.int32),
                   jax.ShapeDtypeStruct((b, n_e, l), F32),
                   jax.ShapeDtypeStruct((b, n_e, SLOT_RADIX, SLOT_RADIX), jnp.int32)),
        compiler_params=_params(("parallel",)),
        name="route_topc",
    )(lt)


def _sc_gather(rows, idx):
    n = idx.shape[0]
    w = rows.shape[1]
    mesh = plsc.VectorSubcoreMesh(core_axis_name="c", subcore_axis_name="s")

    @pl.kernel(out_type=jax.ShapeDtypeStruct((n, w), rows.dtype), mesh=mesh, scratch_types=[])
    def gather(rows_hbm, idx_hbm, out_hbm):
        def window(idx_vmem, out_vmem):
            pltpu.sync_copy(rows_hbm.at[idx_vmem.at[0]], out_vmem)

        pltpu.emit_pipeline(
            window,
            grid=(n // SC_WINDOW,),
            in_specs=[pl.BlockSpec((1, SC_WINDOW), index_map=lambda i: (0, i))],
            out_specs=[pl.BlockSpec((SC_WINDOW, w), index_map=lambda i: (i, 0))],
            core_axis_name=("c", "s"),
            dimension_semantics=(pltpu.PARALLEL,),
        )(idx_hbm, out_hbm)

    return gather(rows, idx.reshape(1, n))


def _ffn_kernel(xe_ref, wg_ref, wu_ref, wd_ref, ye_ref, wg_scr, wu_scr, wd_scr):
    @pl.when(pl.program_id(1) == 0)
    def _():
        wg_scr[...] = wg_ref[0].astype(BF16)
        wu_scr[...] = wu_ref[0].astype(BF16)
        wd_scr[...] = wd_ref[0].astype(BF16)

    halves = []
    for p in range(PACK_PARTS):
        halves.extend(_unpack_pair(xe_ref[p, 0]))
    xe = jnp.concatenate(halves, axis=1)
    he = (_silu(_dot(xe, wg_scr[...])) * _dot(xe, wu_scr[...])).astype(BF16)
    ye_ref[0] = _dot(he, wd_scr[...]).astype(BF16)


def _ffn(xe, wgate, wup, wdown):
    _, n_e, rows, w = xe.shape
    _, d, f = wgate.shape
    tile = min(FFN_ROWS, rows)
    return pl.pallas_call(
        _ffn_kernel,
        grid=(n_e, rows // tile),
        in_specs=[pl.BlockSpec((PACK_PARTS, 1, tile, w), lambda e, m: (0, e, m, 0)),
                  pl.BlockSpec((1, d, f), lambda e, m: (e, 0, 0)),
                  pl.BlockSpec((1, d, f), lambda e, m: (e, 0, 0)),
                  pl.BlockSpec((1, f, d), lambda e, m: (e, 0, 0))],
        out_specs=pl.BlockSpec((1, tile, d), lambda e, m: (e, m, 0)),
        out_shape=jax.ShapeDtypeStruct((n_e, rows, d), BF16),
        scratch_shapes=[pltpu.VMEM((d, f), BF16), pltpu.VMEM((d, f), BF16), pltpu.VMEM((f, d), BF16)],
        compiler_params=_params(("parallel", "arbitrary")),
        name="expert_ffn",
    )(xe, wgate, wup, wdown)


def _combine_kernel(cap, alpha, rank_ref, aff_ref, ye_ref, x1_ref, g2_ref, lng_ref, lnb_ref, *rest):
    o_ref = rest[-1]
    n_e = rank_ref.shape[1]
    slot = lax.broadcasted_iota(jnp.int32, (cap, OUT_ROWS), 0)
    acc = None
    for e in range(n_e):
        hit = slot == rank_ref[0, e:e + 1, :]
        gated = jnp.where(hit, aff_ref[0, e:e + 1, :], 0.0).astype(BF16)
        part = _dot_tn(gated, ye_ref[e, 0])
        acc = part if acc is None else acc + part
    z = alpha * x1_ref[0] + g2_ref[0] * acc
    o_ref[0] = _norm(z) * lng_ref[...] + lnb_ref[...]


def _combine(alpha, rank, aff, ye, x1, g2, ln_g, ln_b, cap, b_total, b0, earlier):
    b, l, d = x1.shape
    n_e = rank.shape[1]
    row = pl.BlockSpec((1, n_e, OUT_ROWS), lambda i, t: (i, 0, t))
    vec = pl.BlockSpec((1, d), lambda i, t: (0, 0))
    in_specs = [row, row,
                pl.BlockSpec((n_e, 1, cap, d), lambda i, t: (0, i, 0, 0)),
                pl.BlockSpec((1, OUT_ROWS, d), lambda i, t: (i, t, 0)),
                pl.BlockSpec((1, 1, d), lambda i, t: (i, 0, 0)), vec, vec]
    args = [rank, aff, ye.reshape(n_e, b, cap, d), x1, g2, ln_g, ln_b]
    aliases = {}
    if earlier is not None:
        in_specs.append(pl.BlockSpec(memory_space=pl.ANY))
        aliases = {len(args): 0}
        args.append(earlier)
    return pl.pallas_call(
        functools.partial(_combine_kernel, cap, alpha),
        grid=(b, l // OUT_ROWS),
        in_specs=in_specs,
        out_specs=pl.BlockSpec((1, OUT_ROWS, d), lambda i, t: (i + b0, t, 0)),
        out_shape=jax.ShapeDtypeStruct((b_total, l, d), F32),
        input_output_aliases=aliases,
        compiler_params=_params(("parallel", "arbitrary")),
        name="expert_combine",
    )(*args)


def _rope_tables(l):
    n_axis = RET_DK // 4
    freqs = ROPE_BASE ** (-np.arange(n_axis, dtype=np.float64) / n_axis)
    pos = np.arange(l)
    ang = np.concatenate([(pos // GRID_W)[:, None] * freqs, (pos % GRID_W)[:, None] * freqs], axis=-1)
    cos, sin = np.cos(ang), np.sin(ang)
    return (np.concatenate([cos, cos], axis=-1).astype(np.float32),
            np.concatenate([-sin, sin], axis=-1).astype(np.float32))


def kernel(x, c, ctx, c_ctx, w_ada, b_ada, w_in, conv_w, conv_b, conv_ln_g, conv_ln_b, w_conv_out,
           log_decay_f, log_decay_b, ret_gn_g, w_ret_out, w_out, ln1_g, ln1_b,
           w_router, w_gate, w_up, w_down, ln2_g, ln2_b):
    depth = w_ada.shape[0]
    assert depth == 1, "single trunk layer"
    b, l, d = x.shape
    alpha = (2.0 * depth) ** 0.25
    cap = EC_FACTOR * l // N_EXPERTS
    u_end = 2 * CONV_DIM
    q_end = u_end + QK_DIM
    k_end = q_end + QK_DIM
    v_end = k_end + V_DIM
    row = lambda a: a.reshape(1, -1)

    n_mod = b + 1
    pad = (-n_mod) % 8
    cc = jnp.concatenate([c, c_ctx[None], jnp.zeros((pad, d), F32)], axis=0)
    mod = _ada(cc, w_ada[0], row(b_ada[0]))
    sh1, sc1, g1, sh2, sc2, g2 = [m.reshape(b, 1, d) for m in jnp.split(mod[:b], 6, axis=-1)]
    csh1, csc1 = mod[b:b + 1, :d], mod[b:b + 1, d:2 * d]

    perm = np.concatenate([np.arange(0, RET_DK, 2), np.arange(1, RET_DK, 2)])
    perm = (np.arange(RET_HEADS)[:, None] * RET_DK + perm[None, :]).reshape(-1)
    w = w_in[0]
    w_kvu = jnp.concatenate([w[:, q_end:k_end][:, perm], w[:, k_end:v_end], w[:, :u_end]], axis=1).astype(BF16)
    wq = w[:, u_end:q_end][:, perm].astype(BF16)
    wg = w[:, v_end:].astype(BF16)

    cos_t, sin_t = _rope_tables(l)
    cq, sq = jnp.asarray(cos_t), jnp.asarray(sin_t)
    k_scale = RET_DK ** -0.5
    ck, sk = jnp.asarray(cos_t * k_scale), jnp.asarray(sin_t * k_scale)

    lgf, lgb = log_decay_f[0], log_decay_b[0]
    sf0, sb0 = _ctx_states(lgf, lgb, ctx, csh1, csc1, w_kvu)
    cw8 = jnp.repeat(conv_w[0], 8, axis=0)
    wco, wro, wo = w_conv_out[0].astype(BF16), w_ret_out[0].astype(BF16), w_out[0].astype(BF16)
    wge, wue, wde = w_gate[0], w_up[0], w_down[0]
    wr_t = w_router[0].T
    groups = BATCH_GROUPS if b % BATCH_GROUPS == 0 else 1
    nb = b // groups
    out = None
    for gi in range(groups):
        b0 = gi * nb
        part = lambda a: a[b0:b0 + nb]
        h, k, v, ya, sf = _proj(lgf, x, b0, part(sh1), part(sc1), w_kvu, ck, sk, part(sf0),
                                cw8, row(conv_b[0]), row(conv_ln_g[0]), row(conv_ln_b[0]))
        x1, hp, lt = _mixer(alpha, lgf, lgb, x, b0, h, k, v, sf, part(sb0), ya, wq, wg, wco, wro, wo,
                            row(ret_gn_g[0]), part(g1), part(sh2), part(sc2), row(ln1_g[0]), row(ln1_b[0]),
                            wr_t, cq, sq)
        rank, aff, slots = _route(lt, cap)
        part_rows = nb * l
        by_expert = slots.reshape(nb, N_EXPERTS, cap).transpose(1, 0, 2).reshape(-1)
        ids = jnp.concatenate([by_expert + p * part_rows for p in range(PACK_PARTS)])
        xe = _sc_gather(hp.reshape(PACK_PARTS * part_rows, hp.shape[-1]), ids)
        ye = _ffn(xe.reshape(PACK_PARTS, N_EXPERTS, nb * cap, hp.shape[-1]), wge, wue, wde)
        out = _combine(alpha, rank, aff, ye, x1, part(g2), row(ln2_g[0]), row(ln2_b[0]), cap, b, b0, out)
    return out
```

```python
import functools

import numpy as np
import jax
import jax.numpy as jnp
from jax import lax
from jax.experimental import pallas as pl
from jax.experimental.pallas import tpu as pltpu
from jax.experimental.pallas import tpu_sc as plsc

F32 = jnp.float32
BF16 = jnp.bfloat16
U32 = jnp.uint32

GRID_W = 64
CONV_DIM = 512
CONV_WIDTH = 31
CONV_HALO = CONV_WIDTH // 2
RET_HEADS = 4
RET_DK = 128
RET_DV = 256
RET_CHUNK = 256
ROPE_BASE = 10000.0
QK_DIM = RET_HEADS * RET_DK
V_DIM = RET_HEADS * RET_DV
N_EXPERTS = 16
EC_FACTOR = 2
LN_EPS = 1e-5

ADA_COLS = 768
PROJ_ROWS = 512
MIX_ROWS = 512
BATCH_GROUPS = 2
CONV_PAD = 16
CONV_BLOCK = 64
OUT_ROWS = 512
FFN_ROWS = 1024
PACK_PARTS = 2
SC_WINDOW = 128
SLOT_SHIFT = 4
SLOT_RADIX = 1 << SLOT_SHIFT
TOKEN_SHIFT = 6
V7X_VMEM_LIMIT = 56 * 1024 * 1024


def _dot(a, b):
    return jnp.dot(a, b, preferred_element_type=F32)


def _dot_nt(a, b):
    return lax.dot_general(a, b, (((1,), (1,)), ((), ())), preferred_element_type=F32)


def _dot_tn(a, b):
    return lax.dot_general(a, b, (((0,), (0,)), ((), ())), preferred_element_type=F32)


def _split(a):
    hi = a.astype(BF16)
    lo = (a - hi.astype(F32)).astype(BF16)
    return hi, lo


def _norm(x):
    mu = jnp.mean(x, axis=-1, keepdims=True)
    xc = x - mu
    var = jnp.mean(xc * xc, axis=-1, keepdims=True)
    return xc * lax.rsqrt(var + LN_EPS)


def _sigmoid(x):
    return 1.0 / (1.0 + jnp.exp(-x))


def _silu(x):
    return x * _sigmoid(x)


def _col_iota(n):
    return lax.broadcasted_iota(jnp.int32, (n, 1), 0).astype(F32)


def _pack_pair(lo, hi):
    lo_bits = pltpu.bitcast(lo.astype(F32), U32)
    hi_bits = pltpu.bitcast(hi.astype(F32), U32)
    return lax.shift_right_logical(lo_bits, jnp.uint32(16)) | hi_bits


def _unpack_pair(word):
    lo = pltpu.bitcast(lax.shift_left(word, jnp.uint32(16)), F32)
    hi = pltpu.bitcast(word & jnp.uint32(0xFFFF0000), F32)
    return lo.astype(BF16), hi.astype(BF16)


def _smem():
    return pl.BlockSpec(memory_space=pltpu.SMEM)


def _params(sem, vmem=V7X_VMEM_LIMIT):
    return pltpu.CompilerParams(dimension_semantics=sem, vmem_limit_bytes=vmem)


def _ada_kernel(c_ref, w_ref, b_ref, o_ref):
    a_hi, a_lo = _split(_silu(c_ref[...]))
    w_hi, w_lo = _split(w_ref[...])
    o_ref[...] = _dot(a_hi, w_hi) + (_dot(a_hi, w_lo) + _dot(a_lo, w_hi)) + b_ref[...]


def _ada(cc, w, b):
    m, d = cc.shape
    n = w.shape[1]
    return pl.pallas_call(
        _ada_kernel,
        grid=(n // ADA_COLS,),
        in_specs=[pl.BlockSpec((m, d), lambda j: (0, 0)),
                  pl.BlockSpec((d, ADA_COLS), lambda j: (0, j)),
                  pl.BlockSpec((1, ADA_COLS), lambda j: (0, j))],
        out_specs=pl.BlockSpec((m, ADA_COLS), lambda j: (0, j)),
        out_shape=jax.ShapeDtypeStruct((m, n), F32),
        compiler_params=_params(("parallel",)),
        name="ada_proj",
    )(cc, w, b)


def _ctx_kernel(lgf_ref, lgb_ref, ctx_ref, sh_ref, sc_ref, w_ref, sf_ref, sb_ref):
    x = ctx_ref[0]
    lc = x.shape[0]
    h = (_norm(x) * (1.0 + sc_ref[...]) + sh_ref[...]).astype(BF16)
    kv = _dot(h, w_ref[...])
    t = _col_iota(lc)
    for hh in range(RET_HEADS):
        k = kv[:, hh * RET_DK:(hh + 1) * RET_DK] * (RET_DK ** -0.5)
        v = kv[:, QK_DIM + hh * RET_DV:QK_DIM + (hh + 1) * RET_DV].astype(BF16)
        wf = jnp.exp(lgf_ref[hh] * (lc - 1.0 - t))
        wb = jnp.exp(lgb_ref[hh] * t)
        sf_ref[0, hh] = _dot_tn((k * wf).astype(BF16), v)
        sb_ref[0, hh] = _dot_tn((k * wb).astype(BF16), v)


def _ctx_states(lgf, lgb, ctx, csh, csc, w_kvu):
    b, lc, d = ctx.shape
    kvw = QK_DIM + V_DIM
    st = jax.ShapeDtypeStruct((b, RET_HEADS, RET_DK, RET_DV), F32)
    st_spec = pl.BlockSpec((1, RET_HEADS, RET_DK, RET_DV), lambda i: (i, 0, 0, 0))
    return pl.pallas_call(
        _ctx_kernel,
        grid=(b,),
        in_specs=[_smem(), _smem(),
                  pl.BlockSpec((1, lc, d), lambda i: (i, 0, 0)),
                  pl.BlockSpec((1, d), lambda i: (0, 0)),
                  pl.BlockSpec((1, d), lambda i: (0, 0)),
                  pl.BlockSpec((d, kvw), lambda i: (0, 0))],
        out_specs=(st_spec, st_spec),
        out_shape=(st, st),
        compiler_params=_params(("parallel",)),
        name="ctx_states",
    )(lgf, lgb, ctx, csh, csc, w_kvu)


def _proj_kernel(lgf_ref, xp_ref, x_ref, xn_ref, sh_ref, sc_ref, w_ref, ck_ref, sk_ref, sf0_ref,
                 cw_ref, cb_ref, clg_ref, clb_ref,
                 h_ref, k_ref, v_ref, ya_ref, sf_ref, s_scr, y_scr, yres_scr, z_scr):
    t = pl.program_id(1)
    nt = pl.num_programs(1)

    @pl.when(t == 0)
    def _():
        s_scr[...] = sf0_ref[0]

    rows = x_ref.shape[1]
    ext = rows + 2 * CONV_PAD
    x_ext = jnp.concatenate([xp_ref[0], x_ref[0], xn_ref[0]], axis=0)
    h_ext = (_norm(x_ext) * (1.0 + sc_ref[0]) + sh_ref[0]).astype(BF16)
    h = h_ext[CONV_PAD:CONV_PAD + rows]
    h_ref[0] = h
    kvw = QK_DIM + V_DIM
    u = _dot(h_ext, w_ref[:, kvw:])
    row = lax.broadcasted_iota(jnp.int32, (ext, 1), 0)
    head_ok = jnp.where(t == 0, 0.0, 1.0)
    tail_ok = jnp.where(t == nt - 1, 0.0, 1.0)
    inside = jnp.where(row < CONV_PAD, head_ok, jnp.where(row >= CONV_PAD + rows, tail_ok, 1.0))
    y_scr[...] = u[:, :CONV_DIM] * _sigmoid(u[:, CONV_DIM:]) * inside

    kk = _dot(h, w_ref[:, :QK_DIM])
    vv = _dot(h, w_ref[:, QK_DIM:kvw]).astype(BF16)
    v_ref[0] = vv
    ck = ck_ref[...]
    sk = sk_ref[...]
    pos = _col_iota(RET_CHUNK)
    for hh in range(RET_HEADS):
        lg = lgf_ref[hh]
        kh = kk[:, hh * RET_DK:(hh + 1) * RET_DK]
        kr = kh * ck + pltpu.roll(kh, RET_DK // 2, 1) * sk
        k_ref[0, :, hh * RET_DK:(hh + 1) * RET_DK] = kr.astype(BF16)
        state_dec = jnp.exp(lg * (RET_CHUNK - 1.0 - pos))
        chunk_dec = jnp.exp(lg * jnp.full((1, RET_DV), float(RET_CHUNK), F32))
        for c in range(rows // RET_CHUNK):
            r0 = c * RET_CHUNK
            s = s_scr[hh]
            sf_ref[0, c, hh] = s.astype(BF16)
            kc = (kr[r0:r0 + RET_CHUNK] * state_dec).astype(BF16)
            vc = vv[r0:r0 + RET_CHUNK, hh * RET_DV:(hh + 1) * RET_DV]
            s_scr[hh] = s * chunk_dec + _dot_tn(kc, vc)

    span = ext - 8
    for r in range(8):
        yres_scr[r] = y_scr[pl.ds(r, span), :]

    for i in range(rows // CONV_BLOCK):
        base = i * CONV_BLOCK
        acc = None
        for w in range(CONV_WIDTH):
            a, r = divmod(w + 1, 8)
            wk = jnp.concatenate([cw_ref[8 * w:8 * w + 8, :]] * (CONV_BLOCK // 8), axis=0)
            tap = yres_scr[r, base + 8 * a:base + 8 * a + CONV_BLOCK, :] * wk
            acc = tap if acc is None else acc + tap
        z_scr[base:base + CONV_BLOCK, :] = acc
    z = _norm(z_scr[...] + cb_ref[...]) * clg_ref[...] + clb_ref[...]
    ya_ref[0] = _silu(z).astype(BF16)


def _proj(lgf, x, b0, sh1, sc1, w_kvu, ck, sk, sf0, cw, cb, clg, clb):
    _, l, d = x.shape
    b = sh1.shape[0]
    nt = l // PROJ_ROWS
    cpt = PROJ_ROWS // RET_CHUNK
    hpt = PROJ_ROWS // CONV_PAD
    n_halo = l // CONV_PAD
    ext = PROJ_ROWS + 2 * CONV_PAD
    tile = lambda w: pl.BlockSpec((1, PROJ_ROWS, w), lambda i, t: (i, t, 0))
    vec = pl.BlockSpec((1, 1, d), lambda i, t: (i, 0, 0))
    full = lambda a: pl.BlockSpec(a.shape, lambda i, t: (0,) * a.ndim, pipeline_mode=pl.Buffered(1))
    rope = pl.BlockSpec((PROJ_ROWS, RET_DK), lambda i, t: (t, 0))
    x_tile = pl.BlockSpec((1, PROJ_ROWS, d), lambda i, t: (i + b0, t, 0))
    prev = pl.BlockSpec((1, CONV_PAD, d), lambda i, t: (i + b0, jnp.maximum(t * hpt - 1, 0), 0))
    nxt = pl.BlockSpec((1, CONV_PAD, d), lambda i, t: (i + b0, jnp.minimum((t + 1) * hpt, n_halo - 1), 0))
    return pl.pallas_call(
        _proj_kernel,
        grid=(b, nt),
        in_specs=[_smem(), prev, x_tile, nxt, vec, vec, full(w_kvu), rope, rope,
                  pl.BlockSpec((1, RET_HEADS, RET_DK, RET_DV), lambda i, t: (i, 0, 0, 0)),
                  full(cw), full(cb), full(clg), full(clb)],
        out_specs=(tile(d), tile(QK_DIM), tile(V_DIM), tile(CONV_DIM),
                   pl.BlockSpec((1, cpt, RET_HEADS, RET_DK, RET_DV), lambda i, t: (i, t, 0, 0, 0))),
        out_shape=(jax.ShapeDtypeStruct((b, l, d), BF16),
                   jax.ShapeDtypeStruct((b, l, QK_DIM), BF16),
                   jax.ShapeDtypeStruct((b, l, V_DIM), BF16),
                   jax.ShapeDtypeStruct((b, l, CONV_DIM), BF16),
                   jax.ShapeDtypeStruct((b, l // RET_CHUNK, RET_HEADS, RET_DK, RET_DV), BF16)),
        scratch_shapes=[pltpu.VMEM((RET_HEADS, RET_DK, RET_DV), F32),
                        pltpu.VMEM((ext, CONV_DIM), F32),
                        pltpu.VMEM((8, ext - 8, CONV_DIM), F32),
                        pltpu.VMEM((PROJ_ROWS, CONV_DIM), F32)],
        compiler_params=_params(("parallel", "arbitrary")),
        name="proj_kv_conv",
    )(lgf, x, x, x, sh1, sc1, w_kvu, ck, sk, sf0, cw, cb, clg, clb)


def _mixer_kernel(alpha, lgf_ref, lgb_ref, x_ref, h_ref, k_ref, v_ref, sf_ref, sb0_ref, ya_ref,
                  wq_ref, wg_ref, wco_ref, wro_ref, wo_ref, gng_ref,
                  g1_ref, sh2_ref, sc2_ref, l1g_ref, l1b_ref, wr_ref, cq_ref, sq_ref,
                  x1_ref, hp_ref, lt_ref, sb_scr):
    t = pl.program_id(1)

    @pl.when(t == 0)
    def _():
        sb_scr[...] = sb0_ref[0]

    h = h_ref[0]
    rows = h.shape[0]
    cq = cq_ref[...]
    sq = sq_ref[...]
    q_all = _dot(h, wq_ref[...])
    chunk = RET_CHUNK
    pos = _col_iota(chunk)
    di = lax.broadcasted_iota(jnp.int32, (chunk, chunk), 0)
    dj = lax.broadcasted_iota(jnp.int32, (chunk, chunk), 1)
    dist = (di - dj).astype(F32)

    n_chunks = rows // chunk
    o_parts = [[None] * RET_HEADS for _ in range(n_chunks)]
    for hh in range(RET_HEADS):
        lgf = lgf_ref[hh]
        lgb = lgb_ref[hh]
        qh = q_all[:, hh * RET_DK:(hh + 1) * RET_DK]
        qr = qh * cq + pltpu.roll(qh, RET_DK // 2, 1) * sq
        decay = jnp.where(dist >= 0.0, jnp.exp(lgf * jnp.maximum(dist, 0.0)),
                          jnp.exp(lgb * jnp.maximum(-dist, 0.0)))
        cross_f = jnp.exp(lgf * (pos + 1.0))
        cross_b = jnp.exp(lgb * (chunk - pos))
        state_b = jnp.exp(lgb * pos)
        chunk_b = jnp.exp(lgb * jnp.full((1, RET_DV), float(chunk), F32))
        for c in reversed(range(n_chunks)):
            r0 = c * chunk
            qc = qr[r0:r0 + chunk]
            kc = k_ref[0, r0:r0 + chunk, hh * RET_DK:(hh + 1) * RET_DK]
            vc = v_ref[0, r0:r0 + chunk, hh * RET_DV:(hh + 1) * RET_DV]
            sb = sb_scr[hh]
            att = _dot_nt(qc.astype(BF16), kc) * decay
            lhs = jnp.concatenate([att.astype(BF16), (qc * cross_f).astype(BF16), (qc * cross_b).astype(BF16)], axis=1)
            rhs = jnp.concatenate([vc, sf_ref[0, c, hh], sb.astype(BF16)], axis=0)
            o = _dot(lhs, rhs)
            kb = (kc.astype(F32) * state_b).astype(BF16)
            sb_scr[hh] = sb * chunk_b + _dot_tn(kb, vc)
            o_parts[c][hh] = _norm(o)
    on = jnp.concatenate([jnp.concatenate(o_parts[c], axis=1) for c in range(n_chunks)], axis=0)

    d = x_ref.shape[-1]
    pw = d // (2 * PACK_PARTS)
    wr_hi, wr_lo = _split(wr_ref[...])
    g_ret = _dot(h, wg_ref[:, :V_DIM])
    yb_in = (_silu(g_ret) * (on * gng_ref[...])).astype(BF16)
    y_b = _dot(yb_in, wro_ref[...])
    y_a = _dot(ya_ref[0], wco_ref[...])
    g_a = _dot(h, wg_ref[:, V_DIM:V_DIM + d])
    g_b = _dot(h, wg_ref[:, V_DIM + d:])
    y = (_sigmoid(g_a) * y_a + _sigmoid(g_b) * y_b).astype(BF16)
    ym = _dot(y, wo_ref[...])

    x1 = _norm(alpha * x_ref[0] + g1_ref[0] * ym) * l1g_ref[...] + l1b_ref[...]
    x1_ref[0] = x1
    h2 = _norm(x1) * (1.0 + sc2_ref[0]) + sh2_ref[0]
    h2_hi, h2_lo = _split(h2)
    for p in range(PACK_PARTS):
        c0 = 2 * p * pw
        hp_ref[p, 0] = _pack_pair(h2_hi[:, c0:c0 + pw], h2_hi[:, c0 + pw:c0 + 2 * pw])
    lt_ref[0] = _dot_nt(wr_hi, h2_hi) + (_dot_nt(wr_hi, h2_lo) + _dot_nt(wr_lo, h2_hi))


def _mixer(alpha, lgf, lgb, x, b0, h, k, v, sf, sb0, ya, wq, wg, wco, wro, wo, gng,
           g1, sh2, sc2, l1g, l1b, wr_t, cq, sq):
    b, l, d = h.shape
    nt = l // MIX_ROWS
    cpt = MIX_ROWS // RET_CHUNK
    pw = d // (2 * PACK_PARTS)
    rev = lambda w: pl.BlockSpec((1, MIX_ROWS, w), lambda i, t: (i, nt - 1 - t, 0))
    vec = pl.BlockSpec((1, 1, d), lambda i, t: (i, 0, 0))
    full = lambda a: pl.BlockSpec(a.shape, lambda i, t: (0,) * a.ndim, pipeline_mode=pl.Buffered(1))
    rope = pl.BlockSpec((MIX_ROWS, RET_DK), lambda i, t: (nt - 1 - t, 0))
    state = pl.BlockSpec((1, RET_HEADS, RET_DK, RET_DV), lambda i, t: (i, 0, 0, 0))
    return pl.pallas_call(
        functools.partial(_mixer_kernel, alpha),
        grid=(b, nt),
        in_specs=[_smem(), _smem(),
                  pl.BlockSpec((1, MIX_ROWS, d), lambda i, t: (i + b0, nt - 1 - t, 0)),
                  rev(d), rev(QK_DIM), rev(V_DIM),
                  pl.BlockSpec((1, cpt, RET_HEADS, RET_DK, RET_DV), lambda i, t: (i, nt - 1 - t, 0, 0, 0)),
                  state, rev(CONV_DIM),
                  full(wq), full(wg), full(wco), full(wro), full(wo), full(gng),
                  vec, vec, vec, full(l1g), full(l1b), full(wr_t), rope, rope],
        out_specs=(rev(d),
                   pl.BlockSpec((PACK_PARTS, 1, MIX_ROWS, pw), lambda i, t: (0, i, nt - 1 - t, 0)),
                   pl.BlockSpec((1, N_EXPERTS, MIX_ROWS), lambda i, t: (i, 0, nt - 1 - t))),
        out_shape=(jax.ShapeDtypeStruct((b, l, d), F32),
                   jax.ShapeDtypeStruct((PACK_PARTS, b, l, pw), U32),
                   jax.ShapeDtypeStruct((b, N_EXPERTS, l), F32)),
        scratch_shapes=[pltpu.VMEM((RET_HEADS, RET_DK, RET_DV), F32)],
        compiler_params=_params(("parallel", "arbitrary")),
        name="mixer",
    )(lgf, lgb, x, h, k, v, sf, sb0, ya, wq, wg, wco, wro, wo, gng,
      g1, sh2, sc2, l1g, l1b, wr_t, cq, sq)


def _route_kernel(cap, lt_ref, rank_ref, aff_ref, idx_ref):
    logits = lt_ref[0]
    n_e, l = logits.shape
    m = jnp.max(logits, axis=0, keepdims=True)
    p = jnp.exp(logits - m)
    aff = p / jnp.sum(p, axis=0, keepdims=True)
    aff_ref[0] = aff
    capf = float(cap)

    def count(ones):
        return jnp.sum(ones, axis=1, keepdims=True)

    def refine(lo, shift, width, keep):
        best = lo
        for digit in range(1, 1 << width):
            cand = lo | jnp.left_shift(jnp.int32(digit), shift)
            best = jnp.where(keep(cand), cand, best)
        return best

    def bisect(n_bits, keep):
        lo = jnp.zeros((n_e, 1), jnp.int32)
        if n_bits % 2:
            lo = refine(lo, n_bits - 1, 1, keep)
        return lax.fori_loop(0, n_bits // 2, lambda i, v: refine(v, 2 * (n_bits // 2 - 1 - i), 2, keep), lo)

    thr = pltpu.bitcast(
        bisect(31, lambda cand: count(jnp.where(aff >= pltpu.bitcast(cand, F32), 1.0, 0.0)) >= capf), F32)
    gt = jnp.where(aff > thr, 1.0, 0.0)
    eq = jnp.where(aff == thr, 1.0, 0.0)
    need = capf - count(gt)
    idx = lax.broadcasted_iota(jnp.int32, (n_e, l), 1)
    idx_bits = int(l - 1).bit_length()

    last = bisect(idx_bits, lambda cand: count(jnp.where(idx < cand, eq, 0.0)) < need)
    bound = jnp.where(need > 0.0, last + 1, 0)
    sel = gt + jnp.where(idx < bound, eq, 0.0)

    blk = 128
    ti = lax.broadcasted_iota(jnp.int32, (blk, blk), 0)
    tj = lax.broadcasted_iota(jnp.int32, (blk, blk), 1)
    tri = jnp.where(ti < tj, 1.0, 0.0).astype(BF16)
    offset = jnp.zeros((n_e, 1), F32)
    for j in range(l // blk):
        sj = sel[:, j * blk:(j + 1) * blk]
        before = _dot(sj.astype(BF16), tri) + offset
        rank_ref[0, :, j * blk:(j + 1) * blk] = jnp.where(sj > 0.0, before, -1.0).astype(jnp.int32)
        offset = offset + jnp.sum(sj, axis=1, keepdims=True)

    rank = rank_ref[0]
    hi_digit = jnp.right_shift(rank, SLOT_SHIFT)
    lo_digit = jnp.bitwise_and(rank, SLOT_RADIX - 1)
    digit = lax.broadcasted_iota(jnp.int32, (SLOT_RADIX, l), 0)
    tok_hi = jnp.right_shift(idx[:1], TOKEN_SHIFT).astype(F32)
    tok_lo = jnp.bitwise_and(idx[:1], (1 << TOKEN_SHIFT) - 1).astype(F32)
    a_hi, a_lo, b_rows = [], [], []
    for e in range(n_e):
        is_a = hi_digit[e:e + 1] == digit
        a_hi.append(jnp.where(is_a, tok_hi, 0.0))
        a_lo.append(jnp.where(is_a, tok_lo, 0.0))
        b_rows.append(jnp.where(lo_digit[e:e + 1] == digit, 1.0, 0.0))
    b_all = jnp.concatenate(b_rows, axis=0).astype(BF16)
    cross_hi = _dot_nt(jnp.concatenate(a_hi, axis=0).astype(BF16), b_all)
    cross_lo = _dot_nt(jnp.concatenate(a_lo, axis=0).astype(BF16), b_all)
    cross = cross_hi * float(1 << TOKEN_SHIFT) + cross_lo + (pl.program_id(0) * l).astype(F32)
    for e in range(n_e):
        s0 = e * SLOT_RADIX
        idx_ref[0, e] = cross[s0:s0 + SLOT_RADIX, s0:s0 + SLOT_RADIX].astype(jnp.int32)


def _route(lt, cap):
    b, n_e, l = lt.shape
    assert cap == SLOT_RADIX * SLOT_RADIX and l <= (1 << TOKEN_SHIFT) * 256
    spec = pl.BlockSpec((1, n_e, l), lambda i: (i, 0, 0))
    return pl.pallas_call(
        functools.partial(_route_kernel, cap),
        grid=(b,),
        in_specs=[spec],
        out_specs=(spec, spec, pl.BlockSpec((1, n_e, SLOT_RADIX, SLOT_RADIX), lambda i: (i, 0, 0, 0))),
        out_shape=(jax.ShapeDtypeStruct((b, n_e, l), jnp.int32),
                   jax.ShapeDtypeStruct((b, n_e, l), F32),
                   jax.ShapeDtypeStruct((b, n_e, SLOT_RADIX, SLOT_RADIX), jnp.int32)),
        compiler_params=_params(("parallel",)),
        name="route_topc",
    )(lt)


def _sc_gather(rows, idx):
    n = idx.shape[0]
    w = rows.shape[1]
    mesh = plsc.VectorSubcoreMesh(core_axis_name="c", subcore_axis_name="s")

    @pl.kernel(out_type=jax.ShapeDtypeStruct((n, w), rows.dtype), mesh=mesh, scratch_types=[])
    def gather(rows_hbm, idx_hbm, out_hbm):
        def window(idx_vmem, out_vmem):
            pltpu.sync_copy(rows_hbm.at[idx_vmem.at[0]], out_vmem)

        pltpu.emit_pipeline(
            window,
            grid=(n // SC_WINDOW,),
            in_specs=[pl.BlockSpec((1, SC_WINDOW), index_map=lambda i: (0, i))],
            out_specs=[pl.BlockSpec((SC_WINDOW, w), index_map=lambda i: (i, 0))],
            core_axis_name=("c", "s"),
            dimension_semantics=(pltpu.PARALLEL,),
        )(idx_hbm, out_hbm)

    return gather(rows, idx.reshape(1, n))


def _ffn_kernel(xe_ref, wg_ref, wu_ref, wd_ref, ye_ref, wg_scr, wu_scr, wd_scr):
    @pl.when(pl.program_id(1) == 0)
    def _():
        wg_scr[...] = wg_ref[0].astype(BF16)
        wu_scr[...] = wu_ref[0].astype(BF16)
        wd_scr[...] = wd_ref[0].astype(BF16)

    halves = []
    for p in range(PACK_PARTS):
        halves.extend(_unpack_pair(xe_ref[p, 0]))
    xe = jnp.concatenate(halves, axis=1)
    he = (_silu(_dot(xe, wg_scr[...])) * _dot(xe, wu_scr[...])).astype(BF16)
    ye_ref[0] = _dot(he, wd_scr[...]).astype(BF16)


def _ffn(xe, wgate, wup, wdown):
    _, n_e, rows, w = xe.shape
    _, d, f = wgate.shape
    tile = min(FFN_ROWS, rows)
    return pl.pallas_call(
        _ffn_kernel,
        grid=(n_e, rows // tile),
        in_specs=[pl.BlockSpec((PACK_PARTS, 1, tile, w), lambda e, m: (0, e, m, 0)),
                  pl.BlockSpec((1, d, f), lambda e, m: (e, 0, 0)),
                  pl.BlockSpec((1, d, f), lambda e, m: (e, 0, 0)),
                  pl.BlockSpec((1, f, d), lambda e, m: (e, 0, 0))],
        out_specs=pl.BlockSpec((1, tile, d), lambda e, m: (e, m, 0)),
        out_shape=jax.ShapeDtypeStruct((n_e, rows, d), BF16),
        scratch_shapes=[pltpu.VMEM((d, f), BF16), pltpu.VMEM((d, f), BF16), pltpu.VMEM((f, d), BF16)],
        compiler_params=_params(("parallel", "arbitrary")),
        name="expert_ffn",
    )(xe, wgate, wup, wdown)


def _combine_kernel(cap, alpha, rank_ref, aff_ref, ye_ref, x1_ref, g2_ref, lng_ref, lnb_ref, *rest):
    o_ref = rest[-1]
    n_e = rank_ref.shape[1]
    slot = lax.broadcasted_iota(jnp.int32, (cap, OUT_ROWS), 0)
    acc = None
    for e in range(n_e):
        hit = slot == rank_ref[0, e:e + 1, :]
        gated = jnp.where(hit, aff_ref[0, e:e + 1, :], 0.0).astype(BF16)
        part = _dot_tn(gated, ye_ref[e, 0])
        acc = part if acc is None else acc + part
    z = alpha * x1_ref[0] + g2_ref[0] * acc
    o_ref[0] = _norm(z) * lng_ref[...] + lnb_ref[...]


def _combine(alpha, rank, aff, ye, x1, g2, ln_g, ln_b, cap, b_total, b0, earlier):
    b, l, d = x1.shape
    n_e = rank.shape[1]
    row = pl.BlockSpec((1, n_e, OUT_ROWS), lambda i, t: (i, 0, t))
    vec = pl.BlockSpec((1, d), lambda i, t: (0, 0))
    in_specs = [row, row,
                pl.BlockSpec((n_e, 1, cap, d), lambda i, t: (0, i, 0, 0)),
                pl.BlockSpec((1, OUT_ROWS, d), lambda i, t: (i, t, 0)),
                pl.BlockSpec((1, 1, d), lambda i, t: (i, 0, 0)), vec, vec]
    args = [rank, aff, ye.reshape(n_e, b, cap, d), x1, g2, ln_g, ln_b]
    aliases = {}
    if earlier is not None:
        in_specs.append(pl.BlockSpec(memory_space=pl.ANY))
        aliases = {len(args): 0}
        args.append(earlier)
    return pl.pallas_call(
        functools.partial(_combine_kernel, cap, alpha),
        grid=(b, l // OUT_ROWS),
        in_specs=in_specs,
        out_specs=pl.BlockSpec((1, OUT_ROWS, d), lambda i, t: (i + b0, t, 0)),
        out_shape=jax.ShapeDtypeStruct((b_total, l, d), F32),
        input_output_aliases=aliases,
        compiler_params=_params(("parallel", "arbitrary")),
        name="expert_combine",
    )(*args)


def _rope_tables(l):
    n_axis = RET_DK // 4
    freqs = ROPE_BASE ** (-np.arange(n_axis, dtype=np.float64) / n_axis)
    pos = np.arange(l)
    ang = np.concatenate([(pos // GRID_W)[:, None] * freqs, (pos % GRID_W)[:, None] * freqs], axis=-1)
    cos, sin = np.cos(ang), np.sin(ang)
    return (np.concatenate([cos, cos], axis=-1).astype(np.float32),
            np.concatenate([-sin, sin], axis=-1).astype(np.float32))


def kernel(x, c, ctx, c_ctx, w_ada, b_ada, w_in, conv_w, conv_b, conv_ln_g, conv_ln_b, w_conv_out,
           log_decay_f, log_decay_b, ret_gn_g, w_ret_out, w_out, ln1_g, ln1_b,
           w_router, w_gate, w_up, w_down, ln2_g, ln2_b):
    depth = w_ada.shape[0]
    assert depth == 1, "single trunk layer"
    b, l, d = x.shape
    alpha = (2.0 * depth) ** 0.25
    cap = EC_FACTOR * l // N_EXPERTS
    u_end = 2 * CONV_DIM
    q_end = u_end + QK_DIM
    k_end = q_end + QK_DIM
    v_end = k_end + V_DIM
    row = lambda a: a.reshape(1, -1)

    n_mod = b + 1
    pad = (-n_mod) % 8
    cc = jnp.concatenate([c, c_ctx[None], jnp.zeros((pad, d), F32)], axis=0)
    mod = _ada(cc, w_ada[0], row(b_ada[0]))
    sh1, sc1, g1, sh2, sc2, g2 = [m.reshape(b, 1, d) for m in jnp.split(mod[:b], 6, axis=-1)]
    csh1, csc1 = mod[b:b + 1, :d], mod[b:b + 1, d:2 * d]

    perm = np.concatenate([np.arange(0, RET_DK, 2), np.arange(1, RET_DK, 2)])
    perm = (np.arange(RET_HEADS)[:, None] * RET_DK + perm[None, :]).reshape(-1)
    w = w_in[0]
    w_kvu = jnp.concatenate([w[:, q_end:k_end][:, perm], w[:, k_end:v_end], w[:, :u_end]], axis=1).astype(BF16)
    wq = w[:, u_end:q_end][:, perm].astype(BF16)
    wg = w[:, v_end:].astype(BF16)

    cos_t, sin_t = _rope_tables(l)
    cq, sq = jnp.asarray(cos_t), jnp.asarray(sin_t)
    k_scale = RET_DK ** -0.5
    ck, sk = jnp.asarray(cos_t * k_scale), jnp.asarray(sin_t * k_scale)

    lgf, lgb = log_decay_f[0], log_decay_b[0]
    sf0, sb0 = _ctx_states(lgf, lgb, ctx, csh1, csc1, w_kvu)
    cw8 = jnp.repeat(conv_w[0], 8, axis=0)
    wco, wro, wo = w_conv_out[0].astype(BF16), w_ret_out[0].astype(BF16), w_out[0].astype(BF16)
    wge, wue, wde = w_gate[0], w_up[0], w_down[0]
    wr_t = w_router[0].T
    groups = BATCH_GROUPS if b % BATCH_GROUPS == 0 else 1
    nb = b // groups
    out = None
    for gi in range(groups):
        b0 = gi * nb
        part = lambda a: a[b0:b0 + nb]
        h, k, v, ya, sf = _proj(lgf, x, b0, part(sh1), part(sc1), w_kvu, ck, sk, part(sf0),
                                cw8, row(conv_b[0]), row(conv_ln_g[0]), row(conv_ln_b[0]))
        x1, hp, lt = _mixer(alpha, lgf, lgb, x, b0, h, k, v, sf, part(sb0), ya, wq, wg, wco, wro, wo,
                            row(ret_gn_g[0]), part(g1), part(sh2), part(sc2), row(ln1_g[0]), row(ln1_b[0]),
                            wr_t, cq, sq)
        rank, aff, slots = _route(lt, cap)
        part_rows = nb * l
        by_expert = slots.reshape(nb, N_EXPERTS, cap).transpose(1, 0, 2).reshape(-1)
        ids = jnp.concatenate([by_expert + p * part_rows for p in range(PACK_PARTS)])
        xe = _sc_gather(hp.reshape(PACK_PARTS * part_rows, hp.shape[-1]), ids)
        ye = _ffn(xe.reshape(PACK_PARTS, N_EXPERTS, nb * cap, hp.shape[-1]), wge, wue, wde)
        out = _combine(alpha, rank, aff, ye, x1, part(g2), row(ln2_g[0]), row(ln2_b[0]), cap, b, b0, out)
    return out
```

```python
import functools

import numpy as np
import jax
import jax.numpy as jnp
from jax import lax
from jax.experimental import pallas as pl
from jax.experimental.pallas import tpu as pltpu
from jax.experimental.pallas import tpu_sc as plsc

F32 = jnp.float32
BF16 = jnp.bfloat16
U32 = jnp.uint32

GRID_W = 64
CONV_DIM = 512
CONV_WIDTH = 31
CONV_HALO = CONV_WIDTH // 2
RET_HEADS = 4
RET_DK = 128
RET_DV = 256
RET_CHUNK = 256
ROPE_BASE = 10000.0
QK_DIM = RET_HEADS * RET_DK
V_DIM = RET_HEADS * RET_DV
N_EXPERTS = 16
EC_FACTOR = 2
LN_EPS = 1e-5

ADA_COLS = 768
PROJ_ROWS = 512
MIX_ROWS = 512
BATCH_GROUPS = 2
CONV_PAD = 16
CONV_BLOCK = 64
OUT_ROWS = 512
FFN_ROWS = 1024
PACK_PARTS = 2
SC_WINDOW = 128
SLOT_SHIFT = 4
SLOT_RADIX = 1 << SLOT_SHIFT
TOKEN_SHIFT = 6
V7X_VMEM_LIMIT = 56 * 1024 * 1024


def _dot(a, b):
    return jnp.dot(a, b, preferred_element_type=F32)


def _dot_nt(a, b):
    return lax.dot_general(a, b, (((1,), (1,)), ((), ())), preferred_element_type=F32)


def _dot_tn(a, b):
    return lax.dot_general(a, b, (((0,), (0,)), ((), ())), preferred_element_type=F32)


def _split(a):
    hi = a.astype(BF16)
    lo = (a - hi.astype(F32)).astype(BF16)
    return hi, lo


def _norm(x):
    mu = jnp.mean(x, axis=-1, keepdims=True)
    xc = x - mu
    var = jnp.mean(xc * xc, axis=-1, keepdims=True)
    return xc * lax.rsqrt(var + LN_EPS)


def _sigmoid(x):
    return 1.0 / (1.0 + jnp.exp(-x))


def _silu(x):
    return x * _sigmoid(x)


def _col_iota(n):
    return lax.broadcasted_iota(jnp.int32, (n, 1), 0).astype(F32)


def _pack_pair(lo, hi):
    lo_bits = pltpu.bitcast(lo.astype(F32), U32)
    hi_bits = pltpu.bitcast(hi.astype(F32), U32)
    return lax.shift_right_logical(lo_bits, jnp.uint32(16)) | hi_bits


def _unpack_pair(word):
    lo = pltpu.bitcast(lax.shift_left(word, jnp.uint32(16)), F32)
    hi = pltpu.bitcast(word & jnp.uint32(0xFFFF0000), F32)
    return lo.astype(BF16), hi.astype(BF16)


def _smem():
    return pl.BlockSpec(memory_space=pltpu.SMEM)


def _params(sem, vmem=V7X_VMEM_LIMIT):
    return pltpu.CompilerParams(dimension_semantics=sem, vmem_limit_bytes=vmem)


def _ada_kernel(c_ref, w_ref, b_ref, o_ref):
    a_hi, a_lo = _split(_silu(c_ref[...]))
    w_hi, w_lo = _split(w_ref[...])
    o_ref[...] = _dot(a_hi, w_hi) + (_dot(a_hi, w_lo) + _dot(a_lo, w_hi)) + b_ref[...]


def _ada(cc, w, b):
    m, d = cc.shape
    n = w.shape[1]
    return pl.pallas_call(
        _ada_kernel,
        grid=(n // ADA_COLS,),
        in_specs=[pl.BlockSpec((m, d), lambda j: (0, 0)),
                  pl.BlockSpec((d, ADA_COLS), lambda j: (0, j)),
                  pl.BlockSpec((1, ADA_COLS), lambda j: (0, j))],
        out_specs=pl.BlockSpec((m, ADA_COLS), lambda j: (0, j)),
        out_shape=jax.ShapeDtypeStruct((m, n), F32),
        compiler_params=_params(("parallel",)),
        name="ada_proj",
    )(cc, w, b)


def _ctx_kernel(lgf_ref, lgb_ref, ctx_ref, sh_ref, sc_ref, w_ref, sf_ref, sb_ref):
    x = ctx_ref[0]
    lc = x.shape[0]
    h = (_norm(x) * (1.0 + sc_ref[...]) + sh_ref[...]).astype(BF16)
    kv = _dot(h, w_ref[...])
    t = _col_iota(lc)
    for hh in range(RET_HEADS):
        k = kv[:, hh * RET_DK:(hh + 1) * RET_DK] * (RET_DK ** -0.5)
        v = kv[:, QK_DIM + hh * RET_DV:QK_DIM + (hh + 1) * RET_DV].astype(BF16)
        wf = jnp.exp(lgf_ref[hh] * (lc - 1.0 - t))
        wb = jnp.exp(lgb_ref[hh] * t)
        sf_ref[0, hh] = _dot_tn((k * wf).astype(BF16), v)
        sb_ref[0, hh] = _dot_tn((k * wb).astype(BF16), v)


def _ctx_states(lgf, lgb, ctx, csh, csc, w_kvu):
    b, lc, d = ctx.shape
    kvw = QK_DIM + V_DIM
    st = jax.ShapeDtypeStruct((b, RET_HEADS, RET_DK, RET_DV), F32)
    st_spec = pl.BlockSpec((1, RET_HEADS, RET_DK, RET_DV), lambda i: (i, 0, 0, 0))
    return pl.pallas_call(
        _ctx_kernel,
        grid=(b,),
        in_specs=[_smem(), _smem(),
                  pl.BlockSpec((1, lc, d), lambda i: (i, 0, 0)),
                  pl.BlockSpec((1, d), lambda i: (0, 0)),
                  pl.BlockSpec((1, d), lambda i: (0, 0)),
                  pl.BlockSpec((d, kvw), lambda i: (0, 0))],
        out_specs=(st_spec, st_spec),
        out_shape=(st, st),
        compiler_params=_params(("parallel",)),
        name="ctx_states",
    )(lgf, lgb, ctx, csh, csc, w_kvu)


def _proj_kernel(lgf_ref, xp_ref, x_ref, xn_ref, sh_ref, sc_ref, w_ref, ck_ref, sk_ref, sf0_ref,
                 cw_ref, cb_ref, clg_ref, clb_ref,
                 h_ref, k_ref, v_ref, ya_ref, sf_ref, s_scr, y_scr, yres_scr, z_scr):
    t = pl.program_id(1)
    nt = pl.num_programs(1)

    @pl.when(t == 0)
    def _():
        s_scr[...] = sf0_ref[0]

    rows = x_ref.shape[1]
    ext = rows + 2 * CONV_PAD
    x_ext = jnp.concatenate([xp_ref[0], x_ref[0], xn_ref[0]], axis=0)
    h_ext = (_norm(x_ext) * (1.0 + sc_ref[0]) + sh_ref[0]).astype(BF16)
    h = h_ext[CONV_PAD:CONV_PAD + rows]
    h_ref[0] = h
    kvw = QK_DIM + V_DIM
    u = _dot(h_ext, w_ref[:, kvw:])
    row = lax.broadcasted_iota(jnp.int32, (ext, 1), 0)
    head_ok = jnp.where(t == 0, 0.0, 1.0)
    tail_ok = jnp.where(t == nt - 1, 0.0, 1.0)
    inside = jnp.where(row < CONV_PAD, head_ok, jnp.where(row >= CONV_PAD + rows, tail_ok, 1.0))
    y_scr[...] = u[:, :CONV_DIM] * _sigmoid(u[:, CONV_DIM:]) * inside

    kk = _dot(h, w_ref[:, :QK_DIM])
    vv = _dot(h, w_ref[:, QK_DIM:kvw]).astype(BF16)
    v_ref[0] = vv
    ck = ck_ref[...]
    sk = sk_ref[...]
    pos = _col_iota(RET_CHUNK)
    for hh in range(RET_HEADS):
        lg = lgf_ref[hh]
        kh = kk[:, hh * RET_DK:(hh + 1) * RET_DK]
        kr = kh * ck + pltpu.roll(kh, RET_DK // 2, 1) * sk
        k_ref[0, :, hh * RET_DK:(hh + 1) * RET_DK] = kr.astype(BF16)
        state_dec = jnp.exp(lg * (RET_CHUNK - 1.0 - pos))
        chunk_dec = jnp.exp(lg * jnp.full((1, RET_DV), float(RET_CHUNK), F32))
        for c in range(rows // RET_CHUNK):
            r0 = c * RET_CHUNK
            s = s_scr[hh]
            sf_ref[0, c, hh] = s.astype(BF16)
            kc = (kr[r0:r0 + RET_CHUNK] * state_dec).astype(BF16)
            vc = vv[r0:r0 + RET_CHUNK, hh * RET_DV:(hh + 1) * RET_DV]
            s_scr[hh] = s * chunk_dec + _dot_tn(kc, vc)

    span = ext - 8
    for r in range(8):
        yres_scr[r] = y_scr[pl.ds(r, span), :]

    for i in range(rows // CONV_BLOCK):
        base = i * CONV_BLOCK
        acc = None
        for w in range(CONV_WIDTH):
            a, r = divmod(w + 1, 8)
            wk = jnp.concatenate([cw_ref[8 * w:8 * w + 8, :]] * (CONV_BLOCK // 8), axis=0)
            tap = yres_scr[r, base + 8 * a:base + 8 * a + CONV_BLOCK, :] * wk
            acc = tap if acc is None else acc + tap
        z_scr[base:base + CONV_BLOCK, :] = acc
    z = _norm(z_scr[...] + cb_ref[...]) * clg_ref[...] + clb_ref[...]
    ya_ref[0] = _silu(z).astype(BF16)


def _proj(lgf, x, b0, sh1, sc1, w_kvu, ck, sk, sf0, cw, cb, clg, clb):
    _, l, d = x.shape
    b = sh1.shape[0]
    nt = l // PROJ_ROWS
    cpt = PROJ_ROWS // RET_CHUNK
    hpt = PROJ_ROWS // CONV_PAD
    n_halo = l // CONV_PAD
    ext = PROJ_ROWS + 2 * CONV_PAD
    tile = lambda w: pl.BlockSpec((1, PROJ_ROWS, w), lambda i, t: (i, t, 0))
    vec = pl.BlockSpec((1, 1, d), lambda i, t: (i, 0, 0))
    full = lambda a: pl.BlockSpec(a.shape, lambda i, t: (0,) * a.ndim, pipeline_mode=pl.Buffered(1))
    rope = pl.BlockSpec((PROJ_ROWS, RET_DK), lambda i, t: (t, 0))
    x_tile = pl.BlockSpec((1, PROJ_ROWS, d), lambda i, t: (i + b0, t, 0))
    prev = pl.BlockSpec((1, CONV_PAD, d), lambda i, t: (i + b0, jnp.maximum(t * hpt - 1, 0), 0))
    nxt = pl.BlockSpec((1, CONV_PAD, d), lambda i, t: (i + b0, jnp.minimum((t + 1) * hpt, n_halo - 1), 0))
    return pl.pallas_call(
        _proj_kernel,
        grid=(b, nt),
        in_specs=[_smem(), prev, x_tile, nxt, vec, vec, full(w_kvu), rope, rope,
                  pl.BlockSpec((1, RET_HEADS, RET_DK, RET_DV), lambda i, t: (i, 0, 0, 0)),
                  full(cw), full(cb), full(clg), full(clb)],
        out_specs=(tile(d), tile(QK_DIM), tile(V_DIM), tile(CONV_DIM),
                   pl.BlockSpec((1, cpt, RET_HEADS, RET_DK, RET_DV), lambda i, t: (i, t, 0, 0, 0))),
        out_shape=(jax.ShapeDtypeStruct((b, l, d), BF16),
                   jax.ShapeDtypeStruct((b, l, QK_DIM), BF16),
                   jax.ShapeDtypeStruct((b, l, V_DIM), BF16),
                   jax.ShapeDtypeStruct((b, l, CONV_DIM), BF16),
                   jax.ShapeDtypeStruct((b, l // RET_CHUNK, RET_HEADS, RET_DK, RET_DV), BF16)),
        scratch_shapes=[pltpu.VMEM((RET_HEADS, RET_DK, RET_DV), F32),
                        pltpu.VMEM((ext, CONV_DIM), F32),
                        pltpu.VMEM((8, ext - 8, CONV_DIM), F32),
                        pltpu.VMEM((PROJ_ROWS, CONV_DIM), F32)],
        compiler_params=_params(("parallel", "arbitrary")),
        name="proj_kv_conv",
    )(lgf, x, x, x, sh1, sc1, w_kvu, ck, sk, sf0, cw, cb, clg, clb)


def _mixer_kernel(alpha, lgf_ref, lgb_ref, x_ref, h_ref, k_ref, v_ref, sf_ref, sb0_ref, ya_ref,
                  wq_ref, wg_ref, wco_ref, wro_ref, wo_ref, gng_ref,
                  g1_ref, sh2_ref, sc2_ref, l1g_ref, l1b_ref, wr_ref, cq_ref, sq_ref,
                  x1_ref, hp_ref, lt_ref, sb_scr):
    t = pl.program_id(1)

    @pl.when(t == 0)
    def _():
        sb_scr[...] = sb0_ref[0]

    h = h_ref[0]
    rows = h.shape[0]
    cq = cq_ref[...]
    sq = sq_ref[...]
    q_all = _dot(h, wq_ref[...])
    chunk = RET_CHUNK
    pos = _col_iota(chunk)
    di = lax.broadcasted_iota(jnp.int32, (chunk, chunk), 0)
    dj = lax.broadcasted_iota(jnp.int32, (chunk, chunk), 1)
    dist = (di - dj).astype(F32)

    n_chunks = rows // chunk
    o_parts = [[None] * RET_HEADS for _ in range(n_chunks)]
    for hh in range(RET_HEADS):
        lgf = lgf_ref[hh]
        lgb = lgb_ref[hh]
        qh = q_all[:, hh * RET_DK:(hh + 1) * RET_DK]
        qr = qh * cq + pltpu.roll(qh, RET_DK // 2, 1) * sq
        decay = jnp.where(dist >= 0.0, jnp.exp(lgf * jnp.maximum(dist, 0.0)),
                          jnp.exp(lgb * jnp.maximum(-dist, 0.0)))
        cross_f = jnp.exp(lgf * (pos + 1.0))
        cross_b = jnp.exp(lgb * (chunk - pos))
        state_b = jnp.exp(lgb * pos)
        chunk_b = jnp.exp(lgb * jnp.full((1, RET_DV), float(chunk), F32))
        for c in reversed(range(n_chunks)):
            r0 = c * chunk
            qc = qr[r0:r0 + chunk]
            kc = k_ref[0, r0:r0 + chunk, hh * RET_DK:(hh + 1) * RET_DK]
            vc = v_ref[0, r0:r0 + chunk, hh * RET_DV:(hh + 1) * RET_DV]
            sb = sb_scr[hh]
            att = _dot_nt(qc.astype(BF16), kc) * decay
            lhs = jnp.concatenate([att.astype(BF16), (qc * cross_f).astype(BF16), (qc * cross_b).astype(BF16)], axis=1)
            rhs = jnp.concatenate([vc, sf_ref[0, c, hh], sb.astype(BF16)], axis=0)
            o = _dot(lhs, rhs)
            kb = (kc.astype(F32) * state_b).astype(BF16)
            sb_scr[hh] = sb * chunk_b + _dot_tn(kb, vc)
            o_parts[c][hh] = _norm(o)
    on = jnp.concatenate([jnp.concatenate(o_parts[c], axis=1) for c in range(n_chunks)], axis=0)

    d = x_ref.shape[-1]
    pw = d // (2 * PACK_PARTS)
    wr_hi, wr_lo = _split(wr_ref[...])
    g_ret = _dot(h, wg_ref[:, :V_DIM])
    yb_in = (_silu(g_ret) * (on * gng_ref[...])).astype(BF16)
    y_b = _dot(yb_in, wro_ref[...])
    y_a = _dot(ya_ref[0], wco_ref[...])
    g_a = _dot(h, wg_ref[:, V_DIM:V_DIM + d])
    g_b = _dot(h, wg_ref[:, V_DIM + d:])
    y = (_sigmoid(g_a) * y_a + _sigmoid(g_b) * y_b).astype(BF16)
    ym = _dot(y, wo_ref[...])

    x1 = _norm(alpha * x_ref[0] + g1_ref[0] * ym) * l1g_ref[...] + l1b_ref[...]
    x1_ref[0] = x1
    h2 = _norm(x1) * (1.0 + sc2_ref[0]) + sh2_ref[0]
    h2_hi, h2_lo = _split(h2)
    for p in range(PACK_PARTS):
        c0 = 2 * p * pw
        hp_ref[p, 0] = _pack_pair(h2_hi[:, c0:c0 + pw], h2_hi[:, c0 + pw:c0 + 2 * pw])
    n_e = wr_hi.shape[0]
    both = _dot_nt(jnp.concatenate([wr_hi, wr_lo], axis=0), h2_hi)
    lt_ref[0] = both[:n_e] + (_dot_nt(wr_hi, h2_lo) + both[n_e:])


def _mixer(alpha, lgf, lgb, x, b0, h, k, v, sf, sb0, ya, wq, wg, wco, wro, wo, gng,
           g1, sh2, sc2, l1g, l1b, wr_t, cq, sq):
    b, l, d = h.shape
    nt = l // MIX_ROWS
    cpt = MIX_ROWS // RET_CHUNK
    pw = d // (2 * PACK_PARTS)
    rev = lambda w: pl.BlockSpec((1, MIX_ROWS, w), lambda i, t: (i, nt - 1 - t, 0))
    vec = pl.BlockSpec((1, 1, d), lambda i, t: (i, 0, 0))
    full = lambda a: pl.BlockSpec(a.shape, lambda i, t: (0,) * a.ndim, pipeline_mode=pl.Buffered(1))
    rope = pl.BlockSpec((MIX_ROWS, RET_DK), lambda i, t: (nt - 1 - t, 0))
    state = pl.BlockSpec((1, RET_HEADS, RET_DK, RET_DV), lambda i, t: (i, 0, 0, 0))
    return pl.pallas_call(
        functools.partial(_mixer_kernel, alpha),
        grid=(b, nt),
        in_specs=[_smem(), _smem(),
                  pl.BlockSpec((1, MIX_ROWS, d), lambda i, t: (i + b0, nt - 1 - t, 0)),
                  rev(d), rev(QK_DIM), rev(V_DIM),
                  pl.BlockSpec((1, cpt, RET_HEADS, RET_DK, RET_DV), lambda i, t: (i, nt - 1 - t, 0, 0, 0)),
                  state, rev(CONV_DIM),
                  full(wq), full(wg), full(wco), full(wro), full(wo), full(gng),
                  vec, vec, vec, full(l1g), full(l1b), full(wr_t), rope, rope],
        out_specs=(rev(d),
                   pl.BlockSpec((PACK_PARTS, 1, MIX_ROWS, pw), lambda i, t: (0, i, nt - 1 - t, 0)),
                   pl.BlockSpec((1, N_EXPERTS, MIX_ROWS), lambda i, t: (i, 0, nt - 1 - t))),
        out_shape=(jax.ShapeDtypeStruct((b, l, d), F32),
                   jax.ShapeDtypeStruct((PACK_PARTS, b, l, pw), U32),
                   jax.ShapeDtypeStruct((b, N_EXPERTS, l), F32)),
        scratch_shapes=[pltpu.VMEM((RET_HEADS, RET_DK, RET_DV), F32)],
        compiler_params=_params(("parallel", "arbitrary")),
        name="mixer",
    )(lgf, lgb, x, h, k, v, sf, sb0, ya, wq, wg, wco, wro, wo, gng,
      g1, sh2, sc2, l1g, l1b, wr_t, cq, sq)


def _route_kernel(cap, lt_ref, rank_ref, aff_ref, idx_ref):
    logits = lt_ref[0]
    n_e, l = logits.shape
    m = jnp.max(logits, axis=0, keepdims=True)
    p = jnp.exp(logits - m)
    aff = p / jnp.sum(p, axis=0, keepdims=True)
    aff_ref[0] = aff
    capf = float(cap)

    def count(ones):
        return jnp.sum(ones, axis=1, keepdims=True)

    def refine(lo, shift, width, keep):
        best = lo
        for digit in range(1, 1 << width):
            cand = lo | jnp.left_shift(jnp.int32(digit), shift)
            best = jnp.where(keep(cand), cand, best)
        return best

    def bisect(n_bits, keep):
        lo = jnp.zeros((n_e, 1), jnp.int32)
        if n_bits % 2:
            lo = refine(lo, n_bits - 1, 1, keep)
        return lax.fori_loop(0, n_bits // 2, lambda i, v: refine(v, 2 * (n_bits // 2 - 1 - i), 2, keep), lo)

    thr = pltpu.bitcast(
        bisect(31, lambda cand: count(jnp.where(aff >= pltpu.bitcast(cand, F32), 1.0, 0.0)) >= capf), F32)
    gt = jnp.where(aff > thr, 1.0, 0.0)
    eq = jnp.where(aff == thr, 1.0, 0.0)
    need = capf - count(gt)
    idx = lax.broadcasted_iota(jnp.int32, (n_e, l), 1)
    idx_bits = int(l - 1).bit_length()

    last = bisect(idx_bits, lambda cand: count(jnp.where(idx < cand, eq, 0.0)) < need)
    bound = jnp.where(need > 0.0, last + 1, 0)
    sel = gt + jnp.where(idx < bound, eq, 0.0)

    blk = 128
    ti = lax.broadcasted_iota(jnp.int32, (blk, blk), 0)
    tj = lax.broadcasted_iota(jnp.int32, (blk, blk), 1)
    tri = jnp.where(ti < tj, 1.0, 0.0).astype(BF16)
    offset = jnp.zeros((n_e, 1), F32)
    for j in range(l // blk):
        sj = sel[:, j * blk:(j + 1) * blk]
        before = _dot(sj.astype(BF16), tri) + offset
        rank_ref[0, :, j * blk:(j + 1) * blk] = jnp.where(sj > 0.0, before, -1.0).astype(jnp.int32)
        offset = offset + jnp.sum(sj, axis=1, keepdims=True)

    rank = rank_ref[0]
    hi_digit = jnp.right_shift(rank, SLOT_SHIFT)
    lo_digit = jnp.bitwise_and(rank, SLOT_RADIX - 1)
    digit = lax.broadcasted_iota(jnp.int32, (SLOT_RADIX, l), 0)
    tok_hi = jnp.right_shift(idx[:1], TOKEN_SHIFT).astype(F32)
    tok_lo = jnp.bitwise_and(idx[:1], (1 << TOKEN_SHIFT) - 1).astype(F32)
    a_hi, a_lo, b_rows = [], [], []
    for e in range(n_e):
        is_a = hi_digit[e:e + 1] == digit
        a_hi.append(jnp.where(is_a, tok_hi, 0.0))
        a_lo.append(jnp.where(is_a, tok_lo, 0.0))
        b_rows.append(jnp.where(lo_digit[e:e + 1] == digit, 1.0, 0.0))
    b_all = jnp.concatenate(b_rows, axis=0).astype(BF16)
    cross_hi = _dot_nt(jnp.concatenate(a_hi, axis=0).astype(BF16), b_all)
    cross_lo = _dot_nt(jnp.concatenate(a_lo, axis=0).astype(BF16), b_all)
    cross = cross_hi * float(1 << TOKEN_SHIFT) + cross_lo + (pl.program_id(0) * l).astype(F32)
    for e in range(n_e):
        s0 = e * SLOT_RADIX
        idx_ref[0, e] = cross[s0:s0 + SLOT_RADIX, s0:s0 + SLOT_RADIX].astype(jnp.int32)


def _route(lt, cap):
    b, n_e, l = lt.shape
    assert cap == SLOT_RADIX * SLOT_RADIX and l <= (1 << TOKEN_SHIFT) * 256
    spec = pl.BlockSpec((1, n_e, l), lambda i: (i, 0, 0))
    return pl.pallas_call(
        functools.partial(_route_kernel, cap),
        grid=(b,),
        in_specs=[spec],
        out_specs=(spec, spec, pl.BlockSpec((1, n_e, SLOT_RADIX, SLOT_RADIX), lambda i: (i, 0, 0, 0))),
        out_shape=(jax.ShapeDtypeStruct((b, n_e, l), jnp.int32),
                   jax.ShapeDtypeStruct((b, n_e, l), F32),
                   jax.ShapeDtypeStruct((b, n_e, SLOT_RADIX, SLOT_RADIX), jnp.int32)),
        compiler_params=_params(("parallel",)),
        name="route_topc",
    )(lt)


def _sc_gather(rows, idx):
    n = idx.shape[0]
    w = rows.shape[1]
    mesh = plsc.VectorSubcoreMesh(core_axis_name="c", subcore_axis_name="s")

    @pl.kernel(out_type=jax.ShapeDtypeStruct((n, w), rows.dtype), mesh=mesh, scratch_types=[])
    def gather(rows_hbm, idx_hbm, out_hbm):
        def window(idx_vmem, out_vmem):
            pltpu.sync_copy(rows_hbm.at[idx_vmem.at[0]], out_vmem)

        pltpu.emit_pipeline(
            window,
            grid=(n // SC_WINDOW,),
            in_specs=[pl.BlockSpec((1, SC_WINDOW), index_map=lambda i: (0, i))],
            out_specs=[pl.BlockSpec((SC_WINDOW, w), index_map=lambda i: (i, 0))],
            core_axis_name=("c", "s"),
            dimension_semantics=(pltpu.PARALLEL,),
        )(idx_hbm, out_hbm)

    return gather(rows, idx.reshape(1, n))


def _ffn_kernel(xe_ref, wg_ref, wu_ref, wd_ref, ye_ref, wg_scr, wu_scr, wd_scr):
    @pl.when(pl.program_id(1) == 0)
    def _():
        wg_scr[...] = wg_ref[0].astype(BF16)
        wu_scr[...] = wu_ref[0].astype(BF16)
        wd_scr[...] = wd_ref[0].astype(BF16)

    halves = []
    for p in range(PACK_PARTS):
        halves.extend(_unpack_pair(xe_ref[p, 0]))
    xe = jnp.concatenate(halves, axis=1)
    he = (_silu(_dot(xe, wg_scr[...])) * _dot(xe, wu_scr[...])).astype(BF16)
    ye_ref[0] = _dot(he, wd_scr[...]).astype(BF16)


def _ffn(xe, wgate, wup, wdown):
    _, n_e, rows, w = xe.shape
    _, d, f = wgate.shape
    tile = min(FFN_ROWS, rows)
    return pl.pallas_call(
        _ffn_kernel,
        grid=(n_e, rows // tile),
        in_specs=[pl.BlockSpec((PACK_PARTS, 1, tile, w), lambda e, m: (0, e, m, 0)),
                  pl.BlockSpec((1, d, f), lambda e, m: (e, 0, 0)),
                  pl.BlockSpec((1, d, f), lambda e, m: (e, 0, 0)),
                  pl.BlockSpec((1, f, d), lambda e, m: (e, 0, 0))],
        out_specs=pl.BlockSpec((1, tile, d), lambda e, m: (e, m, 0)),
        out_shape=jax.ShapeDtypeStruct((n_e, rows, d), BF16),
        scratch_shapes=[pltpu.VMEM((d, f), BF16), pltpu.VMEM((d, f), BF16), pltpu.VMEM((f, d), BF16)],
        compiler_params=_params(("parallel", "arbitrary")),
        name="expert_ffn",
    )(xe, wgate, wup, wdown)


def _combine_kernel(cap, alpha, rank_ref, aff_ref, ye_ref, x1_ref, g2_ref, lng_ref, lnb_ref, *rest):
    o_ref = rest[-1]
    n_e = rank_ref.shape[1]
    slot = lax.broadcasted_iota(jnp.int32, (cap, OUT_ROWS), 0)
    acc = None
    for e in range(n_e):
        hit = slot == rank_ref[0, e:e + 1, :]
        gated = jnp.where(hit, aff_ref[0, e:e + 1, :], 0.0).astype(BF16)
        part = _dot_tn(gated, ye_ref[e, 0])
        acc = part if acc is None else acc + part
    z = alpha * x1_ref[0] + g2_ref[0] * acc
    o_ref[0] = _norm(z) * lng_ref[...] + lnb_ref[...]


def _combine(alpha, rank, aff, ye, x1, g2, ln_g, ln_b, cap, b_total, b0, earlier):
    b, l, d = x1.shape
    n_e = rank.shape[1]
    row = pl.BlockSpec((1, n_e, OUT_ROWS), lambda i, t: (i, 0, t))
    vec = pl.BlockSpec((1, d), lambda i, t: (0, 0))
    in_specs = [row, row,
                pl.BlockSpec((n_e, 1, cap, d), lambda i, t: (0, i, 0, 0)),
                pl.BlockSpec((1, OUT_ROWS, d), lambda i, t: (i, t, 0)),
                pl.BlockSpec((1, 1, d), lambda i, t: (i, 0, 0)), vec, vec]
    args = [rank, aff, ye.reshape(n_e, b, cap, d), x1, g2, ln_g, ln_b]
    aliases = {}
    if earlier is not None:
        in_specs.append(pl.BlockSpec(memory_space=pl.ANY))
        aliases = {len(args): 0}
        args.append(earlier)
    return pl.pallas_call(
        functools.partial(_combine_kernel, cap, alpha),
        grid=(b, l // OUT_ROWS),
        in_specs=in_specs,
        out_specs=pl.BlockSpec((1, OUT_ROWS, d), lambda i, t: (i + b0, t, 0)),
        out_shape=jax.ShapeDtypeStruct((b_total, l, d), F32),
        input_output_aliases=aliases,
        compiler_params=_params(("parallel", "arbitrary")),
        name="expert_combine",
    )(*args)


def _rope_tables(l):
    n_axis = RET_DK // 4
    freqs = ROPE_BASE ** (-np.arange(n_axis, dtype=np.float64) / n_axis)
    pos = np.arange(l)
    ang = np.concatenate([(pos // GRID_W)[:, None] * freqs, (pos % GRID_W)[:, None] * freqs], axis=-1)
    cos, sin = np.cos(ang), np.sin(ang)
    return (np.concatenate([cos, cos], axis=-1).astype(np.float32),
            np.concatenate([-sin, sin], axis=-1).astype(np.float32))


def kernel(x, c, ctx, c_ctx, w_ada, b_ada, w_in, conv_w, conv_b, conv_ln_g, conv_ln_b, w_conv_out,
           log_decay_f, log_decay_b, ret_gn_g, w_ret_out, w_out, ln1_g, ln1_b,
           w_router, w_gate, w_up, w_down, ln2_g, ln2_b):
    depth = w_ada.shape[0]
    assert depth == 1, "single trunk layer"
    b, l, d = x.shape
    alpha = (2.0 * depth) ** 0.25
    cap = EC_FACTOR * l // N_EXPERTS
    u_end = 2 * CONV_DIM
    q_end = u_end + QK_DIM
    k_end = q_end + QK_DIM
    v_end = k_end + V_DIM
    row = lambda a: a.reshape(1, -1)

    n_mod = b + 1
    pad = (-n_mod) % 8
    cc = jnp.concatenate([c, c_ctx[None], jnp.zeros((pad, d), F32)], axis=0)
    mod = _ada(cc, w_ada[0], row(b_ada[0]))
    sh1, sc1, g1, sh2, sc2, g2 = [m.reshape(b, 1, d) for m in jnp.split(mod[:b], 6, axis=-1)]
    csh1, csc1 = mod[b:b + 1, :d], mod[b:b + 1, d:2 * d]

    perm = np.concatenate([np.arange(0, RET_DK, 2), np.arange(1, RET_DK, 2)])
    perm = (np.arange(RET_HEADS)[:, None] * RET_DK + perm[None, :]).reshape(-1)
    w = w_in[0]
    w_kvu = jnp.concatenate([w[:, q_end:k_end][:, perm], w[:, k_end:v_end], w[:, :u_end]], axis=1).astype(BF16)
    wq = w[:, u_end:q_end][:, perm].astype(BF16)
    wg = w[:, v_end:].astype(BF16)

    cos_t, sin_t = _rope_tables(l)
    cq, sq = jnp.asarray(cos_t), jnp.asarray(sin_t)
    k_scale = RET_DK ** -0.5
    ck, sk = jnp.asarray(cos_t * k_scale), jnp.asarray(sin_t * k_scale)

    lgf, lgb = log_decay_f[0], log_decay_b[0]
    sf0, sb0 = _ctx_states(lgf, lgb, ctx, csh1, csc1, w_kvu)
    cw8 = jnp.repeat(conv_w[0], 8, axis=0)
    wco, wro, wo = w_conv_out[0].astype(BF16), w_ret_out[0].astype(BF16), w_out[0].astype(BF16)
    wge, wue, wde = w_gate[0], w_up[0], w_down[0]
    wr_t = w_router[0].T
    groups = BATCH_GROUPS if b % BATCH_GROUPS == 0 else 1
    nb = b // groups
    out = None
    for gi in range(groups):
        b0 = gi * nb
        part = lambda a: a[b0:b0 + nb]
        h, k, v, ya, sf = _proj(lgf, x, b0, part(sh1), part(sc1), w_kvu, ck, sk, part(sf0),
                                cw8, row(conv_b[0]), row(conv_ln_g[0]), row(conv_ln_b[0]))
        x1, hp, lt = _mixer(alpha, lgf, lgb, x, b0, h, k, v, sf, part(sb0), ya, wq, wg, wco, wro, wo,
                            row(ret_gn_g[0]), part(g1), part(sh2), part(sc2), row(ln1_g[0]), row(ln1_b[0]),
                            wr_t, cq, sq)
        rank, aff, slots = _route(lt, cap)
        part_rows = nb * l
        by_expert = slots.reshape(nb, N_EXPERTS, cap).transpose(1, 0, 2).reshape(-1)
        ids = jnp.concatenate([by_expert + p * part_rows for p in range(PACK_PARTS)])
        xe = _sc_gather(hp.reshape(PACK_PARTS * part_rows, hp.shape[-1]), ids)
        ye = _ffn(xe.reshape(PACK_PARTS, N_EXPERTS, nb * cap, hp.shape[-1]), wge, wue, wde)
        out = _combine(alpha, rank, aff, ye, x1, part(g2), row(ln2_g[0]), row(ln2_b[0]), cap, b, b0, out)
    return out
```

```python
import functools

import numpy as np
import jax
import jax.numpy as jnp
from jax import lax
from jax.experimental import pallas as pl
from jax.experimental.pallas import tpu as pltpu
from jax.experimental.pallas import tpu_sc as plsc

F32 = jnp.float32
BF16 = jnp.bfloat16
U32 = jnp.uint32

GRID_W = 64
CONV_DIM = 512
CONV_WIDTH = 31
CONV_HALO = CONV_WIDTH // 2
RET_HEADS = 4
RET_DK = 128
RET_DV = 256
RET_CHUNK = 256
ROPE_BASE = 10000.0
QK_DIM = RET_HEADS * RET_DK
V_DIM = RET_HEADS * RET_DV
N_EXPERTS = 16
EC_FACTOR = 2
LN_EPS = 1e-5

ADA_COLS = 768
PROJ_ROWS = 512
MIX_ROWS = 512
BATCH_GROUPS = 2
CONV_PAD = 16
CONV_BLOCK = 64
OUT_ROWS = 512
FFN_ROWS = 1024
PACK_PARTS = 2
SC_WINDOW = 128
SLOT_SHIFT = 4
SLOT_RADIX = 1 << SLOT_SHIFT
TOKEN_SHIFT = 6
V7X_VMEM_LIMIT = 56 * 1024 * 1024


def _dot(a, b):
    return jnp.dot(a, b, preferred_element_type=F32)


def _dot_nt(a, b):
    return lax.dot_general(a, b, (((1,), (1,)), ((), ())), preferred_element_type=F32)


def _dot_tn(a, b):
    return lax.dot_general(a, b, (((0,), (0,)), ((), ())), preferred_element_type=F32)


def _split(a):
    hi = a.astype(BF16)
    lo = (a - hi.astype(F32)).astype(BF16)
    return hi, lo


def _norm(x):
    mu = jnp.mean(x, axis=-1, keepdims=True)
    xc = x - mu
    var = jnp.mean(xc * xc, axis=-1, keepdims=True)
    return xc * lax.rsqrt(var + LN_EPS)


def _sigmoid(x):
    return 1.0 / (1.0 + jnp.exp(-x))


def _silu(x):
    return x * _sigmoid(x)


def _col_iota(n):
    return lax.broadcasted_iota(jnp.int32, (n, 1), 0).astype(F32)


def _pack_pair(lo, hi):
    lo_bits = pltpu.bitcast(lo.astype(F32), U32)
    hi_bits = pltpu.bitcast(hi.astype(F32), U32)
    return lax.shift_right_logical(lo_bits, jnp.uint32(16)) | hi_bits


def _unpack_pair(word):
    lo = pltpu.bitcast(lax.shift_left(word, jnp.uint32(16)), F32)
    hi = pltpu.bitcast(word & jnp.uint32(0xFFFF0000), F32)
    return lo.astype(BF16), hi.astype(BF16)


def _smem():
    return pl.BlockSpec(memory_space=pltpu.SMEM)


def _params(sem, vmem=V7X_VMEM_LIMIT):
    return pltpu.CompilerParams(dimension_semantics=sem, vmem_limit_bytes=vmem)


def _ada_kernel(c_ref, w_ref, b_ref, o_ref):
    a_hi, a_lo = _split(_silu(c_ref[...]))
    w_hi, w_lo = _split(w_ref[...])
    o_ref[...] = _dot(a_hi, w_hi) + (_dot(a_hi, w_lo) + _dot(a_lo, w_hi)) + b_ref[...]


def _ada(cc, w, b):
    m, d = cc.shape
    n = w.shape[1]
    return pl.pallas_call(
        _ada_kernel,
        grid=(n // ADA_COLS,),
        in_specs=[pl.BlockSpec((m, d), lambda j: (0, 0)),
                  pl.BlockSpec((d, ADA_COLS), lambda j: (0, j)),
                  pl.BlockSpec((1, ADA_COLS), lambda j: (0, j))],
        out_specs=pl.BlockSpec((m, ADA_COLS), lambda j: (0, j)),
        out_shape=jax.ShapeDtypeStruct((m, n), F32),
        compiler_params=_params(("parallel",)),
        name="ada_proj",
    )(cc, w, b)


def _ctx_kernel(lgf_ref, lgb_ref, ctx_ref, sh_ref, sc_ref, w_ref, sf_ref, sb_ref):
    x = ctx_ref[0]
    lc = x.shape[0]
    h = (_norm(x) * (1.0 + sc_ref[...]) + sh_ref[...]).astype(BF16)
    kv = _dot(h, w_ref[...])
    t = _col_iota(lc)
    for hh in range(RET_HEADS):
        k = kv[:, hh * RET_DK:(hh + 1) * RET_DK] * (RET_DK ** -0.5)
        v = kv[:, QK_DIM + hh * RET_DV:QK_DIM + (hh + 1) * RET_DV].astype(BF16)
        wf = jnp.exp(lgf_ref[hh] * (lc - 1.0 - t))
        wb = jnp.exp(lgb_ref[hh] * t)
        both = _dot_tn(jnp.concatenate([(k * wf).astype(BF16), (k * wb).astype(BF16)], axis=1), v)
        sf_ref[0, hh] = both[:RET_DK]
        sb_ref[0, hh] = both[RET_DK:]


def _ctx_states(lgf, lgb, ctx, csh, csc, w_kvu):
    b, lc, d = ctx.shape
    kvw = QK_DIM + V_DIM
    st = jax.ShapeDtypeStruct((b, RET_HEADS, RET_DK, RET_DV), F32)
    st_spec = pl.BlockSpec((1, RET_HEADS, RET_DK, RET_DV), lambda i: (i, 0, 0, 0))
    return pl.pallas_call(
        _ctx_kernel,
        grid=(b,),
        in_specs=[_smem(), _smem(),
                  pl.BlockSpec((1, lc, d), lambda i: (i, 0, 0)),
                  pl.BlockSpec((1, d), lambda i: (0, 0)),
                  pl.BlockSpec((1, d), lambda i: (0, 0)),
                  pl.BlockSpec((d, kvw), lambda i: (0, 0))],
        out_specs=(st_spec, st_spec),
        out_shape=(st, st),
        compiler_params=_params(("parallel",)),
        name="ctx_states",
    )(lgf, lgb, ctx, csh, csc, w_kvu)


def _proj_kernel(lgf_ref, lgb_ref, xp_ref, x_ref, xn_ref, sh_ref, sc_ref, w_ref, ck_ref, sk_ref, sf0_ref,
                 cw_ref, cb_ref, clg_ref, clb_ref,
                 h_ref, k_ref, v_ref, ya_ref, sf_ref, ub_ref, s_scr, y_scr, yres_scr, z_scr):
    t = pl.program_id(1)
    nt = pl.num_programs(1)

    @pl.when(t == 0)
    def _():
        s_scr[...] = sf0_ref[0]

    rows = x_ref.shape[1]
    ext = rows + 2 * CONV_PAD
    x_ext = jnp.concatenate([xp_ref[0], x_ref[0], xn_ref[0]], axis=0)
    h_ext = (_norm(x_ext) * (1.0 + sc_ref[0]) + sh_ref[0]).astype(BF16)
    h = h_ext[CONV_PAD:CONV_PAD + rows]
    h_ref[0] = h
    kvw = QK_DIM + V_DIM
    u = _dot(h_ext, w_ref[:, kvw:])
    row = lax.broadcasted_iota(jnp.int32, (ext, 1), 0)
    head_ok = jnp.where(t == 0, 0.0, 1.0)
    tail_ok = jnp.where(t == nt - 1, 0.0, 1.0)
    inside = jnp.where(row < CONV_PAD, head_ok, jnp.where(row >= CONV_PAD + rows, tail_ok, 1.0))
    y_scr[...] = u[:, :CONV_DIM] * _sigmoid(u[:, CONV_DIM:]) * inside

    kk = _dot(h, w_ref[:, :QK_DIM])
    vv = _dot(h, w_ref[:, QK_DIM:kvw]).astype(BF16)
    v_ref[0] = vv
    ck = ck_ref[...]
    sk = sk_ref[...]
    pos = _col_iota(RET_CHUNK)
    for hh in range(RET_HEADS):
        lg = lgf_ref[hh]
        kh = kk[:, hh * RET_DK:(hh + 1) * RET_DK]
        kr = kh * ck + pltpu.roll(kh, RET_DK // 2, 1) * sk
        k_ref[0, :, hh * RET_DK:(hh + 1) * RET_DK] = kr.astype(BF16)
        state_dec = jnp.exp(lg * (RET_CHUNK - 1.0 - pos))
        state_back = jnp.exp(lgb_ref[hh] * pos)
        chunk_dec = jnp.exp(lg * jnp.full((1, RET_DV), float(RET_CHUNK), F32))
        for c in range(rows // RET_CHUNK):
            r0 = c * RET_CHUNK
            s = s_scr[hh]
            sf_ref[0, c, hh] = s.astype(BF16)
            kc = kr[r0:r0 + RET_CHUNK]
            vc = vv[r0:r0 + RET_CHUNK, hh * RET_DV:(hh + 1) * RET_DV]
            both = _dot_tn(jnp.concatenate([(kc * state_dec).astype(BF16), (kc * state_back).astype(BF16)], axis=1), vc)
            s_scr[hh] = s * chunk_dec + both[:RET_DK]
            ub_ref[0, c, hh] = both[RET_DK:]

    span = ext - 8
    for r in range(8):
        yres_scr[r] = y_scr[pl.ds(r, span), :]

    for i in range(rows // CONV_BLOCK):
        base = i * CONV_BLOCK
        acc = None
        for w in range(CONV_WIDTH):
            a, r = divmod(w + 1, 8)
            wk = jnp.concatenate([cw_ref[8 * w:8 * w + 8, :]] * (CONV_BLOCK // 8), axis=0)
            tap = yres_scr[r, base + 8 * a:base + 8 * a + CONV_BLOCK, :] * wk
            acc = tap if acc is None else acc + tap
        z_scr[base:base + CONV_BLOCK, :] = acc
    z = _norm(z_scr[...] + cb_ref[...]) * clg_ref[...] + clb_ref[...]
    ya_ref[0] = _silu(z).astype(BF16)


def _proj(lgf, lgb, x, b0, sh1, sc1, w_kvu, ck, sk, sf0, cw, cb, clg, clb):
    _, l, d = x.shape
    b = sh1.shape[0]
    nt = l // PROJ_ROWS
    cpt = PROJ_ROWS // RET_CHUNK
    hpt = PROJ_ROWS // CONV_PAD
    n_halo = l // CONV_PAD
    ext = PROJ_ROWS + 2 * CONV_PAD
    tile = lambda w: pl.BlockSpec((1, PROJ_ROWS, w), lambda i, t: (i, t, 0))
    vec = pl.BlockSpec((1, 1, d), lambda i, t: (i, 0, 0))
    full = lambda a: pl.BlockSpec(a.shape, lambda i, t: (0,) * a.ndim, pipeline_mode=pl.Buffered(1))
    rope = pl.BlockSpec((PROJ_ROWS, RET_DK), lambda i, t: (t, 0))
    x_tile = pl.BlockSpec((1, PROJ_ROWS, d), lambda i, t: (i + b0, t, 0))
    prev = pl.BlockSpec((1, CONV_PAD, d), lambda i, t: (i + b0, jnp.maximum(t * hpt - 1, 0), 0))
    nxt = pl.BlockSpec((1, CONV_PAD, d), lambda i, t: (i + b0, jnp.minimum((t + 1) * hpt, n_halo - 1), 0))
    return pl.pallas_call(
        _proj_kernel,
        grid=(b, nt),
        in_specs=[_smem(), _smem(), prev, x_tile, nxt, vec, vec, full(w_kvu), rope, rope,
                  pl.BlockSpec((1, RET_HEADS, RET_DK, RET_DV), lambda i, t: (i, 0, 0, 0)),
                  full(cw), full(cb), full(clg), full(clb)],
        out_specs=(tile(d), tile(QK_DIM), tile(V_DIM), tile(CONV_DIM),
                   pl.BlockSpec((1, cpt, RET_HEADS, RET_DK, RET_DV), lambda i, t: (i, t, 0, 0, 0)),
                   pl.BlockSpec((1, cpt, RET_HEADS, RET_DK, RET_DV), lambda i, t: (i, t, 0, 0, 0))),
        out_shape=(jax.ShapeDtypeStruct((b, l, d), BF16),
                   jax.ShapeDtypeStruct((b, l, QK_DIM), BF16),
                   jax.ShapeDtypeStruct((b, l, V_DIM), BF16),
                   jax.ShapeDtypeStruct((b, l, CONV_DIM), BF16),
                   jax.ShapeDtypeStruct((b, l // RET_CHUNK, RET_HEADS, RET_DK, RET_DV), BF16),
                   jax.ShapeDtypeStruct((b, l // RET_CHUNK, RET_HEADS, RET_DK, RET_DV), F32)),
        scratch_shapes=[pltpu.VMEM((RET_HEADS, RET_DK, RET_DV), F32),
                        pltpu.VMEM((ext, CONV_DIM), F32),
                        pltpu.VMEM((8, ext - 8, CONV_DIM), F32),
                        pltpu.VMEM((PROJ_ROWS, CONV_DIM), F32)],
        compiler_params=_params(("parallel", "arbitrary")),
        name="proj_kv_conv",
    )(lgf, lgb, x, x, x, sh1, sc1, w_kvu, ck, sk, sf0, cw, cb, clg, clb)


def _mixer_kernel(alpha, lgf_ref, lgb_ref, x_ref, h_ref, k_ref, v_ref, sf_ref, ub_ref, sb0_ref, ya_ref,
                  wq_ref, wg_ref, wco_ref, wro_ref, wo_ref, gng_ref,
                  g1_ref, sh2_ref, sc2_ref, l1g_ref, l1b_ref, wr_ref, cq_ref, sq_ref,
                  x1_ref, hp_ref, lt_ref, sb_scr):
    t = pl.program_id(1)

    @pl.when(t == 0)
    def _():
        sb_scr[...] = sb0_ref[0]

    h = h_ref[0]
    rows = h.shape[0]
    cq = cq_ref[...]
    sq = sq_ref[...]
    q_all = _dot(h, wq_ref[...])
    chunk = RET_CHUNK
    pos = _col_iota(chunk)
    di = lax.broadcasted_iota(jnp.int32, (chunk, chunk), 0)
    dj = lax.broadcasted_iota(jnp.int32, (chunk, chunk), 1)
    dist = (di - dj).astype(F32)

    n_chunks = rows // chunk
    o_parts = [[None] * RET_HEADS for _ in range(n_chunks)]
    for hh in range(RET_HEADS):
        lgf = lgf_ref[hh]
        lgb = lgb_ref[hh]
        qh = q_all[:, hh * RET_DK:(hh + 1) * RET_DK]
        qr = qh * cq + pltpu.roll(qh, RET_DK // 2, 1) * sq
        decay = jnp.where(dist >= 0.0, jnp.exp(lgf * jnp.maximum(dist, 0.0)),
                          jnp.exp(lgb * jnp.maximum(-dist, 0.0)))
        cross_f = jnp.exp(lgf * (pos + 1.0))
        cross_b = jnp.exp(lgb * (chunk - pos))
        chunk_b = jnp.exp(lgb * jnp.full((1, RET_DV), float(chunk), F32))
        for c in reversed(range(n_chunks)):
            r0 = c * chunk
            qc = qr[r0:r0 + chunk]
            kc = k_ref[0, r0:r0 + chunk, hh * RET_DK:(hh + 1) * RET_DK]
            vc = v_ref[0, r0:r0 + chunk, hh * RET_DV:(hh + 1) * RET_DV]
            sb = sb_scr[hh]
            att = _dot_nt(qc.astype(BF16), kc) * decay
            lhs = jnp.concatenate([att.astype(BF16), (qc * cross_f).astype(BF16), (qc * cross_b).astype(BF16)], axis=1)
            rhs = jnp.concatenate([vc, sf_ref[0, c, hh], sb.astype(BF16)], axis=0)
            o = _dot(lhs, rhs)
            sb_scr[hh] = sb * chunk_b + ub_ref[0, c, hh]
            o_parts[c][hh] = _norm(o)
    on = jnp.concatenate([jnp.concatenate(o_parts[c], axis=1) for c in range(n_chunks)], axis=0)

    d = x_ref.shape[-1]
    pw = d // (2 * PACK_PARTS)
    wr_hi, wr_lo = _split(wr_ref[...])
    g_ret = _dot(h, wg_ref[:, :V_DIM])
    yb_in = (_silu(g_ret) * (on * gng_ref[...])).astype(BF16)
    y_b = _dot(yb_in, wro_ref[...])
    y_a = _dot(ya_ref[0], wco_ref[...])
    g_a = _dot(h, wg_ref[:, V_DIM:V_DIM + d])
    g_b = _dot(h, wg_ref[:, V_DIM + d:])
    y = (_sigmoid(g_a) * y_a + _sigmoid(g_b) * y_b).astype(BF16)
    ym = _dot(y, wo_ref[...])

    x1 = _norm(alpha * x_ref[0] + g1_ref[0] * ym) * l1g_ref[...] + l1b_ref[...]
    x1_ref[0] = x1
    h2 = _norm(x1) * (1.0 + sc2_ref[0]) + sh2_ref[0]
    h2_hi, h2_lo = _split(h2)
    for p in range(PACK_PARTS):
        c0 = 2 * p * pw
        hp_ref[p, 0] = _pack_pair(h2_hi[:, c0:c0 + pw], h2_hi[:, c0 + pw:c0 + 2 * pw])
    n_e = wr_hi.shape[0]
    both = _dot_nt(jnp.concatenate([wr_hi, wr_lo], axis=0), h2_hi)
    lt_ref[0] = both[:n_e] + (_dot_nt(wr_hi, h2_lo) + both[n_e:])


def _mixer(alpha, lgf, lgb, x, b0, h, k, v, sf, ub, sb0, ya, wq, wg, wco, wro, wo, gng,
           g1, sh2, sc2, l1g, l1b, wr_t, cq, sq):
    b, l, d = h.shape
    nt = l // MIX_ROWS
    cpt = MIX_ROWS // RET_CHUNK
    pw = d // (2 * PACK_PARTS)
    rev = lambda w: pl.BlockSpec((1, MIX_ROWS, w), lambda i, t: (i, nt - 1 - t, 0))
    vec = pl.BlockSpec((1, 1, d), lambda i, t: (i, 0, 0))
    full = lambda a: pl.BlockSpec(a.shape, lambda i, t: (0,) * a.ndim, pipeline_mode=pl.Buffered(1))
    rope = pl.BlockSpec((MIX_ROWS, RET_DK), lambda i, t: (nt - 1 - t, 0))
    state = pl.BlockSpec((1, RET_HEADS, RET_DK, RET_DV), lambda i, t: (i, 0, 0, 0))
    return pl.pallas_call(
        functools.partial(_mixer_kernel, alpha),
        grid=(b, nt),
        in_specs=[_smem(), _smem(),
                  pl.BlockSpec((1, MIX_ROWS, d), lambda i, t: (i + b0, nt - 1 - t, 0)),
                  rev(d), rev(QK_DIM), rev(V_DIM),
                  pl.BlockSpec((1, cpt, RET_HEADS, RET_DK, RET_DV), lambda i, t: (i, nt - 1 - t, 0, 0, 0)),
                  pl.BlockSpec((1, cpt, RET_HEADS, RET_DK, RET_DV), lambda i, t: (i, nt - 1 - t, 0, 0, 0)),
                  state, rev(CONV_DIM),
                  full(wq), full(wg), full(wco), full(wro), full(wo), full(gng),
                  vec, vec, vec, full(l1g), full(l1b), full(wr_t), rope, rope],
        out_specs=(rev(d),
                   pl.BlockSpec((PACK_PARTS, 1, MIX_ROWS, pw), lambda i, t: (0, i, nt - 1 - t, 0)),
                   pl.BlockSpec((1, N_EXPERTS, MIX_ROWS), lambda i, t: (i, 0, nt - 1 - t))),
        out_shape=(jax.ShapeDtypeStruct((b, l, d), F32),
                   jax.ShapeDtypeStruct((PACK_PARTS, b, l, pw), U32),
                   jax.ShapeDtypeStruct((b, N_EXPERTS, l), F32)),
        scratch_shapes=[pltpu.VMEM((RET_HEADS, RET_DK, RET_DV), F32)],
        compiler_params=_params(("parallel", "arbitrary")),
        name="mixer",
    )(lgf, lgb, x, h, k, v, sf, ub, sb0, ya, wq, wg, wco, wro, wo, gng,
      g1, sh2, sc2, l1g, l1b, wr_t, cq, sq)


def _route_kernel(cap, lt_ref, rank_ref, aff_ref, idx_ref):
    logits = lt_ref[0]
    n_e, l = logits.shape
    m = jnp.max(logits, axis=0, keepdims=True)
    p = jnp.exp(logits - m)
    aff = p / jnp.sum(p, axis=0, keepdims=True)
    aff_ref[0] = aff
    capf = float(cap)

    def count(ones):
        return jnp.sum(ones, axis=1, keepdims=True)

    def refine(lo, shift, width, keep):
        best = lo
        for digit in range(1, 1 << width):
            cand = lo | jnp.left_shift(jnp.int32(digit), shift)
            best = jnp.where(keep(cand), cand, best)
        return best

    def bisect(n_bits, keep):
        lo = jnp.zeros((n_e, 1), jnp.int32)
        if n_bits % 2:
            lo = refine(lo, n_bits - 1, 1, keep)
        return lax.fori_loop(0, n_bits // 2, lambda i, v: refine(v, 2 * (n_bits // 2 - 1 - i), 2, keep), lo)

    thr = pltpu.bitcast(
        bisect(31, lambda cand: count(jnp.where(aff >= pltpu.bitcast(cand, F32), 1.0, 0.0)) >= capf), F32)
    gt = jnp.where(aff > thr, 1.0, 0.0)
    eq = jnp.where(aff == thr, 1.0, 0.0)
    need = capf - count(gt)
    idx = lax.broadcasted_iota(jnp.int32, (n_e, l), 1)
    idx_bits = int(l - 1).bit_length()

    last = bisect(idx_bits, lambda cand: count(jnp.where(idx < cand, eq, 0.0)) < need)
    bound = jnp.where(need > 0.0, last + 1, 0)
    sel = gt + jnp.where(idx < bound, eq, 0.0)

    blk = 128
    ti = lax.broadcasted_iota(jnp.int32, (blk, blk), 0)
    tj = lax.broadcasted_iota(jnp.int32, (blk, blk), 1)
    tri = jnp.where(ti < tj, 1.0, 0.0).astype(BF16)
    offset = jnp.zeros((n_e, 1), F32)
    for j in range(l // blk):
        sj = sel[:, j * blk:(j + 1) * blk]
        before = _dot(sj.astype(BF16), tri) + offset
        rank_ref[0, :, j * blk:(j + 1) * blk] = jnp.where(sj > 0.0, before, -1.0).astype(jnp.int32)
        offset = offset + jnp.sum(sj, axis=1, keepdims=True)

    rank = rank_ref[0]
    hi_digit = jnp.right_shift(rank, SLOT_SHIFT)
    lo_digit = jnp.bitwise_and(rank, SLOT_RADIX - 1)
    digit = lax.broadcasted_iota(jnp.int32, (SLOT_RADIX, l), 0)
    tok_hi = jnp.right_shift(idx[:1], TOKEN_SHIFT).astype(F32)
    tok_lo = jnp.bitwise_and(idx[:1], (1 << TOKEN_SHIFT) - 1).astype(F32)
    a_hi, a_lo, b_rows = [], [], []
    for e in range(n_e):
        is_a = hi_digit[e:e + 1] == digit
        a_hi.append(jnp.where(is_a, tok_hi, 0.0))
        a_lo.append(jnp.where(is_a, tok_lo, 0.0))
        b_rows.append(jnp.where(lo_digit[e:e + 1] == digit, 1.0, 0.0))
    b_all = jnp.concatenate(b_rows, axis=0).astype(BF16)
    cross_hi = _dot_nt(jnp.concatenate(a_hi, axis=0).astype(BF16), b_all)
    cross_lo = _dot_nt(jnp.concatenate(a_lo, axis=0).astype(BF16), b_all)
    cross = cross_hi * float(1 << TOKEN_SHIFT) + cross_lo + (pl.program_id(0) * l).astype(F32)
    for e in range(n_e):
        s0 = e * SLOT_RADIX
        idx_ref[0, e] = cross[s0:s0 + SLOT_RADIX, s0:s0 + SLOT_RADIX].astype(jnp.int32)


def _route(lt, cap):
    b, n_e, l = lt.shape
    assert cap == SLOT_RADIX * SLOT_RADIX and l <= (1 << TOKEN_SHIFT) * 256
    spec = pl.BlockSpec((1, n_e, l), lambda i: (i, 0, 0))
    return pl.pallas_call(
        functools.partial(_route_kernel, cap),
        grid=(b,),
        in_specs=[spec],
        out_specs=(spec, spec, pl.BlockSpec((1, n_e, SLOT_RADIX, SLOT_RADIX), lambda i: (i, 0, 0, 0))),
        out_shape=(jax.ShapeDtypeStruct((b, n_e, l), jnp.int32),
                   jax.ShapeDtypeStruct((b, n_e, l), F32),
                   jax.ShapeDtypeStruct((b, n_e, SLOT_RADIX, SLOT_RADIX), jnp.int32)),
        compiler_params=_params(("parallel",)),
        name="route_topc",
    )(lt)


def _sc_gather(rows, idx):
    n = idx.shape[0]
    w = rows.shape[1]
    mesh = plsc.VectorSubcoreMesh(core_axis_name="c", subcore_axis_name="s")

    @pl.kernel(out_type=jax.ShapeDtypeStruct((n, w), rows.dtype), mesh=mesh, scratch_types=[])
    def gather(rows_hbm, idx_hbm, out_hbm):
        def window(idx_vmem, out_vmem):
            pltpu.sync_copy(rows_hbm.at[idx_vmem.at[0]], out_vmem)

        pltpu.emit_pipeline(
            window,
            grid=(n // SC_WINDOW,),
            in_specs=[pl.BlockSpec((1, SC_WINDOW), index_map=lambda i: (0, i))],
            out_specs=[pl.BlockSpec((SC_WINDOW, w), index_map=lambda i: (i, 0))],
            core_axis_name=("c", "s"),
            dimension_semantics=(pltpu.PARALLEL,),
        )(idx_hbm, out_hbm)

    return gather(rows, idx.reshape(1, n))


def _ffn_kernel(xe_ref, wg_ref, wu_ref, wd_ref, ye_ref, wg_scr, wu_scr, wd_scr):
    @pl.when(pl.program_id(1) == 0)
    def _():
        wg_scr[...] = wg_ref[0].astype(BF16)
        wu_scr[...] = wu_ref[0].astype(BF16)
        wd_scr[...] = wd_ref[0].astype(BF16)

    halves = []
    for p in range(PACK_PARTS):
        halves.extend(_unpack_pair(xe_ref[p, 0]))
    xe = jnp.concatenate(halves, axis=1)
    he = (_silu(_dot(xe, wg_scr[...])) * _dot(xe, wu_scr[...])).astype(BF16)
    ye_ref[0] = _dot(he, wd_scr[...]).astype(BF16)


def _ffn(xe, wgate, wup, wdown):
    _, n_e, rows, w = xe.shape
    _, d, f = wgate.shape
    tile = min(FFN_ROWS, rows)
    return pl.pallas_call(
        _ffn_kernel,
        grid=(n_e, rows // tile),
        in_specs=[pl.BlockSpec((PACK_PARTS, 1, tile, w), lambda e, m: (0, e, m, 0)),
                  pl.BlockSpec((1, d, f), lambda e, m: (e, 0, 0)),
                  pl.BlockSpec((1, d, f), lambda e, m: (e, 0, 0)),
                  pl.BlockSpec((1, f, d), lambda e, m: (e, 0, 0))],
        out_specs=pl.BlockSpec((1, tile, d), lambda e, m: (e, m, 0)),
        out_shape=jax.ShapeDtypeStruct((n_e, rows, d), BF16),
        scratch_shapes=[pltpu.VMEM((d, f), BF16), pltpu.VMEM((d, f), BF16), pltpu.VMEM((f, d), BF16)],
        compiler_params=_params(("parallel", "arbitrary")),
        name="expert_ffn",
    )(xe, wgate, wup, wdown)


def _combine_kernel(cap, alpha, rank_ref, aff_ref, ye_ref, x1_ref, g2_ref, lng_ref, lnb_ref, *rest):
    o_ref = rest[-1]
    n_e = rank_ref.shape[1]
    slot = lax.broadcasted_iota(jnp.int32, (cap, OUT_ROWS), 0)
    acc = None
    for e in range(n_e):
        hit = slot == rank_ref[0, e:e + 1, :]
        gated = jnp.where(hit, aff_ref[0, e:e + 1, :], 0.0).astype(BF16)
        part = _dot_tn(gated, ye_ref[e, 0])
        acc = part if acc is None else acc + part
    z = alpha * x1_ref[0] + g2_ref[0] * acc
    o_ref[0] = _norm(z) * lng_ref[...] + lnb_ref[...]


def _combine(alpha, rank, aff, ye, x1, g2, ln_g, ln_b, cap, b_total, b0, earlier):
    b, l, d = x1.shape
    n_e = rank.shape[1]
    row = pl.BlockSpec((1, n_e, OUT_ROWS), lambda i, t: (i, 0, t))
    vec = pl.BlockSpec((1, d), lambda i, t: (0, 0))
    in_specs = [row, row,
                pl.BlockSpec((n_e, 1, cap, d), lambda i, t: (0, i, 0, 0)),
                pl.BlockSpec((1, OUT_ROWS, d), lambda i, t: (i, t, 0)),
                pl.BlockSpec((1, 1, d), lambda i, t: (i, 0, 0)), vec, vec]
    args = [rank, aff, ye.reshape(n_e, b, cap, d), x1, g2, ln_g, ln_b]
    aliases = {}
    if earlier is not None:
        in_specs.append(pl.BlockSpec(memory_space=pl.ANY))
        aliases = {len(args): 0}
        args.append(earlier)
    return pl.pallas_call(
        functools.partial(_combine_kernel, cap, alpha),
        grid=(b, l // OUT_ROWS),
        in_specs=in_specs,
        out_specs=pl.BlockSpec((1, OUT_ROWS, d), lambda i, t: (i + b0, t, 0)),
        out_shape=jax.ShapeDtypeStruct((b_total, l, d), F32),
        input_output_aliases=aliases,
        compiler_params=_params(("parallel", "arbitrary")),
        name="expert_combine",
    )(*args)


def _rope_tables(l):
    n_axis = RET_DK // 4
    freqs = ROPE_BASE ** (-np.arange(n_axis, dtype=np.float64) / n_axis)
    pos = np.arange(l)
    ang = np.concatenate([(pos // GRID_W)[:, None] * freqs, (pos % GRID_W)[:, None] * freqs], axis=-1)
    cos, sin = np.cos(ang), np.sin(ang)
    return (np.concatenate([cos, cos], axis=-1).astype(np.float32),
            np.concatenate([-sin, sin], axis=-1).astype(np.float32))


def kernel(x, c, ctx, c_ctx, w_ada, b_ada, w_in, conv_w, conv_b, conv_ln_g, conv_ln_b, w_conv_out,
           log_decay_f, log_decay_b, ret_gn_g, w_ret_out, w_out, ln1_g, ln1_b,
           w_router, w_gate, w_up, w_down, ln2_g, ln2_b):
    depth = w_ada.shape[0]
    assert depth == 1, "single trunk layer"
    b, l, d = x.shape
    alpha = (2.0 * depth) ** 0.25
    cap = EC_FACTOR * l // N_EXPERTS
    u_end = 2 * CONV_DIM
    q_end = u_end + QK_DIM
    k_end = q_end + QK_DIM
    v_end = k_end + V_DIM
    row = lambda a: a.reshape(1, -1)

    n_mod = b + 1
    pad = (-n_mod) % 8
    cc = jnp.concatenate([c, c_ctx[None], jnp.zeros((pad, d), F32)], axis=0)
    mod = _ada(cc, w_ada[0], row(b_ada[0]))
    sh1, sc1, g1, sh2, sc2, g2 = [m.reshape(b, 1, d) for m in jnp.split(mod[:b], 6, axis=-1)]
    csh1, csc1 = mod[b:b + 1, :d], mod[b:b + 1, d:2 * d]

    perm = np.concatenate([np.arange(0, RET_DK, 2), np.arange(1, RET_DK, 2)])
    perm = (np.arange(RET_HEADS)[:, None] * RET_DK + perm[None, :]).reshape(-1)
    w = w_in[0]
    w_kvu = jnp.concatenate([w[:, q_end:k_end][:, perm], w[:, k_end:v_end], w[:, :u_end]], axis=1).astype(BF16)
    wq = w[:, u_end:q_end][:, perm].astype(BF16)
    wg = w[:, v_end:].astype(BF16)

    cos_t, sin_t = _rope_tables(l)
    cq, sq = jnp.asarray(cos_t), jnp.asarray(sin_t)
    k_scale = RET_DK ** -0.5
    ck, sk = jnp.asarray(cos_t * k_scale), jnp.asarray(sin_t * k_scale)

    lgf, lgb = log_decay_f[0], log_decay_b[0]
    sf0, sb0 = _ctx_states(lgf, lgb, ctx, csh1, csc1, w_kvu)
    cw8 = jnp.repeat(conv_w[0], 8, axis=0)
    wco, wro, wo = w_conv_out[0].astype(BF16), w_ret_out[0].astype(BF16), w_out[0].astype(BF16)
    wge, wue, wde = w_gate[0], w_up[0], w_down[0]
    wr_t = w_router[0].T
    groups = BATCH_GROUPS if b % BATCH_GROUPS == 0 else 1
    nb = b // groups
    out = None
    for gi in range(groups):
        b0 = gi * nb
        part = lambda a: a[b0:b0 + nb]
        h, k, v, ya, sf, ub = _proj(lgf, lgb, x, b0, part(sh1), part(sc1), w_kvu, ck, sk, part(sf0),
                                    cw8, row(conv_b[0]), row(conv_ln_g[0]), row(conv_ln_b[0]))
        x1, hp, lt = _mixer(alpha, lgf, lgb, x, b0, h, k, v, sf, ub, part(sb0), ya, wq, wg, wco, wro, wo,
                            row(ret_gn_g[0]), part(g1), part(sh2), part(sc2), row(ln1_g[0]), row(ln1_b[0]),
                            wr_t, cq, sq)
        rank, aff, slots = _route(lt, cap)
        part_rows = nb * l
        by_expert = slots.reshape(nb, N_EXPERTS, cap).transpose(1, 0, 2).reshape(-1)
        ids = jnp.concatenate([by_expert + p * part_rows for p in range(PACK_PARTS)])
        xe = _sc_gather(hp.reshape(PACK_PARTS * part_rows, hp.shape[-1]), ids)
        ye = _ffn(xe.reshape(PACK_PARTS, N_EXPERTS, nb * cap, hp.shape[-1]), wge, wue, wde)
        out = _combine(alpha, rank, aff, ye, x1, part(g2), row(ln2_g[0]), row(ln2_b[0]), cap, b, b0, out)
    return out
```
